```python
import jax, jax.numpy as jnp
from jax import lax
import numpy as np

D_MODEL = 1024
BATCH = 4
SEQ = 4096
DEPTH = 1

CHUNK = 64
GMLP_BLOCK = 128
A_WIDTH = D_MODEL
A_GROUPS = 8
A_GROUP_DIM = A_WIDTH // A_GROUPS
POOL_WINDOWS = (2, 4, 8, 16)
B_WIDTH = D_MODEL
B_GROUPS = len(POOL_WINDOWS)
B_GROUP_DIM = B_WIDTH // B_GROUPS
IN_WIDTH = 2 * A_WIDTH + B_WIDTH
D_FF = ((8 * D_MODEL // 3) + 127) // 128 * 128
CONV_WIDTH = 3
N_ADA = 6
EPS = 1e-6

kernel_name = "chunk_causal_gmlp_pool_hybrid_adaln"


def rmsnorm(x, g):
    xf = x.astype(jnp.float32)
    y = xf * lax.rsqrt(jnp.mean(xf * xf, axis=-1, keepdims=True) + EPS)
    return (y * g.astype(jnp.float32)).astype(x.dtype)


def layernorm(x, g, b):
    xf = x.astype(jnp.float32)
    mu = jnp.mean(xf, axis=-1, keepdims=True)
    var = jnp.mean(jnp.square(xf - mu), axis=-1, keepdims=True)
    y = (xf - mu) * lax.rsqrt(var + EPS)
    return (y * g.astype(jnp.float32) + b.astype(jnp.float32)).astype(x.dtype)


def chunk_causal_block_mask():
    p = jnp.arange(GMLP_BLOCK)
    return (p[None, :] // CHUNK) <= (p[:, None] // CHUNK)


def spatial_gating(v, w_s, b_s):
    bsz, s_len, _ = v.shape
    vb = v.reshape(bsz, s_len // GMLP_BLOCK, GMLP_BLOCK, A_GROUPS, A_GROUP_DIM)
    w = jnp.where(chunk_causal_block_mask()[None], w_s, jnp.zeros((), w_s.dtype))
    s = jnp.einsum('gpq,bnqgd->bnpgd', w, vb) + b_s.T[None, None, :, :, None]
    return s.reshape(bsz, s_len, A_WIDTH)


def multiscale_pool(hb, w_pool, b_pool, scale):
    bsz, s_len, _ = hb.shape
    hf = hb.astype(jnp.float32).reshape(bsz, s_len, B_GROUPS, B_GROUP_DIM)
    cs = jnp.cumsum(hf, axis=1)
    t = jnp.arange(s_len)
    outs = []
    for gi, w in enumerate(POOL_WINDOWS):
        csg = cs[:, :, gi]
        lo = jnp.pad(csg[:, :s_len - w], ((0, 0), (w, 0), (0, 0)))
        count = jnp.minimum(t + 1, w).astype(jnp.float32)
        mean = (csg - lo) / count[None, :, None]
        outs.append(mean - hf[:, :, gi])
    pooled = jnp.stack(outs, axis=2)
    mixed = jnp.einsum('bsgc,gcd->bsgd', pooled, w_pool.astype(jnp.float32)) + b_pool.astype(jnp.float32)
    y = mixed.reshape(bsz, s_len, B_WIDTH) * scale.astype(jnp.float32)
    return y.astype(hb.dtype)


def causal_dwconv(z, w, b):
    s_len = z.shape[1]
    zp = jnp.pad(z, ((0, 0), (CONV_WIDTH - 1, 0), (0, 0)))
    out = b
    for k in range(CONV_WIDTH):
        out = out + zp[:, k:k + s_len] * w[k]
    return out


def setup_inputs(seed: int = 0) -> dict:
    key = jax.random.key(seed)
    ks = jax.random.split(key, 24)
    L, D = DEPTH, D_MODEL
    f32 = jnp.float32

    def nrm(k, shape, s):
        return jax.random.normal(k, shape, f32) * s

    return {
        "x": jax.random.normal(ks[0], (BATCH, SEQ, D), f32),
        "c": jax.random.normal(ks[1], (BATCH, D), f32),
        "w_ada": nrm(ks[2], (L, D, N_ADA * D), 0.5 * D ** -0.5),
        "b_ada": nrm(ks[3], (L, N_ADA * D), 0.02),
        "g_norm1": 1.0 + nrm(ks[4], (L, D), 0.05),
        "w_in": nrm(ks[5], (L, D, IN_WIDTH), D ** -0.5),
        "ln_v_g": 1.0 + nrm(ks[6], (L, A_WIDTH), 0.05),
        "ln_v_b": nrm(ks[7], (L, A_WIDTH), 0.02),
        "w_spatial": nrm(ks[8], (L, A_GROUPS, GMLP_BLOCK, GMLP_BLOCK), GMLP_BLOCK ** -0.5),
        "b_spatial": 1.0 + nrm(ks[9], (L, A_GROUPS, GMLP_BLOCK), 0.05),
        "w_pool": nrm(ks[10], (L, B_GROUPS, B_GROUP_DIM, B_GROUP_DIM), B_GROUP_DIM ** -0.5),
        "b_pool": nrm(ks[11], (L, B_GROUPS, B_GROUP_DIM), 0.02),
        "pool_scale": 1.0 + nrm(ks[12], (L, B_WIDTH), 0.05),
        "w_proj_a": nrm(ks[13], (L, A_WIDTH, D), A_WIDTH ** -0.5),
        "w_proj_b": nrm(ks[14], (L, B_WIDTH, D), B_WIDTH ** -0.5),
        "w_gate": nrm(ks[15], (L, D, 2 * D), D ** -0.5),
        "b_gate": nrm(ks[16], (L, 2 * D), 0.02),
        "w_out": nrm(ks[17], (L, D, D), D ** -0.5),
        "g_norm2": 1.0 + nrm(ks[18], (L, D), 0.05),
        "w_up": nrm(ks[19], (L, D, 2 * D_FF), D ** -0.5),
        "conv_w": nrm(ks[20], (L, CONV_WIDTH, 2 * D_FF), CONV_WIDTH ** -0.5),
        "conv_b": nrm(ks[21], (L, 2 * D_FF), 0.02),
        "w_down": nrm(ks[22], (L, D_FF, D), D_FF ** -0.5),
        "g_final": 1.0 + nrm(ks[23], (D,), 0.05),
    }


def reference(x, c, w_ada, b_ada, g_norm1, w_in, ln_v_g, ln_v_b, w_spatial, b_spatial,
              w_pool, b_pool, pool_scale, w_proj_a, w_proj_b, w_gate, b_gate, w_out,
              g_norm2, w_up, conv_w, conv_b, w_down, g_final):
    for l in range(DEPTH):
        mod = jax.nn.silu(c) @ w_ada[l] + b_ada[l]
        sh1, sc1, gt1, sh2, sc2, gt2 = [m[:, None, :] for m in jnp.split(mod, N_ADA, axis=-1)]

        h = rmsnorm(x, g_norm1[l]) * (1.0 + sc1) + sh1
        z = h @ w_in[l]
        za = jax.nn.gelu(z[..., :2 * A_WIDTH], approximate=False)
        u, v = za[..., :A_WIDTH], za[..., A_WIDTH:]
        v = layernorm(v, ln_v_g[l], ln_v_b[l])
        y_a = u * spatial_gating(v, w_spatial[l], b_spatial[l])
        y_b = multiscale_pool(z[..., 2 * A_WIDTH:], w_pool[l], b_pool[l], pool_scale[l])
        gates = jax.nn.sigmoid(h @ w_gate[l] + b_gate[l])
        g_a, g_b = gates[..., :D_MODEL], gates[..., D_MODEL:]
        merged = g_a * (y_a @ w_proj_a[l]) + g_b * (y_b @ w_proj_b[l])
        x = x + gt1 * (merged @ w_out[l])

        h2 = rmsnorm(x, g_norm2[l]) * (1.0 + sc2) + sh2
        up = causal_dwconv(h2 @ w_up[l], conv_w[l], conv_b[l])
        f = jax.nn.silu(up[..., :D_FF]) * up[..., D_FF:]
        x = x + gt2 * (f @ w_down[l])
    return rmsnorm(x, g_final)
```

```python
import functools
import math

import jax
import jax.numpy as jnp
from jax import lax
from jax.experimental import pallas as pl
from jax.experimental.pallas import tpu as pltpu

EPS = 1e-6
CHUNK = 64
GMLP_BLOCK = 128
A_GROUPS = 8
POOL_WINDOWS = (2, 4, 8, 16)
CONV_WIDTH = 3
N_ADA = 6

SUBLANES = 8
POOL_HALO = 16
CONV_HALO = SUBLANES
TOKEN_TILE = 256
FF_CHUNK = 256
ADA_TILE = 1024
VMEM_LIMIT_BYTES = 56 * 1024 * 1024

BF16 = jnp.bfloat16
F32 = jnp.float32


def _dot(a, b):
    return jnp.dot(a, b, preferred_element_type=F32)


def _gelu(x):
    return 0.5 * x * (1.0 + lax.erf(x * math.sqrt(0.5)))


def _rms_scale(x):
    ms = jnp.mean(x * x, axis=-1, keepdims=True)
    return x * lax.rsqrt(ms + EPS)


def _ada_kernel(c_ref, w_ref, b_ref, o_ref):
    c = c_ref[...]
    s = c * jax.nn.sigmoid(c)
    o_ref[...] = _dot(s.astype(BF16), w_ref[...].astype(BF16)) + b_ref[...]


def _ada_call(c, w_ada, b_ada):
    bsz, d = c.shape
    n = w_ada.shape[1]
    return pl.pallas_call(
        _ada_kernel,
        grid=(n // ADA_TILE,),
        in_specs=[
            pl.BlockSpec((bsz, d), lambda j: (0, 0)),
            pl.BlockSpec((d, ADA_TILE), lambda j: (0, j)),
            pl.BlockSpec((1, ADA_TILE), lambda j: (0, j)),
        ],
        out_specs=pl.BlockSpec((bsz, ADA_TILE), lambda j: (0, j)),
        out_shape=jax.ShapeDtypeStruct((bsz, n), F32),
        compiler_params=pltpu.CompilerParams(dimension_semantics=("arbitrary",)),
        name="ada_mod",
    )(c, w_ada, b_ada.reshape(1, n))


def _mixer_kernel(x_ref, mod_ref, g1_ref, w_in_ref, w_gate_ref, b_gate_ref, lng_ref, lnb_ref,
                  wsp_ref, bsp_ref, wpool_ref, bpool_ref, pscale_ref, wpa_ref, wpb_ref, wout_ref,
                  o_ref, pool_carry_ref):
    ts, d = x_ref.shape[1], x_ref.shape[2]
    j = pl.program_id(1)

    @pl.when(j == 0)
    def _():
        pool_carry_ref[...] = jnp.zeros_like(pool_carry_ref)

    x = x_ref[0]
    sh1 = mod_ref[0, 0:1, :]
    sc1 = mod_ref[0, 1:2, :]
    gt1 = mod_ref[0, 2:3, :]
    h = _rms_scale(x) * (g1_ref[...] * (1.0 + sc1)) + sh1
    hb = h.astype(BF16)

    z = _dot(hb, w_in_ref[...])
    a_w = d
    u = _gelu(z[:, :a_w])
    v = _gelu(z[:, a_w:2 * a_w])
    mu = jnp.mean(v, axis=-1, keepdims=True)
    vc = v - mu
    var = jnp.mean(vc * vc, axis=-1, keepdims=True)
    vn = (vc * lax.rsqrt(var + EPS)) * lng_ref[...] + lnb_ref[...]
    vnb = vn.astype(BF16)

    gd = a_w // A_GROUPS
    p = lax.broadcasted_iota(jnp.int32, (GMLP_BLOCK, GMLP_BLOCK), 0)
    q = lax.broadcasted_iota(jnp.int32, (GMLP_BLOCK, GMLP_BLOCK), 1)
    allowed = (q // CHUNK) <= (p // CHUNK)
    w_masked = [jnp.where(allowed, wsp_ref[g], 0.0).astype(BF16) for g in range(A_GROUPS)]
    bias_map = bsp_ref[...]
    s_rows = []
    for n in range(ts // GMLP_BLOCK):
        r0 = n * GMLP_BLOCK
        cols = [_dot(w_masked[g], vnb[r0:r0 + GMLP_BLOCK, g * gd:(g + 1) * gd]) for g in range(A_GROUPS)]
        s_rows.append(jnp.concatenate(cols, axis=1) + bias_map)
    s = jnp.concatenate(s_rows, axis=0)
    y_a = u * s

    hbp = z[:, 2 * a_w:]
    ext = jnp.concatenate([pool_carry_ref[...], hbp], axis=0)
    pool_carry_ref[...] = hbp[ts - POOL_HALO:, :]
    bgd = hbp.shape[1] // len(POOL_WINDOWS)
    t = j * ts + lax.broadcasted_iota(jnp.int32, (ts, bgd), 0)
    yb_cols = []
    for gi, w in enumerate(POOL_WINDOWS):
        e = ext[:, gi * bgd:(gi + 1) * bgd]
        acc, span = e, 1
        while span < w:
            acc = acc + pltpu.roll(acc, span, 0)
            span *= 2
        win = acc[POOL_HALO:, :]
        cur = e[POOL_HALO:, :]
        count = jnp.minimum(t + 1, w).astype(F32)
        pooled = win / count - cur
        mixed = _dot(pooled.astype(BF16), wpool_ref[gi]) + bpool_ref[gi:gi + 1, :]
        yb_cols.append(mixed)
    y_b = jnp.concatenate(yb_cols, axis=1) * pscale_ref[...]

    gates = jax.nn.sigmoid(_dot(hb, w_gate_ref[...]) + b_gate_ref[...])
    merged = (gates[:, :d] * _dot(y_a.astype(BF16), wpa_ref[...])
              + gates[:, d:] * _dot(y_b.astype(BF16), wpb_ref[...]))
    o_ref[0] = x + gt1 * _dot(merged.astype(BF16), wout_ref[...])


def _resident(shape):
    nd = len(shape)
    return pl.BlockSpec(shape, lambda b, j: (0,) * nd, pipeline_mode=pl.Buffered(1))


def _mixer_call(x, mod, g1, w_in, w_gate, b_gate, ln_g, ln_b, w_sp, bias_map, w_pool, b_pool,
                pool_scale, w_pa, w_pb, w_out):
    bsz, s_len, d = x.shape
    ts = TOKEN_TILE
    row = lambda a: a.reshape(1, -1)
    operands = [
        (x, pl.BlockSpec((1, ts, d), lambda b, j: (b, j, 0))),
        (mod, pl.BlockSpec((1, N_ADA, d), lambda b, j: (b, 0, 0))),
    ]
    for a in (row(g1), w_in, w_gate, row(b_gate), row(ln_g), row(ln_b), w_sp, bias_map, w_pool,
              b_pool, row(pool_scale), w_pa, w_pb, w_out):
        operands.append((a, _resident(a.shape)))
    return pl.pallas_call(
        _mixer_kernel,
        grid=(bsz, s_len // ts),
        in_specs=[spec for _, spec in operands],
        out_specs=pl.BlockSpec((1, ts, d), lambda b, j: (b, j, 0)),
        out_shape=jax.ShapeDtypeStruct(x.shape, F32),
        scratch_shapes=[pltpu.VMEM((POOL_HALO, w_pool.shape[0] * w_pool.shape[1]), F32)],
        compiler_params=pltpu.CompilerParams(
            dimension_semantics=("arbitrary", "arbitrary"), vmem_limit_bytes=VMEM_LIMIT_BYTES),
        name="mixer",
    )(*[a for a, _ in operands])


def _channel_kernel(x_ref, mod_ref, g2_ref, w_up_ref, cw_ref, cb_ref, w_down_ref, gf_ref,
                    o_ref, conv_carry_ref, *, final_norm):
    ts, d = x_ref.shape[1], x_ref.shape[2]
    d_ff = w_down_ref.shape[0]
    j = pl.program_id(1)

    @pl.when(j == 0)
    def _():
        conv_carry_ref[...] = jnp.zeros_like(conv_carry_ref)

    x = x_ref[0]
    sh2 = mod_ref[0, 3:4, :]
    sc2 = mod_ref[0, 4:5, :]
    gt2 = mod_ref[0, 5:6, :]
    h2 = _rms_scale(x) * (g2_ref[...] * (1.0 + sc2)) + sh2
    h2b = h2.astype(BF16)

    def conv_cols(c0):
        cols = pl.ds(c0, FF_CHUNK)
        pre = _dot(h2b, w_up_ref[:, cols])
        ext = jnp.concatenate([conv_carry_ref[:, cols], pre], axis=0)
        conv_carry_ref[:, cols] = pre[ts - CONV_HALO:, :]
        out = cb_ref[:, cols] + ext * cw_ref[CONV_WIDTH - 1:CONV_WIDTH, cols]
        for k in range(CONV_WIDTH - 1):
            shift = CONV_WIDTH - 1 - k
            out = out + pltpu.roll(ext, shift, 0) * cw_ref[k:k + 1, cols]
        return out[CONV_HALO:, :]

    acc = jnp.zeros((ts, d), F32)
    for c in range(d_ff // FF_CHUNK):
        c0 = c * FF_CHUNK
        gate = conv_cols(c0)
        val = conv_cols(d_ff + c0)
        f = gate * jax.nn.sigmoid(gate) * val
        acc = acc + _dot(f.astype(BF16), w_down_ref[c0:c0 + FF_CHUNK, :])
    x2 = x + gt2 * acc
    if final_norm:
        x2 = _rms_scale(x2) * gf_ref[...]
    o_ref[0] = x2


def _channel_call(x, mod, g2, w_up, conv_w, conv_b, w_down, g_final, final_norm):
    bsz, s_len, d = x.shape
    ts = TOKEN_TILE
    row = lambda a: a.reshape(1, -1)
    operands = [
        (x, pl.BlockSpec((1, ts, d), lambda b, j: (b, j, 0))),
        (mod, pl.BlockSpec((1, N_ADA, d), lambda b, j: (b, 0, 0))),
    ]
    for a in (row(g2), w_up, conv_w, row(conv_b), w_down, row(g_final)):
        operands.append((a, _resident(a.shape)))
    return pl.pallas_call(
        functools.partial(_channel_kernel, final_norm=final_norm),
        grid=(bsz, s_len // ts),
        in_specs=[spec for _, spec in operands],
        out_specs=pl.BlockSpec((1, ts, d), lambda b, j: (b, j, 0)),
        out_shape=jax.ShapeDtypeStruct(x.shape, F32),
        scratch_shapes=[pltpu.VMEM((CONV_HALO, w_up.shape[1]), F32)],
        compiler_params=pltpu.CompilerParams(
            dimension_semantics=("arbitrary", "arbitrary"), vmem_limit_bytes=VMEM_LIMIT_BYTES),
        name="channel",
    )(*[a for a, _ in operands])


def kernel(x, c, w_ada, b_ada, g_norm1, w_in, ln_v_g, ln_v_b, w_spatial, b_spatial, w_pool, b_pool,
           pool_scale, w_proj_a, w_proj_b, w_gate, b_gate, w_out, g_norm2, w_up, conv_w, conv_b,
           w_down, g_final):
    depth = w_ada.shape[0]
    bsz, s_len, d = x.shape
    assert s_len % TOKEN_TILE == 0 and TOKEN_TILE % GMLP_BLOCK == 0
    assert w_down.shape[1] % FF_CHUNK == 0 and w_ada.shape[2] % ADA_TILE == 0
    gd = w_in.shape[2] // 3 // A_GROUPS
    for l in range(depth):
        mod = _ada_call(c, w_ada[l], b_ada[l]).reshape(bsz, N_ADA, d)
        bias_map = jnp.repeat(b_spatial[l].T, gd, axis=1)
        x = _mixer_call(
            x, mod, g_norm1[l], w_in[l].astype(BF16), w_gate[l].astype(BF16), b_gate[l], ln_v_g[l],
            ln_v_b[l], w_spatial[l], bias_map, w_pool[l].astype(BF16), b_pool[l], pool_scale[l],
            w_proj_a[l].astype(BF16), w_proj_b[l].astype(BF16), w_out[l].astype(BF16))
        x = _channel_call(
            x, mod, g_norm2[l], w_up[l].astype(BF16), conv_w[l], conv_b[l], w_down[l].astype(BF16),
            g_final, final_norm=(l == depth - 1))
    return x
```

```python
import functools
import math

import jax
import jax.numpy as jnp
from jax import lax
from jax.experimental import pallas as pl
from jax.experimental.pallas import tpu as pltpu

EPS = 1e-6
CHUNK = 64
GMLP_BLOCK = 128
A_GROUPS = 8
POOL_WINDOWS = (2, 4, 8, 16)
CONV_WIDTH = 3
N_ADA = 6

SUBLANES = 8
POOL_HALO = 16
CONV_HALO = SUBLANES
TOKEN_TILE = 256
FF_CHUNK = 256
ADA_TILE = 1024
VMEM_LIMIT_BYTES = 56 * 1024 * 1024

BF16 = jnp.bfloat16
F32 = jnp.float32


def _dot(a, b):
    return jnp.dot(a, b, preferred_element_type=F32)


def _gelu(x):
    return 0.5 * x * (1.0 + lax.erf(x * math.sqrt(0.5)))


def _rms_scale(x):
    ms = jnp.mean(x * x, axis=-1, keepdims=True)
    return x * lax.rsqrt(ms + EPS)


def _ada_kernel(c_ref, w_ref, b_ref, o_ref):
    c = c_ref[...]
    s = c * jax.nn.sigmoid(c)
    o_ref[...] = _dot(s.astype(BF16), w_ref[...].astype(BF16)) + b_ref[...]


def _ada_call(c, w_ada, b_ada):
    bsz, d = c.shape
    n = w_ada.shape[1]
    return pl.pallas_call(
        _ada_kernel,
        grid=(n // ADA_TILE,),
        in_specs=[
            pl.BlockSpec((bsz, d), lambda j: (0, 0)),
            pl.BlockSpec((d, ADA_TILE), lambda j: (0, j)),
            pl.BlockSpec((1, ADA_TILE), lambda j: (0, j)),
        ],
        out_specs=pl.BlockSpec((bsz, ADA_TILE), lambda j: (0, j)),
        out_shape=jax.ShapeDtypeStruct((bsz, n), F32),
        compiler_params=pltpu.CompilerParams(dimension_semantics=("arbitrary",)),
        name="ada_mod",
    )(c, w_ada, b_ada.reshape(1, n))


def _mixer_kernel(x_ref, mod_ref, g1_ref, w_in_ref, w_gate_ref, b_gate_ref, lng_ref, lnb_ref,
                  wsp_ref, bsp_ref, wpool_ref, bpool_ref, pscale_ref, wpa_ref, wpb_ref, wout_ref,
                  o_ref, pool_carry_ref):
    ts, d = x_ref.shape[1], x_ref.shape[2]
    j = pl.program_id(1)

    @pl.when(j == 0)
    def _():
        pool_carry_ref[...] = jnp.zeros_like(pool_carry_ref)

    x = x_ref[0]
    sh1 = mod_ref[0, 0:1, :]
    sc1 = mod_ref[0, 1:2, :]
    gt1 = mod_ref[0, 2:3, :]
    h = _rms_scale(x) * (g1_ref[...] * (1.0 + sc1)) + sh1
    hb = h.astype(BF16)

    z = _dot(hb, w_in_ref[...])
    a_w = d
    u = _gelu(z[:, :a_w])
    v = _gelu(z[:, a_w:2 * a_w])
    mu = jnp.mean(v, axis=-1, keepdims=True)
    vc = v - mu
    var = jnp.mean(vc * vc, axis=-1, keepdims=True)
    vn = (vc * lax.rsqrt(var + EPS)) * lng_ref[...] + lnb_ref[...]
    vnb = vn.astype(BF16)

    gd = a_w // A_GROUPS
    p = lax.broadcasted_iota(jnp.int32, (GMLP_BLOCK, GMLP_BLOCK), 0)
    q = lax.broadcasted_iota(jnp.int32, (GMLP_BLOCK, GMLP_BLOCK), 1)
    allowed = (q // CHUNK) <= (p // CHUNK)
    w_masked = [jnp.where(allowed, wsp_ref[g], 0.0).astype(BF16) for g in range(A_GROUPS)]
    bias_map = bsp_ref[...]
    s_rows = []
    for n in range(ts // GMLP_BLOCK):
        r0 = n * GMLP_BLOCK
        cols = [_dot(w_masked[g], vnb[r0:r0 + GMLP_BLOCK, g * gd:(g + 1) * gd]) for g in range(A_GROUPS)]
        s_rows.append(jnp.concatenate(cols, axis=1) + bias_map)
    s = jnp.concatenate(s_rows, axis=0)
    y_a = u * s

    hbp = z[:, 2 * a_w:]
    ext = jnp.concatenate([pool_carry_ref[...], hbp], axis=0)
    pool_carry_ref[...] = hbp[ts - POOL_HALO:, :]
    bgd = hbp.shape[1] // len(POOL_WINDOWS)
    t = j * ts + lax.broadcasted_iota(jnp.int32, (ts, bgd), 0)
    yb_cols = []
    for gi, w in enumerate(POOL_WINDOWS):
        e = ext[:, gi * bgd:(gi + 1) * bgd]
        acc, span = e, 1
        while span < w:
            acc = acc + pltpu.roll(acc, span, 0)
            span *= 2
        win = acc[POOL_HALO:, :]
        cur = e[POOL_HALO:, :]
        count = jnp.minimum(t + 1, w).astype(F32)
        pooled = win / count - cur
        mixed = _dot(pooled.astype(BF16), wpool_ref[gi]) + bpool_ref[gi:gi + 1, :]
        yb_cols.append(mixed)
    y_b = jnp.concatenate(yb_cols, axis=1) * pscale_ref[...]

    gates = jax.nn.sigmoid(_dot(hb, w_gate_ref[...]) + b_gate_ref[...])
    merged = (gates[:, :d] * _dot(y_a.astype(BF16), wpa_ref[...])
              + gates[:, d:] * _dot(y_b.astype(BF16), wpb_ref[...]))
    o_ref[0] = x + gt1 * _dot(merged.astype(BF16), wout_ref[...])


def _resident(shape):
    nd = len(shape)
    return pl.BlockSpec(shape, lambda b, j: (0,) * nd, pipeline_mode=pl.Buffered(1))


def _mixer_call(x, mod, g1, w_in, w_gate, b_gate, ln_g, ln_b, w_sp, bias_map, w_pool, b_pool,
                pool_scale, w_pa, w_pb, w_out):
    bsz, s_len, d = x.shape
    ts = TOKEN_TILE
    row = lambda a: a.reshape(1, -1)
    operands = [
        (x, pl.BlockSpec((1, ts, d), lambda b, j: (b, j, 0))),
        (mod, pl.BlockSpec((1, N_ADA, d), lambda b, j: (b, 0, 0))),
    ]
    for a in (row(g1), w_in, w_gate, row(b_gate), row(ln_g), row(ln_b), w_sp, bias_map, w_pool,
              b_pool, row(pool_scale), w_pa, w_pb, w_out):
        operands.append((a, _resident(a.shape)))
    return pl.pallas_call(
        _mixer_kernel,
        grid=(bsz, s_len // ts),
        in_specs=[spec for _, spec in operands],
        out_specs=pl.BlockSpec((1, ts, d), lambda b, j: (b, j, 0)),
        out_shape=jax.ShapeDtypeStruct(x.shape, F32),
        scratch_shapes=[pltpu.VMEM((POOL_HALO, w_pool.shape[0] * w_pool.shape[1]), F32)],
        compiler_params=pltpu.CompilerParams(
            dimension_semantics=("arbitrary", "arbitrary"), vmem_limit_bytes=VMEM_LIMIT_BYTES),
        name="mixer",
    )(*[a for a, _ in operands])


def _channel_kernel(x_ref, mod_ref, g2_ref, w_up_ref, cw_ref, cb_ref, w_down_ref, gf_ref,
                    o_ref, conv_carry_ref, *, final_norm):
    ts, d = x_ref.shape[1], x_ref.shape[2]
    d_ff = w_down_ref.shape[0]
    n_chunks = d_ff // FF_CHUNK
    j = pl.program_id(1)

    @pl.when(j == 0)
    def _():
        conv_carry_ref[...] = jnp.zeros_like(conv_carry_ref)

    x = x_ref[0]
    sh2 = mod_ref[0, 3:4, :]
    sc2 = mod_ref[0, 4:5, :]
    gt2 = mod_ref[0, 5:6, :]
    h2 = _rms_scale(x) * (g2_ref[...] * (1.0 + sc2)) + sh2
    h2b = h2.astype(BF16)

    def up_project(c):
        return [_dot(h2b, w_up_ref[:, pl.ds(c0, FF_CHUNK)]) for c0 in (c * FF_CHUNK, d_ff + c * FF_CHUNK)]

    def conv_cols(pre, c0):
        cols = pl.ds(c0, FF_CHUNK)
        ext = jnp.concatenate([conv_carry_ref[:, cols], pre], axis=0)
        conv_carry_ref[:, cols] = pre[ts - CONV_HALO:, :]
        out = cb_ref[:, cols]
        for k in range(CONV_WIDTH):
            shift = CONV_WIDTH - 1 - k
            tap = pltpu.roll(ext, shift, 0) if shift else ext
            out = out + tap * cw_ref[k:k + 1, cols]
        return out[CONV_HALO:, :]

    nxt = up_project(0)
    acc = jnp.zeros((ts, d), F32)
    for c in range(n_chunks):
        cur = nxt
        if c + 1 < n_chunks:
            nxt = up_project(c + 1)
        gate = conv_cols(cur[0], c * FF_CHUNK)
        val = conv_cols(cur[1], d_ff + c * FF_CHUNK)
        f = gate * jax.nn.sigmoid(gate) * val
        acc = acc + _dot(f.astype(BF16), w_down_ref[c * FF_CHUNK:(c + 1) * FF_CHUNK, :])
    x2 = x + gt2 * acc
    if final_norm:
        x2 = _rms_scale(x2) * gf_ref[...]
    o_ref[0] = x2


def _channel_call(x, mod, g2, w_up, conv_w, conv_b, w_down, g_final, final_norm):
    bsz, s_len, d = x.shape
    ts = TOKEN_TILE
    row = lambda a: a.reshape(1, -1)
    operands = [
        (x, pl.BlockSpec((1, ts, d), lambda b, j: (b, j, 0))),
        (mod, pl.BlockSpec((1, N_ADA, d), lambda b, j: (b, 0, 0))),
    ]
    for a in (row(g2), w_up, conv_w, row(conv_b), w_down, row(g_final)):
        operands.append((a, _resident(a.shape)))
    return pl.pallas_call(
        functools.partial(_channel_kernel, final_norm=final_norm),
        grid=(bsz, s_len // ts),
        in_specs=[spec for _, spec in operands],
        out_specs=pl.BlockSpec((1, ts, d), lambda b, j: (b, j, 0)),
        out_shape=jax.ShapeDtypeStruct(x.shape, F32),
        scratch_shapes=[pltpu.VMEM((CONV_HALO, w_up.shape[1]), F32)],
        compiler_params=pltpu.CompilerParams(
            dimension_semantics=("arbitrary", "arbitrary"), vmem_limit_bytes=VMEM_LIMIT_BYTES),
        name="channel",
    )(*[a for a, _ in operands])


def kernel(x, c, w_ada, b_ada, g_norm1, w_in, ln_v_g, ln_v_b, w_spatial, b_spatial, w_pool, b_pool,
           pool_scale, w_proj_a, w_proj_b, w_gate, b_gate, w_out, g_norm2, w_up, conv_w, conv_b,
           w_down, g_final):
    depth = w_ada.shape[0]
    bsz, s_len, d = x.shape
    assert s_len % TOKEN_TILE == 0 and TOKEN_TILE % GMLP_BLOCK == 0
    assert w_down.shape[1] % FF_CHUNK == 0 and w_ada.shape[2] % ADA_TILE == 0
    gd = w_in.shape[2] // 3 // A_GROUPS
    for l in range(depth):
        mod = _ada_call(c, w_ada[l], b_ada[l]).reshape(bsz, N_ADA, d)
        bias_map = jnp.repeat(b_spatial[l].T, gd, axis=1)
        x = _mixer_call(
            x, mod, g_norm1[l], w_in[l].astype(BF16), w_gate[l].astype(BF16), b_gate[l], ln_v_g[l],
            ln_v_b[l], w_spatial[l], bias_map, w_pool[l].astype(BF16), b_pool[l], pool_scale[l],
            w_proj_a[l].astype(BF16), w_proj_b[l].astype(BF16), w_out[l].astype(BF16))
        x = _channel_call(
            x, mod, g_norm2[l], w_up[l].astype(BF16), conv_w[l], conv_b[l], w_down[l].astype(BF16),
            g_final, final_norm=(l == depth - 1))
    return x
```

```python
import functools
import math

import jax
import jax.numpy as jnp
from jax import lax
from jax.experimental import pallas as pl
from jax.experimental.pallas import tpu as pltpu

EPS = 1e-6
CHUNK = 64
GMLP_BLOCK = 128
A_GROUPS = 8
POOL_WINDOWS = (2, 4, 8, 16)
CONV_WIDTH = 3
N_ADA = 6

SUBLANES = 8
POOL_HALO = 16
CONV_CARRY_ROWS = (CONV_WIDTH - 1) * SUBLANES
TOKEN_TILE = 256
FF_CHUNK = 256
ADA_TILE = 1024
VMEM_LIMIT_BYTES = 56 * 1024 * 1024

BF16 = jnp.bfloat16
F32 = jnp.float32


def _dot(a, b):
    return jnp.dot(a, b, preferred_element_type=F32)


def _gelu(x):
    return 0.5 * x * (1.0 + lax.erf(x * math.sqrt(0.5)))


def _rms_scale(x):
    ms = jnp.mean(x * x, axis=-1, keepdims=True)
    return x * lax.rsqrt(ms + EPS)


def _ada_kernel(c_ref, w_ref, b_ref, o_ref):
    c = c_ref[...]
    s = c * jax.nn.sigmoid(c)
    o_ref[...] = _dot(s.astype(BF16), w_ref[...].astype(BF16)) + b_ref[...]


def _ada_call(c, w_ada, b_ada):
    bsz, d = c.shape
    n = w_ada.shape[1]
    return pl.pallas_call(
        _ada_kernel,
        grid=(n // ADA_TILE,),
        in_specs=[
            pl.BlockSpec((bsz, d), lambda j: (0, 0)),
            pl.BlockSpec((d, ADA_TILE), lambda j: (0, j)),
            pl.BlockSpec((1, ADA_TILE), lambda j: (0, j)),
        ],
        out_specs=pl.BlockSpec((bsz, ADA_TILE), lambda j: (0, j)),
        out_shape=jax.ShapeDtypeStruct((bsz, n), F32),
        compiler_params=pltpu.CompilerParams(dimension_semantics=("arbitrary",)),
        name="ada_mod",
    )(c, w_ada, b_ada.reshape(1, n))


def _mixer_kernel(x_ref, mod_ref, g1_ref, w_in_ref, w_gate_ref, b_gate_ref, lng_ref, lnb_ref,
                  wsp_ref, bsp_ref, wpool_ref, bpool_ref, pscale_ref, wpa_ref, wpb_ref, wout_ref,
                  o_ref, pool_carry_ref):
    ts, d = x_ref.shape[1], x_ref.shape[2]
    j = pl.program_id(1)

    @pl.when(j == 0)
    def _():
        pool_carry_ref[...] = jnp.zeros_like(pool_carry_ref)

    x = x_ref[0]
    sh1 = mod_ref[0, 0:1, :]
    sc1 = mod_ref[0, 1:2, :]
    gt1 = mod_ref[0, 2:3, :]
    h = _rms_scale(x) * (g1_ref[...] * (1.0 + sc1)) + sh1
    hb = h.astype(BF16)

    z = _dot(hb, w_in_ref[...])
    a_w = d
    u = _gelu(z[:, :a_w])
    v = _gelu(z[:, a_w:2 * a_w])
    mu = jnp.mean(v, axis=-1, keepdims=True)
    vc = v - mu
    var = jnp.mean(vc * vc, axis=-1, keepdims=True)
    vn = (vc * lax.rsqrt(var + EPS)) * lng_ref[...] + lnb_ref[...]
    vnb = vn.astype(BF16)

    gd = a_w // A_GROUPS
    p = lax.broadcasted_iota(jnp.int32, (GMLP_BLOCK, GMLP_BLOCK), 0)
    q = lax.broadcasted_iota(jnp.int32, (GMLP_BLOCK, GMLP_BLOCK), 1)
    allowed = (q // CHUNK) <= (p // CHUNK)
    w_masked = [jnp.where(allowed, wsp_ref[g], 0.0).astype(BF16) for g in range(A_GROUPS)]
    bias_map = bsp_ref[...]
    s_rows = []
    for n in range(ts // GMLP_BLOCK):
        r0 = n * GMLP_BLOCK
        cols = [_dot(w_masked[g], vnb[r0:r0 + GMLP_BLOCK, g * gd:(g + 1) * gd]) for g in range(A_GROUPS)]
        s_rows.append(jnp.concatenate(cols, axis=1) + bias_map)
    s = jnp.concatenate(s_rows, axis=0)
    y_a = u * s

    hbp = z[:, 2 * a_w:]
    ext = jnp.concatenate([pool_carry_ref[...], hbp], axis=0)
    pool_carry_ref[...] = hbp[ts - POOL_HALO:, :]
    bgd = hbp.shape[1] // len(POOL_WINDOWS)
    t = j * ts + lax.broadcasted_iota(jnp.int32, (ts, bgd), 0)
    yb_cols = []
    for gi, w in enumerate(POOL_WINDOWS):
        e = ext[:, gi * bgd:(gi + 1) * bgd]
        acc, span = e, 1
        while span < w:
            acc = acc + pltpu.roll(acc, span, 0)
            span *= 2
        win = acc[POOL_HALO:, :]
        cur = e[POOL_HALO:, :]
        count = jnp.minimum(t + 1, w).astype(F32)
        pooled = win / count - cur
        mixed = _dot(pooled.astype(BF16), wpool_ref[gi]) + bpool_ref[gi:gi + 1, :]
        yb_cols.append(mixed)
    y_b = jnp.concatenate(yb_cols, axis=1) * pscale_ref[...]

    gates = jax.nn.sigmoid(_dot(hb, w_gate_ref[...]) + b_gate_ref[...])
    merged = (gates[:, :d] * _dot(y_a.astype(BF16), wpa_ref[...])
              + gates[:, d:] * _dot(y_b.astype(BF16), wpb_ref[...]))
    o_ref[0] = x + gt1 * _dot(merged.astype(BF16), wout_ref[...])


def _resident(shape):
    nd = len(shape)
    return pl.BlockSpec(shape, lambda b, j: (0,) * nd, pipeline_mode=pl.Buffered(1))


def _mixer_call(x, mod, g1, w_in, w_gate, b_gate, ln_g, ln_b, w_sp, bias_map, w_pool, b_pool,
                pool_scale, w_pa, w_pb, w_out):
    bsz, s_len, d = x.shape
    ts = TOKEN_TILE
    row = lambda a: a.reshape(1, -1)
    operands = [
        (x, pl.BlockSpec((1, ts, d), lambda b, j: (b, j, 0))),
        (mod, pl.BlockSpec((1, N_ADA, d), lambda b, j: (b, 0, 0))),
    ]
    for a in (row(g1), w_in, w_gate, row(b_gate), row(ln_g), row(ln_b), w_sp, bias_map, w_pool,
              b_pool, row(pool_scale), w_pa, w_pb, w_out):
        operands.append((a, _resident(a.shape)))
    return pl.pallas_call(
        _mixer_kernel,
        grid=(bsz, s_len // ts),
        in_specs=[spec for _, spec in operands],
        out_specs=pl.BlockSpec((1, ts, d), lambda b, j: (b, j, 0)),
        out_shape=jax.ShapeDtypeStruct(x.shape, F32),
        scratch_shapes=[pltpu.VMEM((POOL_HALO, w_pool.shape[0] * w_pool.shape[1]), F32)],
        compiler_params=pltpu.CompilerParams(
            dimension_semantics=("arbitrary", "arbitrary"), vmem_limit_bytes=VMEM_LIMIT_BYTES),
        name="mixer",
    )(*[a for a, _ in operands])


def _interleave_rows(x):
    n, d = x.shape
    return jnp.swapaxes(x.reshape(SUBLANES, n // SUBLANES, d), 0, 1).reshape(n, d)


def _deinterleave_rows(y):
    n, d = y.shape
    return jnp.swapaxes(y.reshape(n // SUBLANES, SUBLANES, d), 0, 1).reshape(n, d)


def _delay_rows(a, prev_row):
    n = a.shape[0]
    wrapped = pltpu.roll(a[n - SUBLANES:, :], 1, 0)
    first = jnp.where(lax.broadcasted_iota(jnp.int32, wrapped.shape, 0) == 0, prev_row, wrapped)
    return jnp.concatenate([first, a[:n - SUBLANES, :]], axis=0)


def _channel_kernel(x_ref, mod_ref, g2_ref, w_up_ref, cw_ref, cb_ref, w_down_ref, gf_ref,
                    o_ref, conv_carry_ref, *, final_norm):
    ts, d = x_ref.shape[1], x_ref.shape[2]
    d_ff = w_down_ref.shape[0]
    n_chunks = d_ff // FF_CHUNK
    j = pl.program_id(1)

    @pl.when(j == 0)
    def _():
        conv_carry_ref[...] = jnp.zeros_like(conv_carry_ref)

    x = _interleave_rows(x_ref[0])
    sh2 = mod_ref[0, 3:4, :]
    sc2 = mod_ref[0, 4:5, :]
    gt2 = mod_ref[0, 5:6, :]
    h2 = _rms_scale(x) * (g2_ref[...] * (1.0 + sc2)) + sh2
    h2b = h2.astype(BF16)

    def up_project(c):
        return [_dot(h2b, w_up_ref[:, pl.ds(c0, FF_CHUNK)]) for c0 in (c * FF_CHUNK, d_ff + c * FF_CHUNK)]

    def conv_cols(pre, c0):
        cols = pl.ds(c0, FF_CHUNK)
        taps = [pre]
        for m in range(1, CONV_WIDTH):
            r = (CONV_WIDTH - 1 - m) * SUBLANES + SUBLANES - 1
            taps.append(_delay_rows(taps[-1], conv_carry_ref[r:r + 1, cols]))
        conv_carry_ref[:, cols] = pre[ts - CONV_CARRY_ROWS:, :]
        out = cb_ref[:, cols]
        for k in range(CONV_WIDTH):
            out = out + taps[CONV_WIDTH - 1 - k] * cw_ref[k:k + 1, cols]
        return out

    nxt = up_project(0)
    acc = jnp.zeros((ts, d), F32)
    for c in range(n_chunks):
        cur = nxt
        if c + 1 < n_chunks:
            nxt = up_project(c + 1)
        gate = conv_cols(cur[0], c * FF_CHUNK)
        val = conv_cols(cur[1], d_ff + c * FF_CHUNK)
        f = gate * jax.nn.sigmoid(gate) * val
        acc = acc + _dot(f.astype(BF16), w_down_ref[c * FF_CHUNK:(c + 1) * FF_CHUNK, :])
    x2 = x + gt2 * acc
    if final_norm:
        x2 = _rms_scale(x2) * gf_ref[...]
    o_ref[0] = _deinterleave_rows(x2)


def _channel_call(x, mod, g2, w_up, conv_w, conv_b, w_down, g_final, final_norm):
    bsz, s_len, d = x.shape
    ts = TOKEN_TILE
    row = lambda a: a.reshape(1, -1)
    operands = [
        (x, pl.BlockSpec((1, ts, d), lambda b, j: (b, j, 0))),
        (mod, pl.BlockSpec((1, N_ADA, d), lambda b, j: (b, 0, 0))),
    ]
    for a in (row(g2), w_up, conv_w, row(conv_b), w_down, row(g_final)):
        operands.append((a, _resident(a.shape)))
    return pl.pallas_call(
        functools.partial(_channel_kernel, final_norm=final_norm),
        grid=(bsz, s_len // ts),
        in_specs=[spec for _, spec in operands],
        out_specs=pl.BlockSpec((1, ts, d), lambda b, j: (b, j, 0)),
        out_shape=jax.ShapeDtypeStruct(x.shape, F32),
        scratch_shapes=[pltpu.VMEM((CONV_CARRY_ROWS, w_up.shape[1]), F32)],
        compiler_params=pltpu.CompilerParams(
            dimension_semantics=("arbitrary", "arbitrary"), vmem_limit_bytes=VMEM_LIMIT_BYTES),
        name="channel",
    )(*[a for a, _ in operands])


def kernel(x, c, w_ada, b_ada, g_norm1, w_in, ln_v_g, ln_v_b, w_spatial, b_spatial, w_pool, b_pool,
           pool_scale, w_proj_a, w_proj_b, w_gate, b_gate, w_out, g_norm2, w_up, conv_w, conv_b,
           w_down, g_final):
    depth = w_ada.shape[0]
    bsz, s_len, d = x.shape
    assert s_len % TOKEN_TILE == 0 and TOKEN_TILE % GMLP_BLOCK == 0
    assert w_down.shape[1] % FF_CHUNK == 0 and w_ada.shape[2] % ADA_TILE == 0
    gd = w_in.shape[2] // 3 // A_GROUPS
    for l in range(depth):
        mod = _ada_call(c, w_ada[l], b_ada[l]).reshape(bsz, N_ADA, d)
        bias_map = jnp.repeat(b_spatial[l].T, gd, axis=1)
        x = _mixer_call(
            x, mod, g_norm1[l], w_in[l].astype(BF16), w_gate[l].astype(BF16), b_gate[l], ln_v_g[l],
            ln_v_b[l], w_spatial[l], bias_map, w_pool[l].astype(BF16), b_pool[l], pool_scale[l],
            w_proj_a[l].astype(BF16), w_proj_b[l].astype(BF16), w_out[l].astype(BF16))
        x = _channel_call(
            x, mod, g_norm2[l], w_up[l].astype(BF16), conv_w[l], conv_b[l], w_down[l].astype(BF16),
            g_final, final_norm=(l == depth - 1))
    return x
```

```python
import functools
import math

import jax
import jax.numpy as jnp
from jax import lax
from jax.experimental import pallas as pl
from jax.experimental.pallas import tpu as pltpu

EPS = 1e-6
CHUNK = 64
GMLP_BLOCK = 128
A_GROUPS = 8
POOL_WINDOWS = (2, 4, 8, 16)
CONV_WIDTH = 3
N_ADA = 6

SUBLANES = 8
LANES = 128
POOL_HALO = 16
CONV_CARRY_ROWS = (CONV_WIDTH - 1) * SUBLANES
MXU_WIDTH = 256
MIXER_TILE = 256
CHANNEL_TILE = 256
FF_CHUNK = 256
ADA_TILE = 1024
VMEM_LIMIT_BYTES = 56 * 1024 * 1024

BF16 = jnp.bfloat16
F32 = jnp.float32


def _dot(a, b):
    return jnp.dot(a, b, preferred_element_type=F32)


def _dot_ref(a, w_ref):
    n = w_ref.shape[-1]
    assert n % MXU_WIDTH == 0
    return jnp.concatenate(
        [_dot(a, w_ref[:, c0:c0 + MXU_WIDTH]) for c0 in range(0, n, MXU_WIDTH)], axis=1)


def _gelu(x):
    return 0.5 * x * (1.0 + lax.erf(x * math.sqrt(0.5)))


def _rms_scale(x):
    ms = jnp.mean(x * x, axis=-1, keepdims=True)
    return x * lax.rsqrt(ms + EPS)


def _ada_kernel(c_ref, w_ref, b_ref, o_ref):
    c = c_ref[...]
    s = c * jax.nn.sigmoid(c)
    o_ref[...] = _dot(s.astype(BF16), w_ref[...].astype(BF16)) + b_ref[...]


def _ada_call(c, w_ada, b_ada):
    bsz, d = c.shape
    n = w_ada.shape[1]
    return pl.pallas_call(
        _ada_kernel,
        grid=(n // ADA_TILE,),
        in_specs=[
            pl.BlockSpec((bsz, d), lambda j: (0, 0)),
            pl.BlockSpec((d, ADA_TILE), lambda j: (0, j)),
            pl.BlockSpec((1, ADA_TILE), lambda j: (0, j)),
        ],
        out_specs=pl.BlockSpec((bsz, ADA_TILE), lambda j: (0, j)),
        out_shape=jax.ShapeDtypeStruct((bsz, n), F32),
        compiler_params=pltpu.CompilerParams(dimension_semantics=("arbitrary",)),
        name="ada_mod",
    )(c, w_ada, b_ada.reshape(1, n))


def _mixer_kernel(x_ref, mod_ref, g1_ref, w_ref, b_gate_ref, lng_ref, lnb_ref,
                  wsp_ref, bsp_ref, wpool_ref, bpool_ref, pscale_ref,
                  o_ref, pool_carry_ref, *, in_width):
    ts, d = x_ref.shape[1], x_ref.shape[2]
    j = pl.program_id(1)
    w_in_ref = w_ref.at[:, 0:in_width]
    w_gate_ref = w_ref.at[:, in_width:in_width + 2 * d]
    wpa_ref = w_ref.at[:, in_width + 2 * d:in_width + 3 * d]
    wpb_ref = w_ref.at[:, in_width + 3 * d:in_width + 4 * d]
    wout_ref = w_ref.at[:, in_width + 4 * d:in_width + 5 * d]

    @pl.when(j == 0)
    def _():
        pool_carry_ref[...] = jnp.zeros_like(pool_carry_ref)

    x = x_ref[0]
    sh1 = mod_ref[0, 0:1, :]
    sc1 = mod_ref[0, 1:2, :]
    gt1 = mod_ref[0, 2:3, :]
    h = _rms_scale(x) * (g1_ref[...] * (1.0 + sc1)) + sh1
    hb = h.astype(BF16)

    a_w = d
    v = _gelu(_dot_ref(hb, w_in_ref.at[:, a_w:2 * a_w]))
    u = _gelu(_dot_ref(hb, w_in_ref.at[:, 0:a_w]))
    hbp = _dot_ref(hb, w_in_ref.at[:, 2 * a_w:w_in_ref.shape[1]])
    gates = jax.nn.sigmoid(_dot_ref(hb, w_gate_ref) + b_gate_ref[...])
    mu = jnp.mean(v, axis=-1, keepdims=True)
    vc = v - mu
    var = jnp.mean(vc * vc, axis=-1, keepdims=True)
    vn = (vc * lax.rsqrt(var + EPS)) * lng_ref[...] + lnb_ref[...]
    vnb = vn.astype(BF16)

    ext = jnp.concatenate([pool_carry_ref[...], hbp], axis=0)
    pool_carry_ref[...] = hbp[ts - POOL_HALO:, :]
    bgd = hbp.shape[1] // len(POOL_WINDOWS)
    t = j * ts + lax.broadcasted_iota(jnp.int32, (ts, bgd), 0)
    yb_cols = []
    for gi, w in enumerate(POOL_WINDOWS):
        e = ext[:, gi * bgd:(gi + 1) * bgd]
        acc, span = e, 1
        while span < w:
            acc = acc + pltpu.roll(acc, span, 0)
            span *= 2
        win = acc[POOL_HALO:, :]
        cur = e[POOL_HALO:, :]
        count = jnp.minimum(t + 1, w).astype(F32)
        pooled = win / count - cur
        mixed = _dot(pooled.astype(BF16), wpool_ref[gi]) + bpool_ref[gi:gi + 1, :]
        yb_cols.append(mixed)
    y_b = jnp.concatenate(yb_cols, axis=1) * pscale_ref[...]
    branch_b = gates[:, d:] * _dot_ref(y_b.astype(BF16), wpb_ref)

    gd = a_w // A_GROUPS
    p = lax.broadcasted_iota(jnp.int32, (GMLP_BLOCK, GMLP_BLOCK), 0)
    q = lax.broadcasted_iota(jnp.int32, (GMLP_BLOCK, GMLP_BLOCK), 1)
    allowed = (q // CHUNK) <= (p // CHUNK)
    w_masked = [jnp.where(allowed, wsp_ref[g], 0.0).astype(BF16) for g in range(A_GROUPS)]
    bias_map = bsp_ref[...]
    s_rows = []
    for n in range(ts // GMLP_BLOCK):
        r0 = n * GMLP_BLOCK
        cols = [_dot(w_masked[g], vnb[r0:r0 + GMLP_BLOCK, g * gd:(g + 1) * gd]) for g in range(A_GROUPS)]
        s_rows.append(jnp.concatenate(cols, axis=1) + bias_map)
    s = jnp.concatenate(s_rows, axis=0)
    y_a = u * s

    merged = gates[:, :d] * _dot_ref(y_a.astype(BF16), wpa_ref) + branch_b
    o_ref[0] = x + gt1 * _dot_ref(merged.astype(BF16), wout_ref)


def _resident(shape):
    nd = len(shape)
    return pl.BlockSpec(shape, lambda b, j: (0,) * nd, pipeline_mode=pl.Buffered(1))


def _pack_columns(ws):
    rows = ws[0].shape[0]
    pad = LANES if (sum(w.shape[1] for w in ws) // LANES) % SUBLANES == 0 else 0
    cols = [w.astype(BF16) for w in ws] + ([jnp.zeros((rows, pad), BF16)] if pad else [])
    return jnp.concatenate(cols, axis=1)


def _mixer_call(x, mod, g1, w_in, w_gate, b_gate, ln_g, ln_b, w_sp, bias_map, w_pool, b_pool,
                pool_scale, w_pa, w_pb, w_out):
    bsz, s_len, d = x.shape
    ts = MIXER_TILE
    assert s_len % ts == 0 and ts % GMLP_BLOCK == 0 and ts >= POOL_HALO
    assert w_gate.shape[1] == 2 * d and w_pa.shape == w_pb.shape == w_out.shape == (d, d)
    row = lambda a: a.reshape(1, -1)
    w_packed = _pack_columns([w_in, w_gate, w_pa, w_pb, w_out])
    operands = [
        (x, pl.BlockSpec((1, ts, d), lambda b, j: (b, j, 0))),
        (mod, pl.BlockSpec((1, N_ADA, d), lambda b, j: (b, 0, 0))),
    ]
    for a in (row(g1), w_packed, row(b_gate), row(ln_g), row(ln_b), w_sp, bias_map,
              w_pool.astype(BF16), b_pool, row(pool_scale)):
        operands.append((a, _resident(a.shape)))
    return pl.pallas_call(
        functools.partial(_mixer_kernel, in_width=w_in.shape[1]),
        grid=(bsz, s_len // ts),
        in_specs=[spec for _, spec in operands],
        out_specs=pl.BlockSpec((1, ts, d), lambda b, j: (b, j, 0)),
        out_shape=jax.ShapeDtypeStruct(x.shape, F32),
        scratch_shapes=[pltpu.VMEM((POOL_HALO, w_pool.shape[0] * w_pool.shape[1]), F32)],
        compiler_params=pltpu.CompilerParams(
            dimension_semantics=("arbitrary", "arbitrary"), vmem_limit_bytes=VMEM_LIMIT_BYTES),
        name="mixer",
    )(*[a for a, _ in operands])


def _interleave_rows(x):
    n, d = x.shape
    return jnp.swapaxes(x.reshape(SUBLANES, n // SUBLANES, d), 0, 1).reshape(n, d)


def _deinterleave_rows(y):
    n, d = y.shape
    return jnp.swapaxes(y.reshape(n // SUBLANES, SUBLANES, d), 0, 1).reshape(n, d)


def _delay_rows(a, prev_row):
    n = a.shape[0]
    wrapped = pltpu.roll(a[n - SUBLANES:, :], 1, 0)
    first = jnp.where(lax.broadcasted_iota(jnp.int32, wrapped.shape, 0) == 0, prev_row, wrapped)
    return jnp.concatenate([first, a[:n - SUBLANES, :]], axis=0)


def _channel_kernel(x_ref, mod_ref, g2_ref, w_up_ref, cw_ref, cb_ref, w_down_ref, gf_ref,
                    o_ref, conv_carry_ref, *, final_norm):
    ts, d = x_ref.shape[1], x_ref.shape[2]
    d_ff = w_down_ref.shape[0]
    n_chunks = d_ff // FF_CHUNK
    j = pl.program_id(1)

    @pl.when(j == 0)
    def _():
        conv_carry_ref[...] = jnp.zeros_like(conv_carry_ref)

    x = _interleave_rows(x_ref[0])
    sh2 = mod_ref[0, 3:4, :]
    sc2 = mod_ref[0, 4:5, :]
    gt2 = mod_ref[0, 5:6, :]
    h2 = _rms_scale(x) * (g2_ref[...] * (1.0 + sc2)) + sh2
    h2b = h2.astype(BF16)

    def up_project(c):
        return [_dot(h2b, w_up_ref[:, pl.ds(c0, FF_CHUNK)]) for c0 in (c * FF_CHUNK, d_ff + c * FF_CHUNK)]

    def conv_cols(pre, c0):
        cols = pl.ds(c0, FF_CHUNK)
        taps = [pre]
        for m in range(1, CONV_WIDTH):
            r = (CONV_WIDTH - 1 - m) * SUBLANES + SUBLANES - 1
            taps.append(_delay_rows(taps[-1], conv_carry_ref[r:r + 1, cols]))
        conv_carry_ref[:, cols] = pre[ts - CONV_CARRY_ROWS:, :]
        out = cb_ref[:, cols]
        for k in range(CONV_WIDTH):
            out = out + taps[CONV_WIDTH - 1 - k] * cw_ref[k:k + 1, cols]
        return out

    nxt = up_project(0)
    acc = jnp.zeros((ts, d), F32)
    for c in range(n_chunks):
        cur = nxt
        if c + 1 < n_chunks:
            nxt = up_project(c + 1)
        gate = conv_cols(cur[0], c * FF_CHUNK)
        val = conv_cols(cur[1], d_ff + c * FF_CHUNK)
        f = gate * jax.nn.sigmoid(gate) * val
        acc = acc + _dot_ref(f.astype(BF16), w_down_ref.at[c * FF_CHUNK:(c + 1) * FF_CHUNK, 0:d])
    x2 = x + gt2 * acc
    if final_norm:
        x2 = _rms_scale(x2) * gf_ref[...]
    o_ref[0] = _deinterleave_rows(x2)


def _channel_call(x, mod, g2, w_up, conv_w, conv_b, w_down, g_final, final_norm):
    bsz, s_len, d = x.shape
    ts = CHANNEL_TILE
    assert s_len % ts == 0 and ts >= CONV_WIDTH * SUBLANES and w_down.shape[0] % FF_CHUNK == 0
    row = lambda a: a.reshape(1, -1)
    operands = [
        (x, pl.BlockSpec((1, ts, d), lambda b, j: (b, j, 0))),
        (mod, pl.BlockSpec((1, N_ADA, d), lambda b, j: (b, 0, 0))),
    ]
    for a in (row(g2), _pack_columns([w_up]), conv_w, row(conv_b), _pack_columns([w_down]), row(g_final)):
        operands.append((a, _resident(a.shape)))
    return pl.pallas_call(
        functools.partial(_channel_kernel, final_norm=final_norm),
        grid=(bsz, s_len // ts),
        in_specs=[spec for _, spec in operands],
        out_specs=pl.BlockSpec((1, ts, d), lambda b, j: (b, j, 0)),
        out_shape=jax.ShapeDtypeStruct(x.shape, F32),
        scratch_shapes=[pltpu.VMEM((CONV_CARRY_ROWS, w_up.shape[1]), F32)],
        compiler_params=pltpu.CompilerParams(
            dimension_semantics=("arbitrary", "arbitrary"), vmem_limit_bytes=VMEM_LIMIT_BYTES),
        name="channel",
    )(*[a for a, _ in operands])


def kernel(x, c, w_ada, b_ada, g_norm1, w_in, ln_v_g, ln_v_b, w_spatial, b_spatial, w_pool, b_pool,
           pool_scale, w_proj_a, w_proj_b, w_gate, b_gate, w_out, g_norm2, w_up, conv_w, conv_b,
           w_down, g_final):
    depth = w_ada.shape[0]
    bsz, s_len, d = x.shape
    assert w_ada.shape[2] % ADA_TILE == 0
    gd = w_in.shape[2] // 3 // A_GROUPS
    for l in range(depth):
        mod = _ada_call(c, w_ada[l], b_ada[l]).reshape(bsz, N_ADA, d)
        bias_map = jnp.repeat(b_spatial[l].T, gd, axis=1)
        x = _mixer_call(
            x, mod, g_norm1[l], w_in[l], w_gate[l], b_gate[l], ln_v_g[l], ln_v_b[l], w_spatial[l],
            bias_map, w_pool[l], b_pool[l], pool_scale[l], w_proj_a[l], w_proj_b[l], w_out[l])
        x = _channel_call(
            x, mod, g_norm2[l], w_up[l], conv_w[l], conv_b[l], w_down[l], g_final,
            final_norm=(l == depth - 1))
    return x
```

```python
import functools
import math

import jax
import jax.numpy as jnp
from jax import lax
from jax.experimental import pallas as pl
from jax.experimental.pallas import tpu as pltpu

EPS = 1e-6
CHUNK = 64
GMLP_BLOCK = 128
A_GROUPS = 8
POOL_WINDOWS = (2, 4, 8, 16)
CONV_WIDTH = 3
N_ADA = 6

SUBLANES = 8
LANES = 128
POOL_HALO = 16
CONV_CARRY_ROWS = (CONV_WIDTH - 1) * SUBLANES
MXU_WIDTH = 256
MIXER_TILE = 512
CHANNEL_TILE = 256
FF_CHUNK = 256
ADA_TILE = 1024
VMEM_LIMIT_BYTES = 56 * 1024 * 1024

BF16 = jnp.bfloat16
F32 = jnp.float32


def _dot(a, b):
    return jnp.dot(a, b, preferred_element_type=F32)


def _dot_ref(a, w_ref):
    n = w_ref.shape[-1]
    assert n % MXU_WIDTH == 0
    return jnp.concatenate(
        [_dot(a, w_ref[:, c0:c0 + MXU_WIDTH]) for c0 in range(0, n, MXU_WIDTH)], axis=1)


def _gelu(x):
    return 0.5 * x * (1.0 + lax.erf(x * math.sqrt(0.5)))


def _rms_scale(x):
    ms = jnp.mean(x * x, axis=-1, keepdims=True)
    return x * lax.rsqrt(ms + EPS)


def _ada_kernel(c_ref, w_ref, b_ref, o_ref):
    c = c_ref[...]
    s = c * jax.nn.sigmoid(c)
    o_ref[...] = _dot(s.astype(BF16), w_ref[...].astype(BF16)) + b_ref[...]


def _ada_call(c, w_ada, b_ada):
    bsz, d = c.shape
    n = w_ada.shape[1]
    return pl.pallas_call(
        _ada_kernel,
        grid=(n // ADA_TILE,),
        in_specs=[
            pl.BlockSpec((bsz, d), lambda j: (0, 0)),
            pl.BlockSpec((d, ADA_TILE), lambda j: (0, j)),
            pl.BlockSpec((1, ADA_TILE), lambda j: (0, j)),
        ],
        out_specs=pl.BlockSpec((bsz, ADA_TILE), lambda j: (0, j)),
        out_shape=jax.ShapeDtypeStruct((bsz, n), F32),
        compiler_params=pltpu.CompilerParams(dimension_semantics=("arbitrary",)),
        name="ada_mod",
    )(c, w_ada, b_ada.reshape(1, n))


def _mixer_kernel(x_ref, mod_ref, g1_ref, w_ref, b_gate_ref, lng_ref, lnb_ref,
                  wsp_ref, bsp_ref, wpool_ref, bpool_ref, pscale_ref,
                  o_ref, pool_carry_ref, *, in_width):
    ts, d = x_ref.shape[1], x_ref.shape[2]
    j = pl.program_id(1)
    w_in_ref = w_ref.at[:, 0:in_width]
    w_gate_ref = w_ref.at[:, in_width:in_width + 2 * d]
    wpa_ref = w_ref.at[:, in_width + 2 * d:in_width + 3 * d]
    wpb_ref = w_ref.at[:, in_width + 3 * d:in_width + 4 * d]
    wout_ref = w_ref.at[:, in_width + 4 * d:in_width + 5 * d]

    @pl.when(j == 0)
    def _():
        pool_carry_ref[...] = jnp.zeros_like(pool_carry_ref)

    x = x_ref[0]
    sh1 = mod_ref[0, 0:1, :]
    sc1 = mod_ref[0, 1:2, :]
    gt1 = mod_ref[0, 2:3, :]
    h = _rms_scale(x) * (g1_ref[...] * (1.0 + sc1)) + sh1
    hb = h.astype(BF16)

    a_w = d
    v = _gelu(_dot_ref(hb, w_in_ref.at[:, a_w:2 * a_w]))
    u = _gelu(_dot_ref(hb, w_in_ref.at[:, 0:a_w]))
    hbp = _dot_ref(hb, w_in_ref.at[:, 2 * a_w:w_in_ref.shape[1]])
    gates = jax.nn.sigmoid(_dot_ref(hb, w_gate_ref) + b_gate_ref[...])
    mu = jnp.mean(v, axis=-1, keepdims=True)
    vc = v - mu
    var = jnp.mean(vc * vc, axis=-1, keepdims=True)
    vn = (vc * lax.rsqrt(var + EPS)) * lng_ref[...] + lnb_ref[...]
    vnb = vn.astype(BF16)

    ext = jnp.concatenate([pool_carry_ref[...], hbp], axis=0)
    pool_carry_ref[...] = hbp[ts - POOL_HALO:, :]
    bgd = hbp.shape[1] // len(POOL_WINDOWS)
    t = j * ts + lax.broadcasted_iota(jnp.int32, (ts, bgd), 0)
    yb_cols = []
    for gi, w in enumerate(POOL_WINDOWS):
        e = ext[:, gi * bgd:(gi + 1) * bgd]
        acc, span = e, 1
        while span < w:
            acc = acc + pltpu.roll(acc, span, 0)
            span *= 2
        win = acc[POOL_HALO:, :]
        cur = e[POOL_HALO:, :]
        count = jnp.minimum(t + 1, w).astype(F32)
        pooled = win / count - cur
        mixed = _dot(pooled.astype(BF16), wpool_ref[gi]) + bpool_ref[gi:gi + 1, :]
        yb_cols.append(mixed)
    y_b = jnp.concatenate(yb_cols, axis=1) * pscale_ref[...]
    branch_b = gates[:, d:] * _dot_ref(y_b.astype(BF16), wpb_ref)

    gd = a_w // A_GROUPS
    p = lax.broadcasted_iota(jnp.int32, (GMLP_BLOCK, GMLP_BLOCK), 0)
    q = lax.broadcasted_iota(jnp.int32, (GMLP_BLOCK, GMLP_BLOCK), 1)
    allowed = (q // CHUNK) <= (p // CHUNK)
    w_masked = [jnp.where(allowed, wsp_ref[g], 0.0).astype(BF16) for g in range(A_GROUPS)]
    bias_map = bsp_ref[...]
    s_rows = []
    for n in range(ts // GMLP_BLOCK):
        r0 = n * GMLP_BLOCK
        cols = [_dot(w_masked[g], vnb[r0:r0 + GMLP_BLOCK, g * gd:(g + 1) * gd]) for g in range(A_GROUPS)]
        s_rows.append(jnp.concatenate(cols, axis=1) + bias_map)
    s = jnp.concatenate(s_rows, axis=0)
    y_a = u * s

    merged = gates[:, :d] * _dot_ref(y_a.astype(BF16), wpa_ref) + branch_b
    x1 = x + gt1 * _dot_ref(merged.astype(BF16), wout_ref)
    o_ref[0] = jnp.concatenate(
        [_interleave_rows(x1[r:r + CHANNEL_TILE, :]) for r in range(0, ts, CHANNEL_TILE)], axis=0)


def _resident(shape):
    nd = len(shape)
    return pl.BlockSpec(shape, lambda b, j: (0,) * nd, pipeline_mode=pl.Buffered(1))


def _pack_columns(ws):
    rows = ws[0].shape[0]
    pad = LANES if (sum(w.shape[1] for w in ws) // LANES) % SUBLANES == 0 else 0
    cols = [w.astype(BF16) for w in ws] + ([jnp.zeros((rows, pad), BF16)] if pad else [])
    return jnp.concatenate(cols, axis=1)


def _mixer_call(x, mod, g1, w_in, w_gate, b_gate, ln_g, ln_b, w_sp, bias_map, w_pool, b_pool,
                pool_scale, w_pa, w_pb, w_out):
    bsz, s_len, d = x.shape
    ts = MIXER_TILE
    assert s_len % ts == 0 and ts % GMLP_BLOCK == 0 and ts >= POOL_HALO and ts % CHANNEL_TILE == 0
    assert w_gate.shape[1] == 2 * d and w_pa.shape == w_pb.shape == w_out.shape == (d, d)
    row = lambda a: a.reshape(1, -1)
    w_packed = _pack_columns([w_in, w_gate, w_pa, w_pb, w_out])
    operands = [
        (x, pl.BlockSpec((1, ts, d), lambda b, j: (b, j, 0))),
        (mod, pl.BlockSpec((1, N_ADA, d), lambda b, j: (b, 0, 0))),
    ]
    for a in (row(g1), w_packed, row(b_gate), row(ln_g), row(ln_b), w_sp, bias_map,
              w_pool.astype(BF16), b_pool, row(pool_scale)):
        operands.append((a, _resident(a.shape)))
    return pl.pallas_call(
        functools.partial(_mixer_kernel, in_width=w_in.shape[1]),
        grid=(bsz, s_len // ts),
        in_specs=[spec for _, spec in operands],
        out_specs=pl.BlockSpec((1, ts, d), lambda b, j: (b, j, 0)),
        out_shape=jax.ShapeDtypeStruct(x.shape, F32),
        scratch_shapes=[pltpu.VMEM((POOL_HALO, w_pool.shape[0] * w_pool.shape[1]), F32)],
        compiler_params=pltpu.CompilerParams(
            dimension_semantics=("arbitrary", "arbitrary"), vmem_limit_bytes=VMEM_LIMIT_BYTES),
        name="mixer",
    )(*[a for a, _ in operands])


def _interleave_rows(x):
    n, d = x.shape
    return jnp.swapaxes(x.reshape(SUBLANES, n // SUBLANES, d), 0, 1).reshape(n, d)


def _deinterleave_rows(y):
    n, d = y.shape
    return jnp.swapaxes(y.reshape(n // SUBLANES, SUBLANES, d), 0, 1).reshape(n, d)


def _delay_rows(a, prev_row):
    n = a.shape[0]
    wrapped = pltpu.roll(a[n - SUBLANES:, :], 1, 0)
    first = jnp.where(lax.broadcasted_iota(jnp.int32, wrapped.shape, 0) == 0, prev_row, wrapped)
    return jnp.concatenate([first, a[:n - SUBLANES, :]], axis=0)


def _channel_kernel(x_ref, mod_ref, g2_ref, w_up_ref, cw_ref, cb_ref, w_down_ref, gf_ref,
                    o_ref, conv_carry_ref, *, final_norm):
    ts, d = x_ref.shape[1], x_ref.shape[2]
    d_ff = w_down_ref.shape[0]
    n_chunks = d_ff // FF_CHUNK
    j = pl.program_id(1)

    @pl.when(j == 0)
    def _():
        conv_carry_ref[...] = jnp.zeros_like(conv_carry_ref)

    x = x_ref[0]
    sh2 = mod_ref[0, 3:4, :]
    sc2 = mod_ref[0, 4:5, :]
    gt2 = mod_ref[0, 5:6, :]
    h2 = _rms_scale(x) * (g2_ref[...] * (1.0 + sc2)) + sh2
    h2b = h2.astype(BF16)

    def up_project(c):
        return [_dot(h2b, w_up_ref[:, pl.ds(c0, FF_CHUNK)]) for c0 in (c * FF_CHUNK, d_ff + c * FF_CHUNK)]

    def conv_cols(pre, c0):
        cols = pl.ds(c0, FF_CHUNK)
        taps = [pre]
        for m in range(1, CONV_WIDTH):
            r = (CONV_WIDTH - 1 - m) * SUBLANES + SUBLANES - 1
            taps.append(_delay_rows(taps[-1], conv_carry_ref[r:r + 1, cols]))
        conv_carry_ref[:, cols] = pre[ts - CONV_CARRY_ROWS:, :]
        out = cb_ref[:, cols]
        for k in range(CONV_WIDTH):
            out = out + taps[CONV_WIDTH - 1 - k] * cw_ref[k:k + 1, cols]
        return out

    nxt = up_project(0)
    acc = jnp.zeros((ts, d), F32)
    for c in range(n_chunks):
        cur = nxt
        if c + 1 < n_chunks:
            nxt = up_project(c + 1)
        gate = conv_cols(cur[0], c * FF_CHUNK)
        val = conv_cols(cur[1], d_ff + c * FF_CHUNK)
        f = gate * jax.nn.sigmoid(gate) * val
        acc = acc + _dot_ref(f.astype(BF16), w_down_ref.at[c * FF_CHUNK:(c + 1) * FF_CHUNK, 0:d])
    x2 = x + gt2 * acc
    if final_norm:
        x2 = _rms_scale(x2) * gf_ref[...]
    o_ref[0] = _deinterleave_rows(x2)


def _channel_call(x, mod, g2, w_up, conv_w, conv_b, w_down, g_final, final_norm):
    bsz, s_len, d = x.shape
    ts = CHANNEL_TILE
    assert s_len % ts == 0 and ts >= CONV_WIDTH * SUBLANES and w_down.shape[0] % FF_CHUNK == 0
    row = lambda a: a.reshape(1, -1)
    operands = [
        (x, pl.BlockSpec((1, ts, d), lambda b, j: (b, j, 0))),
        (mod, pl.BlockSpec((1, N_ADA, d), lambda b, j: (b, 0, 0))),
    ]
    for a in (row(g2), _pack_columns([w_up]), conv_w, row(conv_b), _pack_columns([w_down]), row(g_final)):
        operands.append((a, _resident(a.shape)))
    return pl.pallas_call(
        functools.partial(_channel_kernel, final_norm=final_norm),
        grid=(bsz, s_len // ts),
        in_specs=[spec for _, spec in operands],
        out_specs=pl.BlockSpec((1, ts, d), lambda b, j: (b, j, 0)),
        out_shape=jax.ShapeDtypeStruct(x.shape, F32),
        scratch_shapes=[pltpu.VMEM((CONV_CARRY_ROWS, w_up.shape[1]), F32)],
        compiler_params=pltpu.CompilerParams(
            dimension_semantics=("arbitrary", "arbitrary"), vmem_limit_bytes=VMEM_LIMIT_BYTES),
        name="channel",
    )(*[a for a, _ in operands])


def kernel(x, c, w_ada, b_ada, g_norm1, w_in, ln_v_g, ln_v_b, w_spatial, b_spatial, w_pool, b_pool,
           pool_scale, w_proj_a, w_proj_b, w_gate, b_gate, w_out, g_norm2, w_up, conv_w, conv_b,
           w_down, g_final):
    depth = w_ada.shape[0]
    bsz, s_len, d = x.shape
    assert w_ada.shape[2] % ADA_TILE == 0
    gd = w_in.shape[2] // 3 // A_GROUPS
    for l in range(depth):
        mod = _ada_call(c, w_ada[l], b_ada[l]).reshape(bsz, N_ADA, d)
        bias_map = jnp.repeat(b_spatial[l].T, gd, axis=1)
        x = _mixer_call(
            x, mod, g_norm1[l], w_in[l], w_gate[l], b_gate[l], ln_v_g[l], ln_v_b[l], w_spatial[l],
            bias_map, w_pool[l], b_pool[l], pool_scale[l], w_proj_a[l], w_proj_b[l], w_out[l])
        x = _channel_call(
            x, mod, g_norm2[l], w_up[l], conv_w[l], conv_b[l], w_down[l], g_final,
            final_norm=(l == depth - 1))
    return x
```

```python
import functools
import math

import jax
import jax.numpy as jnp
from jax import lax
from jax.experimental import pallas as pl
from jax.experimental.pallas import tpu as pltpu

EPS = 1e-6
CHUNK = 64
GMLP_BLOCK = 128
A_GROUPS = 8
POOL_WINDOWS = (2, 4, 8, 16)
CONV_WIDTH = 3
N_ADA = 6

SUBLANES = 8
LANES = 128
BF16_TILE_ROWS = 16
POOL_HALO = 16
CONV_CARRY_ROWS = (CONV_WIDTH - 1) * SUBLANES
MXU_WIDTH = 256
MIXER_TILE = 512
CHANNEL_TILE = 256
FF_CHUNK = 256
ADA_TILE = 1024
VMEM_LIMIT_BYTES = 56 * 1024 * 1024

BF16 = jnp.bfloat16
F32 = jnp.float32


def _dot(a, b):
    return jnp.dot(a, b, preferred_element_type=F32)


def _dot_ref(a, w_ref):
    n = w_ref.shape[-1]
    assert n % MXU_WIDTH == 0
    return jnp.concatenate(
        [_dot(a, w_ref[:, c0:c0 + MXU_WIDTH]) for c0 in range(0, n, MXU_WIDTH)], axis=1)


def _gelu(x):
    return 0.5 * x * (1.0 + lax.erf(x * math.sqrt(0.5)))


def _rms_scale(x):
    ms = jnp.mean(x * x, axis=-1, keepdims=True)
    return x * lax.rsqrt(ms + EPS)


def _ada_kernel(c_ref, w_ref, b_ref, o_ref):
    c = c_ref[...]
    s = c * jax.nn.sigmoid(c)
    o_ref[...] = _dot(s.astype(BF16), w_ref[...].astype(BF16)) + b_ref[...]


def _ada_call(c, w_ada, b_ada):
    bsz, d = c.shape
    n = w_ada.shape[1]
    return pl.pallas_call(
        _ada_kernel,
        grid=(n // ADA_TILE,),
        in_specs=[
            pl.BlockSpec((bsz, d), lambda j: (0, 0)),
            pl.BlockSpec((d, ADA_TILE), lambda j: (0, j)),
            pl.BlockSpec((1, ADA_TILE), lambda j: (0, j)),
        ],
        out_specs=pl.BlockSpec((bsz, ADA_TILE), lambda j: (0, j)),
        out_shape=jax.ShapeDtypeStruct((bsz, n), F32),
        compiler_params=pltpu.CompilerParams(dimension_semantics=("arbitrary",)),
        name="ada_mod",
    )(c, w_ada, b_ada.reshape(1, n))


def _cast_block(src_ref, dst_ref):
    n = src_ref.shape[1]
    dst_ref[:, 0:n] = src_ref[...].astype(BF16)
    if dst_ref.shape[1] > n:
        dst_ref[:, n:] = jnp.zeros((dst_ref.shape[0], dst_ref.shape[1] - n), BF16)


def _mixer_kernel(x_ref, mod_ref, g1_ref, w_ref, b_gate_ref, lng_ref, lnb_ref,
                  wsp_ref, bsp_ref, wpool_ref, bpool_ref, pscale_ref, w_up_ref, w_down_ref,
                  o_ref, w_up_bf_ref, w_down_bf_ref, pool_carry_ref, *, in_width):
    ts, d = x_ref.shape[1], x_ref.shape[2]
    j = pl.program_id(1)
    _cast_block(w_up_ref, w_up_bf_ref)
    _cast_block(w_down_ref, w_down_bf_ref)
    w_in_ref = w_ref.at[:, 0:in_width]
    w_gate_ref = w_ref.at[:, in_width:in_width + 2 * d]
    wpa_ref = w_ref.at[:, in_width + 2 * d:in_width + 3 * d]
    wpb_ref = w_ref.at[:, in_width + 3 * d:in_width + 4 * d]
    wout_ref = w_ref.at[:, in_width + 4 * d:in_width + 5 * d]

    @pl.when(j == 0)
    def _():
        pool_carry_ref[...] = jnp.zeros_like(pool_carry_ref)

    x = x_ref[0]
    sh1 = mod_ref[0, 0:1, :]
    sc1 = mod_ref[0, 1:2, :]
    gt1 = mod_ref[0, 2:3, :]
    h = _rms_scale(x) * (g1_ref[...] * (1.0 + sc1)) + sh1
    hb = h.astype(BF16)

    a_w = d
    v = _gelu(_dot_ref(hb, w_in_ref.at[:, a_w:2 * a_w]))
    u = _gelu(_dot_ref(hb, w_in_ref.at[:, 0:a_w]))
    hbp = _dot_ref(hb, w_in_ref.at[:, 2 * a_w:w_in_ref.shape[1]])
    gates = jax.nn.sigmoid(_dot_ref(hb, w_gate_ref) + b_gate_ref[...])
    mu = jnp.mean(v, axis=-1, keepdims=True)
    vc = v - mu
    var = jnp.mean(vc * vc, axis=-1, keepdims=True)
    vn = (vc * lax.rsqrt(var + EPS)) * lng_ref[...] + lnb_ref[...]
    vnb = vn.astype(BF16)

    ext = jnp.concatenate([pool_carry_ref[...], hbp], axis=0)
    pool_carry_ref[...] = hbp[ts - POOL_HALO:, :]
    bgd = hbp.shape[1] // len(POOL_WINDOWS)
    t = j * ts + lax.broadcasted_iota(jnp.int32, (ts, bgd), 0)
    yb_cols = []
    for gi, w in enumerate(POOL_WINDOWS):
        e = ext[:, gi * bgd:(gi + 1) * bgd]
        acc, span = e, 1
        while span < w:
            acc = acc + pltpu.roll(acc, span, 0)
            span *= 2
        win = acc[POOL_HALO:, :]
        cur = e[POOL_HALO:, :]
        count = jnp.minimum(t + 1, w).astype(F32)
        pooled = win / count - cur
        mixed = _dot(pooled.astype(BF16), wpool_ref[gi]) + bpool_ref[gi:gi + 1, :]
        yb_cols.append(mixed)
    y_b = jnp.concatenate(yb_cols, axis=1) * pscale_ref[...]
    branch_b = gates[:, d:] * _dot_ref(y_b.astype(BF16), wpb_ref)

    gd = a_w // A_GROUPS
    p = lax.broadcasted_iota(jnp.int32, (GMLP_BLOCK, GMLP_BLOCK), 0)
    q = lax.broadcasted_iota(jnp.int32, (GMLP_BLOCK, GMLP_BLOCK), 1)
    allowed = (q // CHUNK) <= (p // CHUNK)
    w_masked = [jnp.where(allowed, wsp_ref[g], 0.0).astype(BF16) for g in range(A_GROUPS)]
    bias_map = bsp_ref[...]
    s_rows = []
    for n in range(ts // GMLP_BLOCK):
        r0 = n * GMLP_BLOCK
        cols = [_dot(w_masked[g], vnb[r0:r0 + GMLP_BLOCK, g * gd:(g + 1) * gd]) for g in range(A_GROUPS)]
        s_rows.append(jnp.concatenate(cols, axis=1) + bias_map)
    s = jnp.concatenate(s_rows, axis=0)
    y_a = u * s

    merged = gates[:, :d] * _dot_ref(y_a.astype(BF16), wpa_ref) + branch_b
    x1 = x + gt1 * _dot_ref(merged.astype(BF16), wout_ref)
    o_ref[0] = jnp.concatenate(
        [_interleave_rows(x1[r:r + CHANNEL_TILE, :]) for r in range(0, ts, CHANNEL_TILE)], axis=0)


def _resident(shape):
    nd = len(shape)
    return pl.BlockSpec(shape, lambda b, j: (0,) * nd, pipeline_mode=pl.Buffered(1))


def _pitch_pad(n_cols):
    return LANES if (n_cols // LANES) % SUBLANES == 0 else 0


def _pack_columns(ws):
    rows = ws[0].shape[0]
    pad = _pitch_pad(sum(w.shape[1] for w in ws))
    cols = [w.astype(BF16) for w in ws] + ([jnp.zeros((rows, pad), BF16)] if pad else [])
    return jnp.concatenate(cols, axis=1)


def _row_blocking(n_rows, n_steps):
    for steps_per_block in range(1, n_steps + 1):
        if n_steps % steps_per_block == 0 and (n_rows * steps_per_block) % n_steps == 0:
            rows = n_rows * steps_per_block // n_steps
            if rows % BF16_TILE_ROWS == 0:
                return rows, steps_per_block
    raise ValueError(f"no bf16-aligned row blocking of {n_rows} rows over {n_steps} steps")


def _step_block(b, j, *, n_tiles, steps_per_block):
    return ((b * n_tiles + j) // steps_per_block, 0)


def _mixer_call(x, mod, g1, w_in, w_gate, b_gate, ln_g, ln_b, w_sp, bias_map, w_pool, b_pool,
                pool_scale, w_pa, w_pb, w_out, w_up, w_down):
    bsz, s_len, d = x.shape
    ts = MIXER_TILE
    n_tiles = s_len // ts
    assert s_len % ts == 0 and ts % GMLP_BLOCK == 0 and ts >= POOL_HALO and ts % CHANNEL_TILE == 0
    assert w_gate.shape[1] == 2 * d and w_pa.shape == w_pb.shape == w_out.shape == (d, d)
    row = lambda a: a.reshape(1, -1)
    w_packed = _pack_columns([w_in, w_gate, w_pa, w_pb, w_out])
    operands = [
        (x, pl.BlockSpec((1, ts, d), lambda b, j: (b, j, 0))),
        (mod, pl.BlockSpec((1, N_ADA, d), lambda b, j: (b, 0, 0))),
    ]
    for a in (row(g1), w_packed, row(b_gate), row(ln_g), row(ln_b), w_sp, bias_map,
              w_pool.astype(BF16), b_pool, row(pool_scale)):
        operands.append((a, _resident(a.shape)))
    out_specs = [pl.BlockSpec((1, ts, d), lambda b, j: (b, j, 0))]
    out_shapes = [jax.ShapeDtypeStruct(x.shape, F32)]
    for w in (w_up, w_down):
        rows, steps_per_block = _row_blocking(w.shape[0], bsz * n_tiles)
        index_map = functools.partial(_step_block, n_tiles=n_tiles, steps_per_block=steps_per_block)
        operands.append((w, pl.BlockSpec((rows, w.shape[1]), index_map)))
        padded = w.shape[1] + _pitch_pad(w.shape[1])
        out_specs.append(pl.BlockSpec((rows, padded), index_map))
        out_shapes.append(jax.ShapeDtypeStruct((w.shape[0], padded), BF16))
    return pl.pallas_call(
        functools.partial(_mixer_kernel, in_width=w_in.shape[1]),
        grid=(bsz, n_tiles),
        in_specs=[spec for _, spec in operands],
        out_specs=out_specs,
        out_shape=out_shapes,
        scratch_shapes=[pltpu.VMEM((POOL_HALO, w_pool.shape[0] * w_pool.shape[1]), F32)],
        compiler_params=pltpu.CompilerParams(
            dimension_semantics=("arbitrary", "arbitrary"), vmem_limit_bytes=VMEM_LIMIT_BYTES),
        name="mixer",
    )(*[a for a, _ in operands])


def _interleave_rows(x):
    n, d = x.shape
    return jnp.swapaxes(x.reshape(SUBLANES, n // SUBLANES, d), 0, 1).reshape(n, d)


def _deinterleave_rows(y):
    n, d = y.shape
    return jnp.swapaxes(y.reshape(n // SUBLANES, SUBLANES, d), 0, 1).reshape(n, d)


def _delay_rows(a, prev_row):
    n = a.shape[0]
    wrapped = pltpu.roll(a[n - SUBLANES:, :], 1, 0)
    first = jnp.where(lax.broadcasted_iota(jnp.int32, wrapped.shape, 0) == 0, prev_row, wrapped)
    return jnp.concatenate([first, a[:n - SUBLANES, :]], axis=0)


def _channel_kernel(x_ref, mod_ref, g2_ref, w_up_ref, cw_ref, cb_ref, w_down_ref, gf_ref,
                    o_ref, conv_carry_ref, *, final_norm):
    ts, d = x_ref.shape[1], x_ref.shape[2]
    d_ff = w_down_ref.shape[0]
    n_chunks = d_ff // FF_CHUNK
    j = pl.program_id(1)

    @pl.when(j == 0)
    def _():
        conv_carry_ref[...] = jnp.zeros_like(conv_carry_ref)

    x = x_ref[0]
    sh2 = mod_ref[0, 3:4, :]
    sc2 = mod_ref[0, 4:5, :]
    gt2 = mod_ref[0, 5:6, :]
    h2 = _rms_scale(x) * (g2_ref[...] * (1.0 + sc2)) + sh2
    h2b = h2.astype(BF16)

    def up_project(c):
        return [_dot(h2b, w_up_ref[:, pl.ds(c0, FF_CHUNK)]) for c0 in (c * FF_CHUNK, d_ff + c * FF_CHUNK)]

    def conv_cols(pre, c0):
        cols = pl.ds(c0, FF_CHUNK)
        taps = [pre]
        for m in range(1, CONV_WIDTH):
            r = (CONV_WIDTH - 1 - m) * SUBLANES + SUBLANES - 1
            taps.append(_delay_rows(taps[-1], conv_carry_ref[r:r + 1, cols]))
        conv_carry_ref[:, cols] = pre[ts - CONV_CARRY_ROWS:, :]
        out = cb_ref[:, cols]
        for k in range(CONV_WIDTH):
            out = out + taps[CONV_WIDTH - 1 - k] * cw_ref[k:k + 1, cols]
        return out

    nxt = up_project(0)
    acc = jnp.zeros((ts, d), F32)
    for c in range(n_chunks):
        cur = nxt
        if c + 1 < n_chunks:
            nxt = up_project(c + 1)
        gate = conv_cols(cur[0], c * FF_CHUNK)
        val = conv_cols(cur[1], d_ff + c * FF_CHUNK)
        f = gate * jax.nn.sigmoid(gate) * val
        acc = acc + _dot_ref(f.astype(BF16), w_down_ref.at[c * FF_CHUNK:(c + 1) * FF_CHUNK, 0:d])
    x2 = x + gt2 * acc
    if final_norm:
        x2 = _rms_scale(x2) * gf_ref[...]
    o_ref[0] = _deinterleave_rows(x2)


def _channel_call(x, mod, g2, w_up, conv_w, conv_b, w_down, g_final, final_norm):
    bsz, s_len, d = x.shape
    ts = CHANNEL_TILE
    assert s_len % ts == 0 and ts >= CONV_WIDTH * SUBLANES and w_down.shape[0] % FF_CHUNK == 0
    row = lambda a: a.reshape(1, -1)
    operands = [
        (x, pl.BlockSpec((1, ts, d), lambda b, j: (b, j, 0))),
        (mod, pl.BlockSpec((1, N_ADA, d), lambda b, j: (b, 0, 0))),
    ]
    for a in (row(g2), w_up, conv_w, row(conv_b), w_down, row(g_final)):
        operands.append((a, _resident(a.shape)))
    return pl.pallas_call(
        functools.partial(_channel_kernel, final_norm=final_norm),
        grid=(bsz, s_len // ts),
        in_specs=[spec for _, spec in operands],
        out_specs=pl.BlockSpec((1, ts, d), lambda b, j: (b, j, 0)),
        out_shape=jax.ShapeDtypeStruct(x.shape, F32),
        scratch_shapes=[pltpu.VMEM((CONV_CARRY_ROWS, conv_w.shape[1]), F32)],
        compiler_params=pltpu.CompilerParams(
            dimension_semantics=("arbitrary", "arbitrary"), vmem_limit_bytes=VMEM_LIMIT_BYTES),
        name="channel",
    )(*[a for a, _ in operands])


def kernel(x, c, w_ada, b_ada, g_norm1, w_in, ln_v_g, ln_v_b, w_spatial, b_spatial, w_pool, b_pool,
           pool_scale, w_proj_a, w_proj_b, w_gate, b_gate, w_out, g_norm2, w_up, conv_w, conv_b,
           w_down, g_final):
    depth = w_ada.shape[0]
    bsz, s_len, d = x.shape
    assert w_ada.shape[2] % ADA_TILE == 0
    gd = w_in.shape[2] // 3 // A_GROUPS
    for l in range(depth):
        mod = _ada_call(c, w_ada[l], b_ada[l]).reshape(bsz, N_ADA, d)
        bias_map = jnp.repeat(b_spatial[l].T, gd, axis=1)
        x, w_up_bf, w_down_bf = _mixer_call(
            x, mod, g_norm1[l], w_in[l], w_gate[l], b_gate[l], ln_v_g[l], ln_v_b[l], w_spatial[l],
            bias_map, w_pool[l], b_pool[l], pool_scale[l], w_proj_a[l], w_proj_b[l], w_out[l],
            w_up[l], w_down[l])
        x = _channel_call(
            x, mod, g_norm2[l], w_up_bf, conv_w[l], conv_b[l], w_down_bf, g_final,
            final_norm=(l == depth - 1))
    return x
```

```python
import functools
import math

import jax
import jax.numpy as jnp
from jax import lax
from jax.experimental import pallas as pl
from jax.experimental.pallas import tpu as pltpu

EPS = 1e-6
CHUNK = 64
GMLP_BLOCK = 128
A_GROUPS = 8
POOL_WINDOWS = (2, 4, 8, 16)
CONV_WIDTH = 3
N_ADA = 6

SUBLANES = 8
LANES = 128
BF16_TILE_ROWS = 16
POOL_HALO = 16
CONV_CARRY_ROWS = (CONV_WIDTH - 1) * SUBLANES
MXU_WIDTH = 256
MIXER_TILE = 512
CHANNEL_TILE = 512
FF_CHUNK = 256
ADA_TILE = 1024
VMEM_LIMIT_BYTES = 56 * 1024 * 1024

BF16 = jnp.bfloat16
F32 = jnp.float32


def _dot(a, b):
    return jnp.dot(a, b, preferred_element_type=F32)


def _dot_ref(a, w_ref):
    n = w_ref.shape[-1]
    assert n % MXU_WIDTH == 0
    return jnp.concatenate(
        [_dot(a, w_ref[:, c0:c0 + MXU_WIDTH]) for c0 in range(0, n, MXU_WIDTH)], axis=1)


def _gelu(x):
    return 0.5 * x * (1.0 + lax.erf(x * math.sqrt(0.5)))


def _rms_scale(x):
    ms = jnp.mean(x * x, axis=-1, keepdims=True)
    return x * lax.rsqrt(ms + EPS)


def _ada_kernel(c_ref, w_ref, b_ref, o_ref):
    c = c_ref[...]
    s = c * jax.nn.sigmoid(c)
    o_ref[...] = _dot(s.astype(BF16), w_ref[...].astype(BF16)) + b_ref[...]


def _ada_call(c, w_ada, b_ada):
    bsz, d = c.shape
    n = w_ada.shape[1]
    return pl.pallas_call(
        _ada_kernel,
        grid=(n // ADA_TILE,),
        in_specs=[
            pl.BlockSpec((bsz, d), lambda j: (0, 0)),
            pl.BlockSpec((d, ADA_TILE), lambda j: (0, j)),
            pl.BlockSpec((1, ADA_TILE), lambda j: (0, j)),
        ],
        out_specs=pl.BlockSpec((bsz, ADA_TILE), lambda j: (0, j)),
        out_shape=jax.ShapeDtypeStruct((bsz, n), F32),
        compiler_params=pltpu.CompilerParams(dimension_semantics=("arbitrary",)),
        name="ada_mod",
    )(c, w_ada, b_ada.reshape(1, n))


def _cast_block(src_ref, dst_ref):
    n = src_ref.shape[1]
    dst_ref[:, 0:n] = src_ref[...].astype(BF16)
    if dst_ref.shape[1] > n:
        dst_ref[:, n:] = jnp.zeros((dst_ref.shape[0], dst_ref.shape[1] - n), BF16)


def _mixer_kernel(x_ref, mod_ref, g1_ref, w_ref, b_gate_ref, lng_ref, lnb_ref,
                  wsp_ref, bsp_ref, wpool_ref, bpool_ref, pscale_ref, w_up_ref, w_down_ref,
                  o_ref, w_up_bf_ref, w_down_bf_ref, pool_carry_ref, *, in_width):
    ts, d = x_ref.shape[1], x_ref.shape[2]
    j = pl.program_id(1)
    _cast_block(w_up_ref, w_up_bf_ref)
    _cast_block(w_down_ref, w_down_bf_ref)
    w_in_ref = w_ref.at[:, 0:in_width]
    w_gate_ref = w_ref.at[:, in_width:in_width + 2 * d]
    wpa_ref = w_ref.at[:, in_width + 2 * d:in_width + 3 * d]
    wpb_ref = w_ref.at[:, in_width + 3 * d:in_width + 4 * d]
    wout_ref = w_ref.at[:, in_width + 4 * d:in_width + 5 * d]

    @pl.when(j == 0)
    def _():
        pool_carry_ref[...] = jnp.zeros_like(pool_carry_ref)

    x = x_ref[0]
    sh1 = mod_ref[0, 0:1, :]
    sc1 = mod_ref[0, 1:2, :]
    gt1 = mod_ref[0, 2:3, :]
    h = _rms_scale(x) * (g1_ref[...] * (1.0 + sc1)) + sh1
    hb = h.astype(BF16)

    a_w = d
    v = _gelu(_dot_ref(hb, w_in_ref.at[:, a_w:2 * a_w]))
    u = _gelu(_dot_ref(hb, w_in_ref.at[:, 0:a_w]))
    hbp = _dot_ref(hb, w_in_ref.at[:, 2 * a_w:w_in_ref.shape[1]])
    gates = jax.nn.sigmoid(_dot_ref(hb, w_gate_ref) + b_gate_ref[...])
    mu = jnp.mean(v, axis=-1, keepdims=True)
    vc = v - mu
    var = jnp.mean(vc * vc, axis=-1, keepdims=True)
    vn = (vc * lax.rsqrt(var + EPS)) * lng_ref[...] + lnb_ref[...]
    vnb = vn.astype(BF16)

    ext = jnp.concatenate([pool_carry_ref[...], hbp], axis=0)
    pool_carry_ref[...] = hbp[ts - POOL_HALO:, :]
    bgd = hbp.shape[1] // len(POOL_WINDOWS)
    t = j * ts + lax.broadcasted_iota(jnp.int32, (ts, bgd), 0)
    yb_cols = []
    for gi, w in enumerate(POOL_WINDOWS):
        e = ext[:, gi * bgd:(gi + 1) * bgd]
        acc, span = e, 1
        while span < w:
            acc = acc + pltpu.roll(acc, span, 0)
            span *= 2
        win = acc[POOL_HALO:, :]
        cur = e[POOL_HALO:, :]
        count = jnp.minimum(t + 1, w).astype(F32)
        pooled = win / count - cur
        mixed = _dot(pooled.astype(BF16), wpool_ref[gi]) + bpool_ref[gi:gi + 1, :]
        yb_cols.append(mixed)
    y_b = jnp.concatenate(yb_cols, axis=1) * pscale_ref[...]
    branch_b = gates[:, d:] * _dot_ref(y_b.astype(BF16), wpb_ref)

    gd = a_w // A_GROUPS
    p = lax.broadcasted_iota(jnp.int32, (GMLP_BLOCK, GMLP_BLOCK), 0)
    q = lax.broadcasted_iota(jnp.int32, (GMLP_BLOCK, GMLP_BLOCK), 1)
    allowed = (q // CHUNK) <= (p // CHUNK)
    w_masked = [jnp.where(allowed, wsp_ref[g], 0.0).astype(BF16) for g in range(A_GROUPS)]
    bias_map = bsp_ref[...]
    s_rows = []
    for n in range(ts // GMLP_BLOCK):
        r0 = n * GMLP_BLOCK
        cols = [_dot(w_masked[g], vnb[r0:r0 + GMLP_BLOCK, g * gd:(g + 1) * gd]) for g in range(A_GROUPS)]
        s_rows.append(jnp.concatenate(cols, axis=1) + bias_map)
    s = jnp.concatenate(s_rows, axis=0)
    y_a = u * s

    merged = gates[:, :d] * _dot_ref(y_a.astype(BF16), wpa_ref) + branch_b
    x1 = x + gt1 * _dot_ref(merged.astype(BF16), wout_ref)
    o_ref[0] = jnp.concatenate(
        [_interleave_rows(x1[r:r + CHANNEL_TILE, :]) for r in range(0, ts, CHANNEL_TILE)], axis=0)


def _resident(shape):
    nd = len(shape)
    return pl.BlockSpec(shape, lambda b, j: (0,) * nd, pipeline_mode=pl.Buffered(1))


def _pitch_pad(n_cols):
    return LANES if (n_cols // LANES) % SUBLANES == 0 else 0


def _pack_columns(ws):
    rows = ws[0].shape[0]
    pad = _pitch_pad(sum(w.shape[1] for w in ws))
    cols = [w.astype(BF16) for w in ws] + ([jnp.zeros((rows, pad), BF16)] if pad else [])
    return jnp.concatenate(cols, axis=1)


def _row_blocking(n_rows, n_steps):
    for steps_per_block in range(1, n_steps + 1):
        if n_steps % steps_per_block == 0 and (n_rows * steps_per_block) % n_steps == 0:
            rows = n_rows * steps_per_block // n_steps
            if rows % BF16_TILE_ROWS == 0:
                return rows, steps_per_block
    raise ValueError(f"no bf16-aligned row blocking of {n_rows} rows over {n_steps} steps")


def _step_block(b, j, *, n_tiles, steps_per_block):
    return ((b * n_tiles + j) // steps_per_block, 0)


def _mixer_call(x, mod, g1, w_in, w_gate, b_gate, ln_g, ln_b, w_sp, bias_map, w_pool, b_pool,
                pool_scale, w_pa, w_pb, w_out, w_up, w_down):
    bsz, s_len, d = x.shape
    ts = MIXER_TILE
    n_tiles = s_len // ts
    assert s_len % ts == 0 and ts % GMLP_BLOCK == 0 and ts >= POOL_HALO and ts % CHANNEL_TILE == 0
    assert w_gate.shape[1] == 2 * d and w_pa.shape == w_pb.shape == w_out.shape == (d, d)
    row = lambda a: a.reshape(1, -1)
    w_packed = _pack_columns([w_in, w_gate, w_pa, w_pb, w_out])
    operands = [
        (x, pl.BlockSpec((1, ts, d), lambda b, j: (b, j, 0))),
        (mod, pl.BlockSpec((1, N_ADA, d), lambda b, j: (b, 0, 0))),
    ]
    for a in (row(g1), w_packed, row(b_gate), row(ln_g), row(ln_b), w_sp, bias_map,
              w_pool.astype(BF16), b_pool, row(pool_scale)):
        operands.append((a, _resident(a.shape)))
    out_specs = [pl.BlockSpec((1, ts, d), lambda b, j: (b, j, 0))]
    out_shapes = [jax.ShapeDtypeStruct(x.shape, F32)]
    for w in (w_up, w_down):
        rows, steps_per_block = _row_blocking(w.shape[0], bsz * n_tiles)
        index_map = functools.partial(_step_block, n_tiles=n_tiles, steps_per_block=steps_per_block)
        operands.append((w, pl.BlockSpec((rows, w.shape[1]), index_map)))
        padded = w.shape[1] + _pitch_pad(w.shape[1])
        out_specs.append(pl.BlockSpec((rows, padded), index_map))
        out_shapes.append(jax.ShapeDtypeStruct((w.shape[0], padded), BF16))
    return pl.pallas_call(
        functools.partial(_mixer_kernel, in_width=w_in.shape[1]),
        grid=(bsz, n_tiles),
        in_specs=[spec for _, spec in operands],
        out_specs=out_specs,
        out_shape=out_shapes,
        scratch_shapes=[pltpu.VMEM((POOL_HALO, w_pool.shape[0] * w_pool.shape[1]), F32)],
        compiler_params=pltpu.CompilerParams(
            dimension_semantics=("arbitrary", "arbitrary"), vmem_limit_bytes=VMEM_LIMIT_BYTES),
        name="mixer",
    )(*[a for a, _ in operands])


def _interleave_rows(x):
    n, d = x.shape
    return jnp.swapaxes(x.reshape(SUBLANES, n // SUBLANES, d), 0, 1).reshape(n, d)


def _deinterleave_rows(y):
    n, d = y.shape
    return jnp.swapaxes(y.reshape(n // SUBLANES, SUBLANES, d), 0, 1).reshape(n, d)


def _delay_rows(a, prev_row):
    n = a.shape[0]
    wrapped = pltpu.roll(a[n - SUBLANES:, :], 1, 0)
    first = jnp.where(lax.broadcasted_iota(jnp.int32, wrapped.shape, 0) == 0, prev_row, wrapped)
    return jnp.concatenate([first, a[:n - SUBLANES, :]], axis=0)


def _channel_kernel(x_ref, mod_ref, g2_ref, w_up_ref, cw_ref, cb_ref, w_down_ref, gf_ref,
                    o_ref, conv_carry_ref, *, final_norm):
    ts, d = x_ref.shape[1], x_ref.shape[2]
    d_ff = w_down_ref.shape[0]
    n_chunks = d_ff // FF_CHUNK
    j = pl.program_id(1)

    @pl.when(j == 0)
    def _():
        conv_carry_ref[...] = jnp.zeros_like(conv_carry_ref)

    x = x_ref[0]
    sh2 = mod_ref[0, 3:4, :]
    sc2 = mod_ref[0, 4:5, :]
    gt2 = mod_ref[0, 5:6, :]
    h2 = _rms_scale(x) * (g2_ref[...] * (1.0 + sc2)) + sh2
    h2b = h2.astype(BF16)

    def up_project(c):
        return [_dot(h2b, w_up_ref[:, pl.ds(c0, FF_CHUNK)]) for c0 in (c * FF_CHUNK, d_ff + c * FF_CHUNK)]

    def conv_cols(pre, c0):
        cols = pl.ds(c0, FF_CHUNK)
        taps = [pre]
        for m in range(1, CONV_WIDTH):
            r = (CONV_WIDTH - 1 - m) * SUBLANES + SUBLANES - 1
            taps.append(_delay_rows(taps[-1], conv_carry_ref[r:r + 1, cols]))
        conv_carry_ref[:, cols] = pre[ts - CONV_CARRY_ROWS:, :]
        out = cb_ref[:, cols]
        for k in range(CONV_WIDTH):
            out = out + taps[CONV_WIDTH - 1 - k] * cw_ref[k:k + 1, cols]
        return out

    nxt = up_project(0)
    acc = jnp.zeros((ts, d), F32)
    for c in range(n_chunks):
        cur = nxt
        if c + 1 < n_chunks:
            nxt = up_project(c + 1)
        gate = conv_cols(cur[0], c * FF_CHUNK)
        val = conv_cols(cur[1], d_ff + c * FF_CHUNK)
        f = gate * jax.nn.sigmoid(gate) * val
        acc = acc + _dot_ref(f.astype(BF16), w_down_ref.at[c * FF_CHUNK:(c + 1) * FF_CHUNK, 0:d])
    x2 = x + gt2 * acc
    if final_norm:
        x2 = _rms_scale(x2) * gf_ref[...]
    o_ref[0] = _deinterleave_rows(x2)


def _channel_call(x, mod, g2, w_up, conv_w, conv_b, w_down, g_final, final_norm):
    bsz, s_len, d = x.shape
    ts = CHANNEL_TILE
    assert s_len % ts == 0 and ts >= CONV_WIDTH * SUBLANES and w_down.shape[0] % FF_CHUNK == 0
    row = lambda a: a.reshape(1, -1)
    operands = [
        (x, pl.BlockSpec((1, ts, d), lambda b, j: (b, j, 0))),
        (mod, pl.BlockSpec((1, N_ADA, d), lambda b, j: (b, 0, 0))),
    ]
    for a in (row(g2), w_up, conv_w, row(conv_b), w_down, row(g_final)):
        operands.append((a, _resident(a.shape)))
    return pl.pallas_call(
        functools.partial(_channel_kernel, final_norm=final_norm),
        grid=(bsz, s_len // ts),
        in_specs=[spec for _, spec in operands],
        out_specs=pl.BlockSpec((1, ts, d), lambda b, j: (b, j, 0)),
        out_shape=jax.ShapeDtypeStruct(x.shape, F32),
        scratch_shapes=[pltpu.VMEM((CONV_CARRY_ROWS, conv_w.shape[1]), F32)],
        compiler_params=pltpu.CompilerParams(
            dimension_semantics=("arbitrary", "arbitrary"), vmem_limit_bytes=VMEM_LIMIT_BYTES),
        name="channel",
    )(*[a for a, _ in operands])


def kernel(x, c, w_ada, b_ada, g_norm1, w_in, ln_v_g, ln_v_b, w_spatial, b_spatial, w_pool, b_pool,
           pool_scale, w_proj_a, w_proj_b, w_gate, b_gate, w_out, g_norm2, w_up, conv_w, conv_b,
           w_down, g_final):
    depth = w_ada.shape[0]
    bsz, s_len, d = x.shape
    assert w_ada.shape[2] % ADA_TILE == 0
    gd = w_in.shape[2] // 3 // A_GROUPS
    for l in range(depth):
        mod = _ada_call(c, w_ada[l], b_ada[l]).reshape(bsz, N_ADA, d)
        bias_map = jnp.repeat(b_spatial[l].T, gd, axis=1)
        x, w_up_bf, w_down_bf = _mixer_call(
            x, mod, g_norm1[l], w_in[l], w_gate[l], b_gate[l], ln_v_g[l], ln_v_b[l], w_spatial[l],
            bias_map, w_pool[l], b_pool[l], pool_scale[l], w_proj_a[l], w_proj_b[l], w_out[l],
            w_up[l], w_down[l])
        x = _channel_call(
            x, mod, g_norm2[l], w_up_bf, conv_w[l], conv_b[l], w_down_bf, g_final,
            final_norm=(l == depth - 1))
    return x
```

```python
import functools
import math

import jax
import jax.numpy as jnp
from jax import lax
from jax.experimental import pallas as pl
from jax.experimental.pallas import tpu as pltpu

EPS = 1e-6
CHUNK = 64
GMLP_BLOCK = 128
A_GROUPS = 8
POOL_WINDOWS = (2, 4, 8, 16)
CONV_WIDTH = 3
N_ADA = 6

SUBLANES = 8
LANES = 128
BF16_TILE_ROWS = 16
POOL_HALO = 16
CONV_CARRY_ROWS = (CONV_WIDTH - 1) * SUBLANES
MXU_WIDTH = 256
MIXER_TILE = 512
CHANNEL_TILE = 256
FF_CHUNK = 256
ADA_TILE = 2048
VMEM_LIMIT_BYTES = 56 * 1024 * 1024

BF16 = jnp.bfloat16
F32 = jnp.float32


def _dot(a, b):
    return jnp.dot(a, b, preferred_element_type=F32)


def _dot_ref(a, w_ref):
    n = w_ref.shape[-1]
    assert n % MXU_WIDTH == 0
    return jnp.concatenate(
        [_dot(a, w_ref[:, c0:c0 + MXU_WIDTH]) for c0 in range(0, n, MXU_WIDTH)], axis=1)


def _gelu(x):
    return 0.5 * x * (1.0 + lax.erf(x * math.sqrt(0.5)))


def _rms_scale(x):
    ms = jnp.mean(x * x, axis=-1, keepdims=True)
    return x * lax.rsqrt(ms + EPS)


def _ada_kernel(c_ref, w_ref, b_ref, o_ref):
    c = c_ref[...]
    s = c * jax.nn.sigmoid(c)
    o_ref[...] = _dot(s.astype(BF16), w_ref[...].astype(BF16)) + b_ref[...]


def _ada_call(c, w_ada, b_ada):
    bsz, d = c.shape
    n = w_ada.shape[1]
    return pl.pallas_call(
        _ada_kernel,
        grid=(n // ADA_TILE,),
        in_specs=[
            pl.BlockSpec((bsz, d), lambda j: (0, 0)),
            pl.BlockSpec((d, ADA_TILE), lambda j: (0, j)),
            pl.BlockSpec((1, ADA_TILE), lambda j: (0, j)),
        ],
        out_specs=pl.BlockSpec((bsz, ADA_TILE), lambda j: (0, j)),
        out_shape=jax.ShapeDtypeStruct((bsz, n), F32),
        compiler_params=pltpu.CompilerParams(dimension_semantics=("arbitrary",)),
        name="ada_mod",
    )(c, w_ada, b_ada.reshape(1, n))


def _cast_block(src_ref, dst_ref):
    n = src_ref.shape[1]
    dst_ref[:, 0:n] = src_ref[...].astype(BF16)
    if dst_ref.shape[1] > n:
        dst_ref[:, n:] = jnp.zeros((dst_ref.shape[0], dst_ref.shape[1] - n), BF16)


def _mixer_kernel(x_ref, mod_ref, g1_ref, w_ref, b_gate_ref, lng_ref, lnb_ref,
                  wsp_ref, bsp_ref, wpool_ref, bpool_ref, pscale_ref, w_up_ref, w_down_ref,
                  o_ref, w_up_bf_ref, w_down_bf_ref, pool_carry_ref, *, in_width):
    ts, d = x_ref.shape[1], x_ref.shape[2]
    j = pl.program_id(1)
    _cast_block(w_up_ref, w_up_bf_ref)
    _cast_block(w_down_ref, w_down_bf_ref)
    w_in_ref = w_ref.at[:, 0:in_width]
    w_gate_ref = w_ref.at[:, in_width:in_width + 2 * d]
    wpa_ref = w_ref.at[:, in_width + 2 * d:in_width + 3 * d]
    wpb_ref = w_ref.at[:, in_width + 3 * d:in_width + 4 * d]
    wout_ref = w_ref.at[:, in_width + 4 * d:in_width + 5 * d]

    @pl.when(j == 0)
    def _():
        pool_carry_ref[...] = jnp.zeros_like(pool_carry_ref)

    x = x_ref[0]
    sh1 = mod_ref[0, 0:1, :]
    sc1 = mod_ref[0, 1:2, :]
    gt1 = mod_ref[0, 2:3, :]
    h = _rms_scale(x) * (g1_ref[...] * (1.0 + sc1)) + sh1
    hb = h.astype(BF16)

    a_w = d
    v = _gelu(_dot_ref(hb, w_in_ref.at[:, a_w:2 * a_w]))
    u = _gelu(_dot_ref(hb, w_in_ref.at[:, 0:a_w]))
    hbp = _dot_ref(hb, w_in_ref.at[:, 2 * a_w:w_in_ref.shape[1]])
    gates = jax.nn.sigmoid(_dot_ref(hb, w_gate_ref) + b_gate_ref[...])
    mu = jnp.mean(v, axis=-1, keepdims=True)
    vc = v - mu
    var = jnp.mean(vc * vc, axis=-1, keepdims=True)
    vn = (vc * lax.rsqrt(var + EPS)) * lng_ref[...] + lnb_ref[...]
    vnb = vn.astype(BF16)

    ext = jnp.concatenate([pool_carry_ref[...], hbp], axis=0)
    pool_carry_ref[...] = hbp[ts - POOL_HALO:, :]
    bgd = hbp.shape[1] // len(POOL_WINDOWS)
    t = j * ts + lax.broadcasted_iota(jnp.int32, (ts, bgd), 0)
    yb_cols = []
    for gi, w in enumerate(POOL_WINDOWS):
        e = ext[:, gi * bgd:(gi + 1) * bgd]
        acc, span = e, 1
        while span < w:
            acc = acc + pltpu.roll(acc, span, 0)
            span *= 2
        win = acc[POOL_HALO:, :]
        cur = e[POOL_HALO:, :]
        count = jnp.minimum(t + 1, w).astype(F32)
        pooled = win / count - cur
        mixed = _dot(pooled.astype(BF16), wpool_ref[gi]) + bpool_ref[gi:gi + 1, :]
        yb_cols.append(mixed)
    y_b = jnp.concatenate(yb_cols, axis=1) * pscale_ref[...]
    branch_b = gates[:, d:] * _dot_ref(y_b.astype(BF16), wpb_ref)

    gd = a_w // A_GROUPS
    p = lax.broadcasted_iota(jnp.int32, (GMLP_BLOCK, GMLP_BLOCK), 0)
    q = lax.broadcasted_iota(jnp.int32, (GMLP_BLOCK, GMLP_BLOCK), 1)
    allowed = (q // CHUNK) <= (p // CHUNK)
    w_masked = [jnp.where(allowed, wsp_ref[g], 0.0).astype(BF16) for g in range(A_GROUPS)]
    bias_map = bsp_ref[...]
    s_rows = []
    for n in range(ts // GMLP_BLOCK):
        r0 = n * GMLP_BLOCK
        cols = [_dot(w_masked[g], vnb[r0:r0 + GMLP_BLOCK, g * gd:(g + 1) * gd]) for g in range(A_GROUPS)]
        s_rows.append(jnp.concatenate(cols, axis=1) + bias_map)
    s = jnp.concatenate(s_rows, axis=0)
    y_a = u * s

    merged = gates[:, :d] * _dot_ref(y_a.astype(BF16), wpa_ref) + branch_b
    x1 = x + gt1 * _dot_ref(merged.astype(BF16), wout_ref)
    o_ref[0] = jnp.concatenate(
        [_interleave_rows(x1[r:r + CHANNEL_TILE, :]) for r in range(0, ts, CHANNEL_TILE)], axis=0)


def _resident(shape):
    nd = len(shape)
    return pl.BlockSpec(shape, lambda b, j: (0,) * nd, pipeline_mode=pl.Buffered(1))


def _pitch_pad(n_cols):
    return LANES if (n_cols // LANES) % SUBLANES == 0 else 0


def _pack_columns(ws):
    rows = ws[0].shape[0]
    pad = _pitch_pad(sum(w.shape[1] for w in ws))
    cols = list(ws) + ([jnp.zeros((rows, pad), ws[0].dtype)] if pad else [])
    return jnp.concatenate(cols, axis=1).astype(BF16)


def _row_blocking(n_rows, n_steps):
    for steps_per_block in range(1, n_steps + 1):
        if n_steps % steps_per_block == 0 and (n_rows * steps_per_block) % n_steps == 0:
            rows = n_rows * steps_per_block // n_steps
            if rows % BF16_TILE_ROWS == 0:
                return rows, steps_per_block
    raise ValueError(f"no bf16-aligned row blocking of {n_rows} rows over {n_steps} steps")


def _step_block(b, j, *, n_tiles, steps_per_block):
    return ((b * n_tiles + j) // steps_per_block, 0)


def _mixer_call(x, mod, g1, w_in, w_gate, b_gate, ln_g, ln_b, w_sp, bias_map, w_pool, b_pool,
                pool_scale, w_pa, w_pb, w_out, w_up, w_down):
    bsz, s_len, d = x.shape
    ts = MIXER_TILE
    n_tiles = s_len // ts
    assert s_len % ts == 0 and ts % GMLP_BLOCK == 0 and ts >= POOL_HALO and ts % CHANNEL_TILE == 0
    assert w_gate.shape[1] == 2 * d and w_pa.shape == w_pb.shape == w_out.shape == (d, d)
    row = lambda a: a.reshape(1, -1)
    w_packed = _pack_columns([w_in, w_gate, w_pa, w_pb, w_out])
    operands = [
        (x, pl.BlockSpec((1, ts, d), lambda b, j: (b, j, 0))),
        (mod, pl.BlockSpec((1, N_ADA, d), lambda b, j: (b, 0, 0))),
    ]
    for a in (row(g1), w_packed, row(b_gate), row(ln_g), row(ln_b), w_sp, bias_map,
              w_pool.astype(BF16), b_pool, row(pool_scale)):
        operands.append((a, _resident(a.shape)))
    out_specs = [pl.BlockSpec((1, ts, d), lambda b, j: (b, j, 0))]
    out_shapes = [jax.ShapeDtypeStruct(x.shape, F32)]
    for w in (w_up, w_down):
        rows, steps_per_block = _row_blocking(w.shape[0], bsz * n_tiles)
        index_map = functools.partial(_step_block, n_tiles=n_tiles, steps_per_block=steps_per_block)
        operands.append((w, pl.BlockSpec((rows, w.shape[1]), index_map)))
        padded = w.shape[1] + _pitch_pad(w.shape[1])
        out_specs.append(pl.BlockSpec((rows, padded), index_map))
        out_shapes.append(jax.ShapeDtypeStruct((w.shape[0], padded), BF16))
    return pl.pallas_call(
        functools.partial(_mixer_kernel, in_width=w_in.shape[1]),
        grid=(bsz, n_tiles),
        in_specs=[spec for _, spec in operands],
        out_specs=out_specs,
        out_shape=out_shapes,
        scratch_shapes=[pltpu.VMEM((POOL_HALO, w_pool.shape[0] * w_pool.shape[1]), F32)],
        compiler_params=pltpu.CompilerParams(
            dimension_semantics=("arbitrary", "arbitrary"), vmem_limit_bytes=VMEM_LIMIT_BYTES),
        name="mixer",
    )(*[a for a, _ in operands])


def _interleave_rows(x):
    n, d = x.shape
    return jnp.swapaxes(x.reshape(SUBLANES, n // SUBLANES, d), 0, 1).reshape(n, d)


def _deinterleave_rows(y):
    n, d = y.shape
    return jnp.swapaxes(y.reshape(n // SUBLANES, SUBLANES, d), 0, 1).reshape(n, d)


def _delay_rows(a, prev_row):
    n = a.shape[0]
    wrapped = pltpu.roll(a[n - SUBLANES:, :], 1, 0)
    first = jnp.where(lax.broadcasted_iota(jnp.int32, wrapped.shape, 0) == 0, prev_row, wrapped)
    return jnp.concatenate([first, a[:n - SUBLANES, :]], axis=0)


def _channel_kernel(x_ref, mod_ref, g2_ref, w_up_ref, cw_ref, cb_ref, w_down_ref, gf_ref,
                    o_ref, conv_carry_ref, *, final_norm):
    ts, d = x_ref.shape[1], x_ref.shape[2]
    d_ff = w_down_ref.shape[0]
    n_chunks = d_ff // FF_CHUNK
    j = pl.program_id(1)

    @pl.when(j == 0)
    def _():
        conv_carry_ref[...] = jnp.zeros_like(conv_carry_ref)

    x = x_ref[0]
    sh2 = mod_ref[0, 3:4, :]
    sc2 = mod_ref[0, 4:5, :]
    gt2 = mod_ref[0, 5:6, :]
    h2 = _rms_scale(x) * (g2_ref[...] * (1.0 + sc2)) + sh2
    h2b = h2.astype(BF16)

    def up_project(c):
        return [_dot(h2b, w_up_ref[:, pl.ds(c0, FF_CHUNK)]) for c0 in (c * FF_CHUNK, d_ff + c * FF_CHUNK)]

    def conv_cols(pre, c0):
        cols = pl.ds(c0, FF_CHUNK)
        taps = [pre]
        for m in range(1, CONV_WIDTH):
            r = (CONV_WIDTH - 1 - m) * SUBLANES + SUBLANES - 1
            taps.append(_delay_rows(taps[-1], conv_carry_ref[r:r + 1, cols]))
        conv_carry_ref[:, cols] = pre[ts - CONV_CARRY_ROWS:, :]
        out = cb_ref[:, cols]
        for k in range(CONV_WIDTH):
            out = out + taps[CONV_WIDTH - 1 - k] * cw_ref[k:k + 1, cols]
        return out

    nxt = up_project(0)
    acc = jnp.zeros((ts, d), F32)
    for c in range(n_chunks):
        cur = nxt
        if c + 1 < n_chunks:
            nxt = up_project(c + 1)
        gate = conv_cols(cur[0], c * FF_CHUNK)
        val = conv_cols(cur[1], d_ff + c * FF_CHUNK)
        f = gate * jax.nn.sigmoid(gate) * val
        acc = acc + _dot_ref(f.astype(BF16), w_down_ref.at[c * FF_CHUNK:(c + 1) * FF_CHUNK, 0:d])
    x2 = x + gt2 * acc
    if final_norm:
        x2 = _rms_scale(x2) * gf_ref[...]
    o_ref[0] = _deinterleave_rows(x2)


def _channel_call(x, mod, g2, w_up, conv_w, conv_b, w_down, g_final, final_norm):
    bsz, s_len, d = x.shape
    ts = CHANNEL_TILE
    assert s_len % ts == 0 and ts >= CONV_WIDTH * SUBLANES and w_down.shape[0] % FF_CHUNK == 0
    row = lambda a: a.reshape(1, -1)
    operands = [
        (x, pl.BlockSpec((1, ts, d), lambda b, j: (b, j, 0))),
        (mod, pl.BlockSpec((1, N_ADA, d), lambda b, j: (b, 0, 0))),
    ]
    for a in (row(g2), w_up, conv_w, row(conv_b), w_down, row(g_final)):
        operands.append((a, _resident(a.shape)))
    return pl.pallas_call(
        functools.partial(_channel_kernel, final_norm=final_norm),
        grid=(bsz, s_len // ts),
        in_specs=[spec for _, spec in operands],
        out_specs=pl.BlockSpec((1, ts, d), lambda b, j: (b, j, 0)),
        out_shape=jax.ShapeDtypeStruct(x.shape, F32),
        scratch_shapes=[pltpu.VMEM((CONV_CARRY_ROWS, conv_w.shape[1]), F32)],
        compiler_params=pltpu.CompilerParams(
            dimension_semantics=("arbitrary", "arbitrary"), vmem_limit_bytes=VMEM_LIMIT_BYTES),
        name="channel",
    )(*[a for a, _ in operands])


def kernel(x, c, w_ada, b_ada, g_norm1, w_in, ln_v_g, ln_v_b, w_spatial, b_spatial, w_pool, b_pool,
           pool_scale, w_proj_a, w_proj_b, w_gate, b_gate, w_out, g_norm2, w_up, conv_w, conv_b,
           w_down, g_final):
    depth = w_ada.shape[0]
    bsz, s_len, d = x.shape
    assert w_ada.shape[2] % ADA_TILE == 0
    gd = w_in.shape[2] // 3 // A_GROUPS
    for l in range(depth):
        mod = _ada_call(c, w_ada[l], b_ada[l]).reshape(bsz, N_ADA, d)
        bias_map = jnp.repeat(b_spatial[l].T, gd, axis=1)
        x, w_up_bf, w_down_bf = _mixer_call(
            x, mod, g_norm1[l], w_in[l], w_gate[l], b_gate[l], ln_v_g[l], ln_v_b[l], w_spatial[l],
            bias_map, w_pool[l], b_pool[l], pool_scale[l], w_proj_a[l], w_proj_b[l], w_out[l],
            w_up[l], w_down[l])
        x = _channel_call(
            x, mod, g_norm2[l], w_up_bf, conv_w[l], conv_b[l], w_down_bf, g_final,
            final_norm=(l == depth - 1))
    return x
```

```python
import functools
import math

import jax
import jax.numpy as jnp
from jax import lax
from jax.experimental import pallas as pl
from jax.experimental.pallas import tpu as pltpu

EPS = 1e-6
CHUNK = 64
GMLP_BLOCK = 128
A_GROUPS = 8
POOL_WINDOWS = (2, 4, 8, 16)
CONV_WIDTH = 3
N_ADA = 6

SUBLANES = 8
LANES = 128
BF16_TILE_ROWS = 16
POOL_HALO = 16
CONV_CARRY_ROWS = (CONV_WIDTH - 1) * SUBLANES
MXU_WIDTH = 256
MIXER_TILE = 512
CHANNEL_TILE = 256
FF_CHUNK = 256
UP_LOOKAHEAD = 3
ADA_TILE = 2048
VMEM_LIMIT_BYTES = 56 * 1024 * 1024

BF16 = jnp.bfloat16
F32 = jnp.float32


def _dot(a, b):
    return jnp.dot(a, b, preferred_element_type=F32)


def _dot_ref(a, w_ref):
    n = w_ref.shape[-1]
    assert n % MXU_WIDTH == 0
    return jnp.concatenate(
        [_dot(a, w_ref[:, c0:c0 + MXU_WIDTH]) for c0 in range(0, n, MXU_WIDTH)], axis=1)


def _gelu(x):
    return 0.5 * x * (1.0 + lax.erf(x * math.sqrt(0.5)))


def _rms_scale(x):
    ms = jnp.mean(x * x, axis=-1, keepdims=True)
    return x * lax.rsqrt(ms + EPS)


def _ada_kernel(c_ref, w_ref, b_ref, o_ref):
    c = c_ref[...]
    s = c * jax.nn.sigmoid(c)
    o_ref[...] = _dot(s.astype(BF16), w_ref[...].astype(BF16)) + b_ref[...]


def _ada_call(c, w_ada, b_ada):
    bsz, d = c.shape
    n = w_ada.shape[1]
    return pl.pallas_call(
        _ada_kernel,
        grid=(n // ADA_TILE,),
        in_specs=[
            pl.BlockSpec((bsz, d), lambda j: (0, 0)),
            pl.BlockSpec((d, ADA_TILE), lambda j: (0, j)),
            pl.BlockSpec((1, ADA_TILE), lambda j: (0, j)),
        ],
        out_specs=pl.BlockSpec((bsz, ADA_TILE), lambda j: (0, j)),
        out_shape=jax.ShapeDtypeStruct((bsz, n), F32),
        compiler_params=pltpu.CompilerParams(dimension_semantics=("arbitrary",)),
        name="ada_mod",
    )(c, w_ada, b_ada.reshape(1, n))


def _cast_block(src_ref, dst_ref):
    n = src_ref.shape[1]
    dst_ref[:, 0:n] = src_ref[...].astype(BF16)
    if dst_ref.shape[1] > n:
        dst_ref[:, n:] = jnp.zeros((dst_ref.shape[0], dst_ref.shape[1] - n), BF16)


def _mixer_kernel(x_ref, mod_ref, g1_ref, w_ref, b_gate_ref, lng_ref, lnb_ref,
                  wsp_ref, bsp_ref, wpool_ref, bpool_ref, pscale_ref, w_up_ref, w_down_ref,
                  o_ref, w_up_bf_ref, w_down_bf_ref, pool_carry_ref, *, in_width):
    ts, d = x_ref.shape[1], x_ref.shape[2]
    j = pl.program_id(1)
    _cast_block(w_up_ref, w_up_bf_ref)
    _cast_block(w_down_ref, w_down_bf_ref)
    w_in_ref = w_ref.at[:, 0:in_width]
    w_gate_ref = w_ref.at[:, in_width:in_width + 2 * d]
    wpa_ref = w_ref.at[:, in_width + 2 * d:in_width + 3 * d]
    wpb_ref = w_ref.at[:, in_width + 3 * d:in_width + 4 * d]
    wout_ref = w_ref.at[:, in_width + 4 * d:in_width + 5 * d]

    @pl.when(j == 0)
    def _():
        pool_carry_ref[...] = jnp.zeros_like(pool_carry_ref)

    x = x_ref[0]
    sh1 = mod_ref[0, 0:1, :]
    sc1 = mod_ref[0, 1:2, :]
    gt1 = mod_ref[0, 2:3, :]
    h = _rms_scale(x) * (g1_ref[...] * (1.0 + sc1)) + sh1
    hb = h.astype(BF16)

    a_w = d
    v = _gelu(_dot_ref(hb, w_in_ref.at[:, a_w:2 * a_w]))
    hbp = _dot_ref(hb, w_in_ref.at[:, 2 * a_w:w_in_ref.shape[1]])
    gates = jax.nn.sigmoid(_dot_ref(hb, w_gate_ref) + b_gate_ref[...])
    u = _gelu(_dot_ref(hb, w_in_ref.at[:, 0:a_w]))
    mu = jnp.mean(v, axis=-1, keepdims=True)
    vc = v - mu
    var = jnp.mean(vc * vc, axis=-1, keepdims=True)
    vn = (vc * lax.rsqrt(var + EPS)) * lng_ref[...] + lnb_ref[...]
    vnb = vn.astype(BF16)

    ext = jnp.concatenate([pool_carry_ref[...], hbp], axis=0)
    pool_carry_ref[...] = hbp[ts - POOL_HALO:, :]
    bgd = hbp.shape[1] // len(POOL_WINDOWS)
    t = j * ts + lax.broadcasted_iota(jnp.int32, (ts, bgd), 0)
    yb_cols = []
    for gi, w in enumerate(POOL_WINDOWS):
        e = ext[:, gi * bgd:(gi + 1) * bgd]
        acc, span = e, 1
        while span < w:
            acc = acc + pltpu.roll(acc, span, 0)
            span *= 2
        win = acc[POOL_HALO:, :]
        cur = e[POOL_HALO:, :]
        count = jnp.minimum(t + 1, w).astype(F32)
        pooled = win / count - cur
        mixed = _dot(pooled.astype(BF16), wpool_ref[gi]) + bpool_ref[gi:gi + 1, :]
        yb_cols.append(mixed)
    y_b = jnp.concatenate(yb_cols, axis=1) * pscale_ref[...]
    branch_b = gates[:, d:] * _dot_ref(y_b.astype(BF16), wpb_ref)

    gd = a_w // A_GROUPS
    p = lax.broadcasted_iota(jnp.int32, (GMLP_BLOCK, GMLP_BLOCK), 0)
    q = lax.broadcasted_iota(jnp.int32, (GMLP_BLOCK, GMLP_BLOCK), 1)
    allowed = (q // CHUNK) <= (p // CHUNK)
    w_masked = [jnp.where(allowed, wsp_ref[g], 0.0).astype(BF16) for g in range(A_GROUPS)]
    bias_map = bsp_ref[...]
    s_rows = []
    for n in range(ts // GMLP_BLOCK):
        r0 = n * GMLP_BLOCK
        cols = [_dot(w_masked[g], vnb[r0:r0 + GMLP_BLOCK, g * gd:(g + 1) * gd]) for g in range(A_GROUPS)]
        s_rows.append(jnp.concatenate(cols, axis=1) + bias_map)
    s = jnp.concatenate(s_rows, axis=0)
    y_a = u * s

    merged = gates[:, :d] * _dot_ref(y_a.astype(BF16), wpa_ref) + branch_b
    x1 = x + gt1 * _dot_ref(merged.astype(BF16), wout_ref)
    o_ref[0] = jnp.concatenate(
        [_interleave_rows(x1[r:r + CHANNEL_TILE, :]) for r in range(0, ts, CHANNEL_TILE)], axis=0)


def _resident(shape):
    nd = len(shape)
    return pl.BlockSpec(shape, lambda b, j: (0,) * nd, pipeline_mode=pl.Buffered(1))


def _pitch_pad(n_cols):
    return LANES if (n_cols // LANES) % SUBLANES == 0 else 0


def _pack_columns(ws):
    rows = ws[0].shape[0]
    pad = _pitch_pad(sum(w.shape[1] for w in ws))
    cols = list(ws) + ([jnp.zeros((rows, pad), ws[0].dtype)] if pad else [])
    return jnp.concatenate(cols, axis=1).astype(BF16)


def _row_blocking(n_rows, n_steps):
    for steps_per_block in range(1, n_steps + 1):
        if n_steps % steps_per_block == 0 and (n_rows * steps_per_block) % n_steps == 0:
            rows = n_rows * steps_per_block // n_steps
            if rows % BF16_TILE_ROWS == 0:
                return rows, steps_per_block
    raise ValueError(f"no bf16-aligned row blocking of {n_rows} rows over {n_steps} steps")


def _step_block(b, j, *, n_tiles, steps_per_block):
    return ((b * n_tiles + j) // steps_per_block, 0)


def _mixer_call(x, mod, g1, w_in, w_gate, b_gate, ln_g, ln_b, w_sp, bias_map, w_pool, b_pool,
                pool_scale, w_pa, w_pb, w_out, w_up, w_down):
    bsz, s_len, d = x.shape
    ts = MIXER_TILE
    n_tiles = s_len // ts
    assert s_len % ts == 0 and ts % GMLP_BLOCK == 0 and ts >= POOL_HALO and ts % CHANNEL_TILE == 0
    assert w_gate.shape[1] == 2 * d and w_pa.shape == w_pb.shape == w_out.shape == (d, d)
    row = lambda a: a.reshape(1, -1)
    w_packed = _pack_columns([w_in, w_gate, w_pa, w_pb, w_out])
    operands = [
        (x, pl.BlockSpec((1, ts, d), lambda b, j: (b, j, 0))),
        (mod, pl.BlockSpec((1, N_ADA, d), lambda b, j: (b, 0, 0))),
    ]
    for a in (row(g1), w_packed, row(b_gate), row(ln_g), row(ln_b), w_sp, bias_map,
              w_pool.astype(BF16), b_pool, row(pool_scale)):
        operands.append((a, _resident(a.shape)))
    out_specs = [pl.BlockSpec((1, ts, d), lambda b, j: (b, j, 0))]
    out_shapes = [jax.ShapeDtypeStruct(x.shape, F32)]
    for w in (w_up, w_down):
        rows, steps_per_block = _row_blocking(w.shape[0], bsz * n_tiles)
        index_map = functools.partial(_step_block, n_tiles=n_tiles, steps_per_block=steps_per_block)
        operands.append((w, pl.BlockSpec((rows, w.shape[1]), index_map)))
        padded = w.shape[1] + _pitch_pad(w.shape[1])
        out_specs.append(pl.BlockSpec((rows, padded), index_map))
        out_shapes.append(jax.ShapeDtypeStruct((w.shape[0], padded), BF16))
    return pl.pallas_call(
        functools.partial(_mixer_kernel, in_width=w_in.shape[1]),
        grid=(bsz, n_tiles),
        in_specs=[spec for _, spec in operands],
        out_specs=out_specs,
        out_shape=out_shapes,
        scratch_shapes=[pltpu.VMEM((POOL_HALO, w_pool.shape[0] * w_pool.shape[1]), F32)],
        compiler_params=pltpu.CompilerParams(
            dimension_semantics=("arbitrary", "arbitrary"), vmem_limit_bytes=VMEM_LIMIT_BYTES),
        name="mixer",
    )(*[a for a, _ in operands])


def _interleave_rows(x):
    n, d = x.shape
    return jnp.swapaxes(x.reshape(SUBLANES, n // SUBLANES, d), 0, 1).reshape(n, d)


def _deinterleave_rows(y):
    n, d = y.shape
    return jnp.swapaxes(y.reshape(n // SUBLANES, SUBLANES, d), 0, 1).reshape(n, d)


def _delay_rows(a, prev_row):
    n = a.shape[0]
    wrapped = pltpu.roll(a[n - SUBLANES:, :], 1, 0)
    first = jnp.where(lax.broadcasted_iota(jnp.int32, wrapped.shape, 0) == 0, prev_row, wrapped)
    return jnp.concatenate([first, a[:n - SUBLANES, :]], axis=0)


def _channel_kernel(x_ref, mod_ref, g2_ref, w_up_ref, cw_ref, cb_ref, w_down_ref, gf_ref,
                    o_ref, conv_carry_ref, *, final_norm):
    ts, d = x_ref.shape[1], x_ref.shape[2]
    d_ff = w_down_ref.shape[0]
    n_chunks = d_ff // FF_CHUNK
    j = pl.program_id(1)

    @pl.when(j == 0)
    def _():
        conv_carry_ref[...] = jnp.zeros_like(conv_carry_ref)

    x = x_ref[0]
    sh2 = mod_ref[0, 3:4, :]
    sc2 = mod_ref[0, 4:5, :]
    gt2 = mod_ref[0, 5:6, :]
    h2 = _rms_scale(x) * (g2_ref[...] * (1.0 + sc2)) + sh2
    h2b = h2.astype(BF16)

    def up_project(c):
        return [_dot(h2b, w_up_ref[:, pl.ds(c0, FF_CHUNK)]) for c0 in (c * FF_CHUNK, d_ff + c * FF_CHUNK)]

    def conv_cols(pre, c0):
        cols = pl.ds(c0, FF_CHUNK)
        taps = [pre]
        for m in range(1, CONV_WIDTH):
            r = (CONV_WIDTH - 1 - m) * SUBLANES + SUBLANES - 1
            taps.append(_delay_rows(taps[-1], conv_carry_ref[r:r + 1, cols]))
        conv_carry_ref[:, cols] = pre[ts - CONV_CARRY_ROWS:, :]
        out = cb_ref[:, cols]
        for k in range(CONV_WIDTH):
            out = out + taps[CONV_WIDTH - 1 - k] * cw_ref[k:k + 1, cols]
        return out

    ahead = [up_project(i) for i in range(min(UP_LOOKAHEAD, n_chunks))]
    acc = jnp.zeros((ts, d), F32)
    for c in range(n_chunks):
        cur = ahead.pop(0)
        if c + UP_LOOKAHEAD < n_chunks:
            ahead.append(up_project(c + UP_LOOKAHEAD))
        gate = conv_cols(cur[0], c * FF_CHUNK)
        val = conv_cols(cur[1], d_ff + c * FF_CHUNK)
        f = gate * jax.nn.sigmoid(gate) * val
        acc = acc + _dot_ref(f.astype(BF16), w_down_ref.at[c * FF_CHUNK:(c + 1) * FF_CHUNK, 0:d])
    x2 = x + gt2 * acc
    if final_norm:
        x2 = _rms_scale(x2) * gf_ref[...]
    o_ref[0] = _deinterleave_rows(x2)


def _channel_call(x, mod, g2, w_up, conv_w, conv_b, w_down, g_final, final_norm):
    bsz, s_len, d = x.shape
    ts = CHANNEL_TILE
    assert s_len % ts == 0 and ts >= CONV_WIDTH * SUBLANES and w_down.shape[0] % FF_CHUNK == 0
    row = lambda a: a.reshape(1, -1)
    operands = [
        (x, pl.BlockSpec((1, ts, d), lambda b, j: (b, j, 0))),
        (mod, pl.BlockSpec((1, N_ADA, d), lambda b, j: (b, 0, 0))),
    ]
    for a in (row(g2), w_up, conv_w, row(conv_b), w_down, row(g_final)):
        operands.append((a, _resident(a.shape)))
    return pl.pallas_call(
        functools.partial(_channel_kernel, final_norm=final_norm),
        grid=(bsz, s_len // ts),
        in_specs=[spec for _, spec in operands],
        out_specs=pl.BlockSpec((1, ts, d), lambda b, j: (b, j, 0)),
        out_shape=jax.ShapeDtypeStruct(x.shape, F32),
        scratch_shapes=[pltpu.VMEM((CONV_CARRY_ROWS, conv_w.shape[1]), F32)],
        compiler_params=pltpu.CompilerParams(
            dimension_semantics=("arbitrary", "arbitrary"), vmem_limit_bytes=VMEM_LIMIT_BYTES),
        name="channel",
    )(*[a for a, _ in operands])


def kernel(x, c, w_ada, b_ada, g_norm1, w_in, ln_v_g, ln_v_b, w_spatial, b_spatial, w_pool, b_pool,
           pool_scale, w_proj_a, w_proj_b, w_gate, b_gate, w_out, g_norm2, w_up, conv_w, conv_b,
           w_down, g_final):
    depth = w_ada.shape[0]
    bsz, s_len, d = x.shape
    assert w_ada.shape[2] % ADA_TILE == 0
    gd = w_in.shape[2] // 3 // A_GROUPS
    for l in range(depth):
        mod = _ada_call(c, w_ada[l], b_ada[l]).reshape(bsz, N_ADA, d)
        bias_map = jnp.repeat(b_spatial[l].T, gd, axis=1)
        x, w_up_bf, w_down_bf = _mixer_call(
            x, mod, g_norm1[l], w_in[l], w_gate[l], b_gate[l], ln_v_g[l], ln_v_b[l], w_spatial[l],
            bias_map, w_pool[l], b_pool[l], pool_scale[l], w_proj_a[l], w_proj_b[l], w_out[l],
            w_up[l], w_down[l])
        x = _channel_call(
            x, mod, g_norm2[l], w_up_bf, conv_w[l], conv_b[l], w_down_bf, g_final,
            final_norm=(l == depth - 1))
    return x
```

```python
import functools
import math

import jax
import jax.numpy as jnp
from jax import lax
from jax.experimental import pallas as pl
from jax.experimental.pallas import tpu as pltpu

EPS = 1e-6
CHUNK = 64
GMLP_BLOCK = 128
A_GROUPS = 8
POOL_WINDOWS = (2, 4, 8, 16)
CONV_WIDTH = 3
N_ADA = 6

SUBLANES = 8
LANES = 128
BF16_TILE_ROWS = 16
POOL_HALO = 16
CONV_CARRY_ROWS = (CONV_WIDTH - 1) * SUBLANES
MXU_WIDTH = 256
MIXER_TILE = 512
CHANNEL_TILE = 256
CHANNEL_TILES_PER_STEP = 2
FF_CHUNK = 256
UP_LOOKAHEAD = 3
ADA_TILE = 2048
VMEM_LIMIT_BYTES = 56 * 1024 * 1024

BF16 = jnp.bfloat16
F32 = jnp.float32


def _dot(a, b):
    return jnp.dot(a, b, preferred_element_type=F32)


def _dot_ref(a, w_ref):
    n = w_ref.shape[-1]
    assert n % MXU_WIDTH == 0
    return jnp.concatenate(
        [_dot(a, w_ref[:, c0:c0 + MXU_WIDTH]) for c0 in range(0, n, MXU_WIDTH)], axis=1)


def _gelu(x):
    return 0.5 * x * (1.0 + lax.erf(x * math.sqrt(0.5)))


def _rms_scale(x):
    ms = jnp.mean(x * x, axis=-1, keepdims=True)
    return x * lax.rsqrt(ms + EPS)


def _ada_kernel(c_ref, w_ref, b_ref, o_ref):
    c = c_ref[...]
    s = c * jax.nn.sigmoid(c)
    o_ref[...] = _dot(s.astype(BF16), w_ref[...].astype(BF16)) + b_ref[...]


def _ada_call(c, w_ada, b_ada):
    bsz, d = c.shape
    n = w_ada.shape[1]
    return pl.pallas_call(
        _ada_kernel,
        grid=(n // ADA_TILE,),
        in_specs=[
            pl.BlockSpec((bsz, d), lambda j: (0, 0)),
            pl.BlockSpec((d, ADA_TILE), lambda j: (0, j)),
            pl.BlockSpec((1, ADA_TILE), lambda j: (0, j)),
        ],
        out_specs=pl.BlockSpec((bsz, ADA_TILE), lambda j: (0, j)),
        out_shape=jax.ShapeDtypeStruct((bsz, n), F32),
        compiler_params=pltpu.CompilerParams(dimension_semantics=("arbitrary",)),
        name="ada_mod",
    )(c, w_ada, b_ada.reshape(1, n))


def _cast_block(src_ref, dst_ref):
    n = src_ref.shape[1]
    dst_ref[:, 0:n] = src_ref[...].astype(BF16)
    if dst_ref.shape[1] > n:
        dst_ref[:, n:] = jnp.zeros((dst_ref.shape[0], dst_ref.shape[1] - n), BF16)


def _mixer_kernel(x_ref, mod_ref, g1_ref, w_ref, b_gate_ref, lng_ref, lnb_ref,
                  wsp_ref, bsp_ref, wpool_ref, bpool_ref, pscale_ref, w_up_ref, w_down_ref,
                  o_ref, w_up_bf_ref, w_down_bf_ref, pool_carry_ref, *, in_width):
    ts, d = x_ref.shape[1], x_ref.shape[2]
    j = pl.program_id(1)
    _cast_block(w_up_ref, w_up_bf_ref)
    _cast_block(w_down_ref, w_down_bf_ref)
    w_in_ref = w_ref.at[:, 0:in_width]
    w_gate_ref = w_ref.at[:, in_width:in_width + 2 * d]
    wpa_ref = w_ref.at[:, in_width + 2 * d:in_width + 3 * d]
    wpb_ref = w_ref.at[:, in_width + 3 * d:in_width + 4 * d]
    wout_ref = w_ref.at[:, in_width + 4 * d:in_width + 5 * d]

    @pl.when(j == 0)
    def _():
        pool_carry_ref[...] = jnp.zeros_like(pool_carry_ref)

    x = x_ref[0]
    sh1 = mod_ref[0, 0:1, :]
    sc1 = mod_ref[0, 1:2, :]
    gt1 = mod_ref[0, 2:3, :]
    h = _rms_scale(x) * (g1_ref[...] * (1.0 + sc1)) + sh1
    hb = h.astype(BF16)

    a_w = d
    v = _gelu(_dot_ref(hb, w_in_ref.at[:, a_w:2 * a_w]))
    hbp = _dot_ref(hb, w_in_ref.at[:, 2 * a_w:w_in_ref.shape[1]])
    gates = jax.nn.sigmoid(_dot_ref(hb, w_gate_ref) + b_gate_ref[...])
    u = _gelu(_dot_ref(hb, w_in_ref.at[:, 0:a_w]))
    mu = jnp.mean(v, axis=-1, keepdims=True)
    vc = v - mu
    var = jnp.mean(vc * vc, axis=-1, keepdims=True)
    vn = (vc * lax.rsqrt(var + EPS)) * lng_ref[...] + lnb_ref[...]
    vnb = vn.astype(BF16)

    ext = jnp.concatenate([pool_carry_ref[...], hbp], axis=0)
    pool_carry_ref[...] = hbp[ts - POOL_HALO:, :]
    bgd = hbp.shape[1] // len(POOL_WINDOWS)
    t = j * ts + lax.broadcasted_iota(jnp.int32, (ts, bgd), 0)
    yb_cols = []
    for gi, w in enumerate(POOL_WINDOWS):
        e = ext[:, gi * bgd:(gi + 1) * bgd]
        acc, span = e, 1
        while span < w:
            acc = acc + pltpu.roll(acc, span, 0)
            span *= 2
        win = acc[POOL_HALO:, :]
        cur = e[POOL_HALO:, :]
        count = jnp.minimum(t + 1, w).astype(F32)
        pooled = win / count - cur
        mixed = _dot(pooled.astype(BF16), wpool_ref[gi]) + bpool_ref[gi:gi + 1, :]
        yb_cols.append(mixed)
    y_b = jnp.concatenate(yb_cols, axis=1) * pscale_ref[...]
    branch_b = gates[:, d:] * _dot_ref(y_b.astype(BF16), wpb_ref)

    gd = a_w // A_GROUPS
    p = lax.broadcasted_iota(jnp.int32, (GMLP_BLOCK, GMLP_BLOCK), 0)
    q = lax.broadcasted_iota(jnp.int32, (GMLP_BLOCK, GMLP_BLOCK), 1)
    allowed = (q // CHUNK) <= (p // CHUNK)
    w_masked = [jnp.where(allowed, wsp_ref[g], 0.0).astype(BF16) for g in range(A_GROUPS)]
    bias_map = bsp_ref[...]
    s_rows = []
    for n in range(ts // GMLP_BLOCK):
        r0 = n * GMLP_BLOCK
        cols = [_dot(w_masked[g], vnb[r0:r0 + GMLP_BLOCK, g * gd:(g + 1) * gd]) for g in range(A_GROUPS)]
        s_rows.append(jnp.concatenate(cols, axis=1) + bias_map)
    s = jnp.concatenate(s_rows, axis=0)
    y_a = u * s

    merged = gates[:, :d] * _dot_ref(y_a.astype(BF16), wpa_ref) + branch_b
    x1 = x + gt1 * _dot_ref(merged.astype(BF16), wout_ref)
    o_ref[0] = jnp.concatenate(
        [_interleave_rows(x1[r:r + CHANNEL_TILE, :]) for r in range(0, ts, CHANNEL_TILE)], axis=0)


def _resident(shape):
    nd = len(shape)
    return pl.BlockSpec(shape, lambda b, j: (0,) * nd, pipeline_mode=pl.Buffered(1))


def _pitch_pad(n_cols):
    return LANES if (n_cols // LANES) % SUBLANES == 0 else 0


def _pack_columns(ws):
    rows = ws[0].shape[0]
    pad = _pitch_pad(sum(w.shape[1] for w in ws))
    cols = list(ws) + ([jnp.zeros((rows, pad), ws[0].dtype)] if pad else [])
    return jnp.concatenate(cols, axis=1).astype(BF16)


def _row_blocking(n_rows, n_steps):
    for steps_per_block in range(1, n_steps + 1):
        if n_steps % steps_per_block == 0 and (n_rows * steps_per_block) % n_steps == 0:
            rows = n_rows * steps_per_block // n_steps
            if rows % BF16_TILE_ROWS == 0:
                return rows, steps_per_block
    raise ValueError(f"no bf16-aligned row blocking of {n_rows} rows over {n_steps} steps")


def _step_block(b, j, *, n_tiles, steps_per_block):
    return ((b * n_tiles + j) // steps_per_block, 0)


def _mixer_call(x, mod, g1, w_in, w_gate, b_gate, ln_g, ln_b, w_sp, bias_map, w_pool, b_pool,
                pool_scale, w_pa, w_pb, w_out, w_up, w_down):
    bsz, s_len, d = x.shape
    ts = MIXER_TILE
    n_tiles = s_len // ts
    assert s_len % ts == 0 and ts % GMLP_BLOCK == 0 and ts >= POOL_HALO and ts % CHANNEL_TILE == 0
    assert w_gate.shape[1] == 2 * d and w_pa.shape == w_pb.shape == w_out.shape == (d, d)
    row = lambda a: a.reshape(1, -1)
    w_packed = _pack_columns([w_in, w_gate, w_pa, w_pb, w_out])
    operands = [
        (x, pl.BlockSpec((1, ts, d), lambda b, j: (b, j, 0))),
        (mod, pl.BlockSpec((1, N_ADA, d), lambda b, j: (b, 0, 0))),
    ]
    for a in (row(g1), w_packed, row(b_gate), row(ln_g), row(ln_b), w_sp, bias_map,
              w_pool.astype(BF16), b_pool, row(pool_scale)):
        operands.append((a, _resident(a.shape)))
    out_specs = [pl.BlockSpec((1, ts, d), lambda b, j: (b, j, 0))]
    out_shapes = [jax.ShapeDtypeStruct(x.shape, F32)]
    for w in (w_up, w_down):
        rows, steps_per_block = _row_blocking(w.shape[0], bsz * n_tiles)
        index_map = functools.partial(_step_block, n_tiles=n_tiles, steps_per_block=steps_per_block)
        operands.append((w, pl.BlockSpec((rows, w.shape[1]), index_map)))
        padded = w.shape[1] + _pitch_pad(w.shape[1])
        out_specs.append(pl.BlockSpec((rows, padded), index_map))
        out_shapes.append(jax.ShapeDtypeStruct((w.shape[0], padded), BF16))
    return pl.pallas_call(
        functools.partial(_mixer_kernel, in_width=w_in.shape[1]),
        grid=(bsz, n_tiles),
        in_specs=[spec for _, spec in operands],
        out_specs=out_specs,
        out_shape=out_shapes,
        scratch_shapes=[pltpu.VMEM((POOL_HALO, w_pool.shape[0] * w_pool.shape[1]), F32)],
        compiler_params=pltpu.CompilerParams(
            dimension_semantics=("arbitrary", "arbitrary"), vmem_limit_bytes=VMEM_LIMIT_BYTES),
        name="mixer",
    )(*[a for a, _ in operands])


def _interleave_rows(x):
    n, d = x.shape
    return jnp.swapaxes(x.reshape(SUBLANES, n // SUBLANES, d), 0, 1).reshape(n, d)


def _deinterleave_rows(y):
    n, d = y.shape
    return jnp.swapaxes(y.reshape(n // SUBLANES, SUBLANES, d), 0, 1).reshape(n, d)


def _delay_rows(a, prev_row):
    n = a.shape[0]
    wrapped = pltpu.roll(a[n - SUBLANES:, :], 1, 0)
    first = jnp.where(lax.broadcasted_iota(jnp.int32, wrapped.shape, 0) == 0, prev_row, wrapped)
    return jnp.concatenate([first, a[:n - SUBLANES, :]], axis=0)


def _channel_kernel(x_ref, mod_ref, g2_ref, w_up_ref, cw_ref, cb_ref, w_down_ref, gf_ref,
                    o_ref, conv_carry_ref, *, final_norm):
    ts, d = CHANNEL_TILE, x_ref.shape[2]
    d_ff = w_down_ref.shape[0]
    n_chunks = d_ff // FF_CHUNK

    @pl.when(pl.program_id(1) == 0)
    def _():
        conv_carry_ref[...] = jnp.zeros_like(conv_carry_ref)

    sh2 = mod_ref[0, 3:4, :]
    sc2 = mod_ref[0, 4:5, :]
    gt2 = mod_ref[0, 5:6, :]

    def conv_cols(pre, c0):
        cols = pl.ds(c0, FF_CHUNK)
        taps = [pre]
        for m in range(1, CONV_WIDTH):
            r = (CONV_WIDTH - 1 - m) * SUBLANES + SUBLANES - 1
            taps.append(_delay_rows(taps[-1], conv_carry_ref[r:r + 1, cols]))
        conv_carry_ref[:, cols] = pre[ts - CONV_CARRY_ROWS:, :]
        out = cb_ref[:, cols]
        for k in range(CONV_WIDTH):
            out = out + taps[CONV_WIDTH - 1 - k] * cw_ref[k:k + 1, cols]
        return out

    def token_tile(r0):
        x = x_ref[0, r0:r0 + ts, :]
        h2 = _rms_scale(x) * (g2_ref[...] * (1.0 + sc2)) + sh2
        h2b = h2.astype(BF16)

        def up_project(c):
            return [_dot(h2b, w_up_ref[:, pl.ds(c0, FF_CHUNK)]) for c0 in (c * FF_CHUNK, d_ff + c * FF_CHUNK)]

        ahead = [up_project(i) for i in range(min(UP_LOOKAHEAD, n_chunks))]
        acc = jnp.zeros((ts, d), F32)
        for c in range(n_chunks):
            cur = ahead.pop(0)
            if c + UP_LOOKAHEAD < n_chunks:
                ahead.append(up_project(c + UP_LOOKAHEAD))
            gate = conv_cols(cur[0], c * FF_CHUNK)
            val = conv_cols(cur[1], d_ff + c * FF_CHUNK)
            f = gate * jax.nn.sigmoid(gate) * val
            acc = acc + _dot_ref(f.astype(BF16), w_down_ref.at[c * FF_CHUNK:(c + 1) * FF_CHUNK, 0:d])
        x2 = x + gt2 * acc
        if final_norm:
            x2 = _rms_scale(x2) * gf_ref[...]
        o_ref[0, r0:r0 + ts, :] = _deinterleave_rows(x2)

    for r0 in range(0, x_ref.shape[1], ts):
        token_tile(r0)


def _channel_call(x, mod, g2, w_up, conv_w, conv_b, w_down, g_final, final_norm):
    bsz, s_len, d = x.shape
    ts = CHANNEL_TILE * CHANNEL_TILES_PER_STEP
    assert s_len % ts == 0 and CHANNEL_TILE >= CONV_WIDTH * SUBLANES and w_down.shape[0] % FF_CHUNK == 0
    row = lambda a: a.reshape(1, -1)
    operands = [
        (x, pl.BlockSpec((1, ts, d), lambda b, j: (b, j, 0))),
        (mod, pl.BlockSpec((1, N_ADA, d), lambda b, j: (b, 0, 0))),
    ]
    for a in (row(g2), w_up, conv_w, row(conv_b), w_down, row(g_final)):
        operands.append((a, _resident(a.shape)))
    return pl.pallas_call(
        functools.partial(_channel_kernel, final_norm=final_norm),
        grid=(bsz, s_len // ts),
        in_specs=[spec for _, spec in operands],
        out_specs=pl.BlockSpec((1, ts, d), lambda b, j: (b, j, 0)),
        out_shape=jax.ShapeDtypeStruct(x.shape, F32),
        scratch_shapes=[pltpu.VMEM((CONV_CARRY_ROWS, conv_w.shape[1]), F32)],
        compiler_params=pltpu.CompilerParams(
            dimension_semantics=("arbitrary", "arbitrary"), vmem_limit_bytes=VMEM_LIMIT_BYTES),
        name="channel",
    )(*[a for a, _ in operands])


def kernel(x, c, w_ada, b_ada, g_norm1, w_in, ln_v_g, ln_v_b, w_spatial, b_spatial, w_pool, b_pool,
           pool_scale, w_proj_a, w_proj_b, w_gate, b_gate, w_out, g_norm2, w_up, conv_w, conv_b,
           w_down, g_final):
    depth = w_ada.shape[0]
    bsz, s_len, d = x.shape
    assert w_ada.shape[2] % ADA_TILE == 0
    gd = w_in.shape[2] // 3 // A_GROUPS
    for l in range(depth):
        mod = _ada_call(c, w_ada[l], b_ada[l]).reshape(bsz, N_ADA, d)
        bias_map = jnp.repeat(b_spatial[l].T, gd, axis=1)
        x, w_up_bf, w_down_bf = _mixer_call(
            x, mod, g_norm1[l], w_in[l], w_gate[l], b_gate[l], ln_v_g[l], ln_v_b[l], w_spatial[l],
            bias_map, w_pool[l], b_pool[l], pool_scale[l], w_proj_a[l], w_proj_b[l], w_out[l],
            w_up[l], w_down[l])
        x = _channel_call(
            x, mod, g_norm2[l], w_up_bf, conv_w[l], conv_b[l], w_down_bf, g_final,
            final_norm=(l == depth - 1))
    return x
```

```python
import functools
import math

import jax
import jax.numpy as jnp
from jax import lax
from jax.experimental import pallas as pl
from jax.experimental.pallas import tpu as pltpu

EPS = 1e-6
CHUNK = 64
GMLP_BLOCK = 128
A_GROUPS = 8
POOL_WINDOWS = (2, 4, 8, 16)
CONV_WIDTH = 3
N_ADA = 6

SUBLANES = 8
LANES = 128
BF16_TILE_ROWS = 16
POOL_HALO = 16
CONV_CARRY_ROWS = (CONV_WIDTH - 1) * SUBLANES
MXU_WIDTH = 256
MIXER_TILE = 512
MIXER_TILES_PER_STEP = 2
CHANNEL_TILE = 256
CHANNEL_TILES_PER_STEP = 2
FF_CHUNK = 256
UP_LOOKAHEAD = 3
ADA_TILE = 2048
VMEM_LIMIT_BYTES = 56 * 1024 * 1024

BF16 = jnp.bfloat16
F32 = jnp.float32


def _dot(a, b):
    return jnp.dot(a, b, preferred_element_type=F32)


def _dot_ref(a, w_ref):
    n = w_ref.shape[-1]
    assert n % MXU_WIDTH == 0
    return jnp.concatenate(
        [_dot(a, w_ref[:, c0:c0 + MXU_WIDTH]) for c0 in range(0, n, MXU_WIDTH)], axis=1)


def _gelu(x):
    return 0.5 * x * (1.0 + lax.erf(x * math.sqrt(0.5)))


def _rms_scale(x):
    ms = jnp.mean(x * x, axis=-1, keepdims=True)
    return x * lax.rsqrt(ms + EPS)


def _ada_kernel(c_ref, w_ref, b_ref, o_ref):
    c = c_ref[...]
    s = c * jax.nn.sigmoid(c)
    o_ref[...] = _dot(s.astype(BF16), w_ref[...].astype(BF16)) + b_ref[...]


def _ada_call(c, w_ada, b_ada):
    bsz, d = c.shape
    n = w_ada.shape[1]
    return pl.pallas_call(
        _ada_kernel,
        grid=(n // ADA_TILE,),
        in_specs=[
            pl.BlockSpec((bsz, d), lambda j: (0, 0)),
            pl.BlockSpec((d, ADA_TILE), lambda j: (0, j)),
            pl.BlockSpec((1, ADA_TILE), lambda j: (0, j)),
        ],
        out_specs=pl.BlockSpec((bsz, ADA_TILE), lambda j: (0, j)),
        out_shape=jax.ShapeDtypeStruct((bsz, n), F32),
        compiler_params=pltpu.CompilerParams(dimension_semantics=("arbitrary",)),
        name="ada_mod",
    )(c, w_ada, b_ada.reshape(1, n))


def _cast_block(src_ref, dst_ref):
    n = src_ref.shape[1]
    dst_ref[:, 0:n] = src_ref[...].astype(BF16)
    if dst_ref.shape[1] > n:
        dst_ref[:, n:] = jnp.zeros((dst_ref.shape[0], dst_ref.shape[1] - n), BF16)


def _mixer_kernel(x_ref, mod_ref, g1_ref, w_ref, b_gate_ref, lng_ref, lnb_ref,
                  wsp_ref, bsp_ref, wpool_ref, bpool_ref, pscale_ref, w_up_ref, w_down_ref,
                  o_ref, w_up_bf_ref, w_down_bf_ref, pool_carry_ref, *, in_width):
    ts, d = MIXER_TILE, x_ref.shape[2]
    j = pl.program_id(1)
    _cast_block(w_up_ref, w_up_bf_ref)
    _cast_block(w_down_ref, w_down_bf_ref)
    w_in_ref = w_ref.at[:, 0:in_width]
    w_gate_ref = w_ref.at[:, in_width:in_width + 2 * d]
    wpa_ref = w_ref.at[:, in_width + 2 * d:in_width + 3 * d]
    wpb_ref = w_ref.at[:, in_width + 3 * d:in_width + 4 * d]
    wout_ref = w_ref.at[:, in_width + 4 * d:in_width + 5 * d]

    @pl.when(j == 0)
    def _():
        pool_carry_ref[...] = jnp.zeros_like(pool_carry_ref)

    sh1 = mod_ref[0, 0:1, :]
    sc1 = mod_ref[0, 1:2, :]
    gt1 = mod_ref[0, 2:3, :]
    a_w = d
    gd = a_w // A_GROUPS
    p = lax.broadcasted_iota(jnp.int32, (GMLP_BLOCK, GMLP_BLOCK), 0)
    q = lax.broadcasted_iota(jnp.int32, (GMLP_BLOCK, GMLP_BLOCK), 1)
    allowed = (q // CHUNK) <= (p // CHUNK)
    w_masked = [jnp.where(allowed, wsp_ref[g], 0.0).astype(BF16) for g in range(A_GROUPS)]

    def token_tile(row0):
        x = x_ref[0, row0:row0 + ts, :]
        h = _rms_scale(x) * (g1_ref[...] * (1.0 + sc1)) + sh1
        hb = h.astype(BF16)

        v = _gelu(_dot_ref(hb, w_in_ref.at[:, a_w:2 * a_w]))
        hbp = _dot_ref(hb, w_in_ref.at[:, 2 * a_w:w_in_ref.shape[1]])
        gates = jax.nn.sigmoid(_dot_ref(hb, w_gate_ref) + b_gate_ref[...])
        u = _gelu(_dot_ref(hb, w_in_ref.at[:, 0:a_w]))
        mu = jnp.mean(v, axis=-1, keepdims=True)
        vc = v - mu
        var = jnp.mean(vc * vc, axis=-1, keepdims=True)
        vn = (vc * lax.rsqrt(var + EPS)) * lng_ref[...] + lnb_ref[...]
        vnb = vn.astype(BF16)

        ext = jnp.concatenate([pool_carry_ref[...], hbp], axis=0)
        pool_carry_ref[...] = hbp[ts - POOL_HALO:, :]
        bgd = hbp.shape[1] // len(POOL_WINDOWS)
        t = j * x_ref.shape[1] + row0 + lax.broadcasted_iota(jnp.int32, (ts, bgd), 0)
        yb_cols = []
        for gi, w in enumerate(POOL_WINDOWS):
            e = ext[:, gi * bgd:(gi + 1) * bgd]
            acc, span = e, 1
            while span < w:
                acc = acc + pltpu.roll(acc, span, 0)
                span *= 2
            win = acc[POOL_HALO:, :]
            cur = e[POOL_HALO:, :]
            count = jnp.minimum(t + 1, w).astype(F32)
            pooled = win / count - cur
            mixed = _dot(pooled.astype(BF16), wpool_ref[gi]) + bpool_ref[gi:gi + 1, :]
            yb_cols.append(mixed)
        y_b = jnp.concatenate(yb_cols, axis=1) * pscale_ref[...]
        branch_b = gates[:, d:] * _dot_ref(y_b.astype(BF16), wpb_ref)

        bias_map = bsp_ref[...]
        s_rows = []
        for n in range(ts // GMLP_BLOCK):
            b0 = n * GMLP_BLOCK
            cols = [_dot(w_masked[g], vnb[b0:b0 + GMLP_BLOCK, g * gd:(g + 1) * gd]) for g in range(A_GROUPS)]
            s_rows.append(jnp.concatenate(cols, axis=1) + bias_map)
        s = jnp.concatenate(s_rows, axis=0)
        y_a = u * s

        merged = gates[:, :d] * _dot_ref(y_a.astype(BF16), wpa_ref) + branch_b
        x1 = x + gt1 * _dot_ref(merged.astype(BF16), wout_ref)
        o_ref[0, row0:row0 + ts, :] = jnp.concatenate(
            [_interleave_rows(x1[r:r + CHANNEL_TILE, :]) for r in range(0, ts, CHANNEL_TILE)], axis=0)

    for row0 in range(0, x_ref.shape[1], ts):
        token_tile(row0)


def _resident(shape):
    nd = len(shape)
    return pl.BlockSpec(shape, lambda b, j: (0,) * nd, pipeline_mode=pl.Buffered(1))


def _pitch_pad(n_cols):
    return LANES if (n_cols // LANES) % SUBLANES == 0 else 0


def _pack_columns(ws):
    rows = ws[0].shape[0]
    pad = _pitch_pad(sum(w.shape[1] for w in ws))
    cols = list(ws) + ([jnp.zeros((rows, pad), ws[0].dtype)] if pad else [])
    return jnp.concatenate(cols, axis=1).astype(BF16)


def _row_blocking(n_rows, n_steps):
    for steps_per_block in range(1, n_steps + 1):
        if n_steps % steps_per_block == 0 and (n_rows * steps_per_block) % n_steps == 0:
            rows = n_rows * steps_per_block // n_steps
            if rows % BF16_TILE_ROWS == 0:
                return rows, steps_per_block
    raise ValueError(f"no bf16-aligned row blocking of {n_rows} rows over {n_steps} steps")


def _step_block(b, j, *, n_tiles, steps_per_block):
    return ((b * n_tiles + j) // steps_per_block, 0)


def _mixer_call(x, mod, g1, w_in, w_gate, b_gate, ln_g, ln_b, w_sp, bias_map, w_pool, b_pool,
                pool_scale, w_pa, w_pb, w_out, w_up, w_down):
    bsz, s_len, d = x.shape
    ts = MIXER_TILE * MIXER_TILES_PER_STEP
    n_tiles = s_len // ts
    assert s_len % ts == 0 and MIXER_TILE % GMLP_BLOCK == 0 and MIXER_TILE >= POOL_HALO
    assert MIXER_TILE % CHANNEL_TILE == 0
    assert w_gate.shape[1] == 2 * d and w_pa.shape == w_pb.shape == w_out.shape == (d, d)
    row = lambda a: a.reshape(1, -1)
    w_packed = _pack_columns([w_in, w_gate, w_pa, w_pb, w_out])
    operands = [
        (x, pl.BlockSpec((1, ts, d), lambda b, j: (b, j, 0))),
        (mod, pl.BlockSpec((1, N_ADA, d), lambda b, j: (b, 0, 0))),
    ]
    for a in (row(g1), w_packed, row(b_gate), row(ln_g), row(ln_b), w_sp, bias_map,
              w_pool.astype(BF16), b_pool, row(pool_scale)):
        operands.append((a, _resident(a.shape)))
    out_specs = [pl.BlockSpec((1, ts, d), lambda b, j: (b, j, 0))]
    out_shapes = [jax.ShapeDtypeStruct(x.shape, F32)]
    for w in (w_up, w_down):
        rows, steps_per_block = _row_blocking(w.shape[0], bsz * n_tiles)
        index_map = functools.partial(_step_block, n_tiles=n_tiles, steps_per_block=steps_per_block)
        operands.append((w, pl.BlockSpec((rows, w.shape[1]), index_map)))
        padded = w.shape[1] + _pitch_pad(w.shape[1])
        out_specs.append(pl.BlockSpec((rows, padded), index_map))
        out_shapes.append(jax.ShapeDtypeStruct((w.shape[0], padded), BF16))
    return pl.pallas_call(
        functools.partial(_mixer_kernel, in_width=w_in.shape[1]),
        grid=(bsz, n_tiles),
        in_specs=[spec for _, spec in operands],
        out_specs=out_specs,
        out_shape=out_shapes,
        scratch_shapes=[pltpu.VMEM((POOL_HALO, w_pool.shape[0] * w_pool.shape[1]), F32)],
        compiler_params=pltpu.CompilerParams(
            dimension_semantics=("arbitrary", "arbitrary"), vmem_limit_bytes=VMEM_LIMIT_BYTES),
        name="mixer",
    )(*[a for a, _ in operands])


def _interleave_rows(x):
    n, d = x.shape
    return jnp.swapaxes(x.reshape(SUBLANES, n // SUBLANES, d), 0, 1).reshape(n, d)


def _deinterleave_rows(y):
    n, d = y.shape
    return jnp.swapaxes(y.reshape(n // SUBLANES, SUBLANES, d), 0, 1).reshape(n, d)


def _delay_rows(a, prev_row):
    n = a.shape[0]
    wrapped = pltpu.roll(a[n - SUBLANES:, :], 1, 0)
    first = jnp.where(lax.broadcasted_iota(jnp.int32, wrapped.shape, 0) == 0, prev_row, wrapped)
    return jnp.concatenate([first, a[:n - SUBLANES, :]], axis=0)


def _channel_kernel(x_ref, mod_ref, g2_ref, w_up_ref, cw_ref, cb_ref, w_down_ref, gf_ref,
                    o_ref, conv_carry_ref, *, final_norm):
    ts, d = CHANNEL_TILE, x_ref.shape[2]
    d_ff = w_down_ref.shape[0]
    n_chunks = d_ff // FF_CHUNK

    @pl.when(pl.program_id(1) == 0)
    def _():
        conv_carry_ref[...] = jnp.zeros_like(conv_carry_ref)

    sh2 = mod_ref[0, 3:4, :]
    sc2 = mod_ref[0, 4:5, :]
    gt2 = mod_ref[0, 5:6, :]

    def conv_cols(pre, c0):
        cols = pl.ds(c0, FF_CHUNK)
        taps = [pre]
        for m in range(1, CONV_WIDTH):
            r = (CONV_WIDTH - 1 - m) * SUBLANES + SUBLANES - 1
            taps.append(_delay_rows(taps[-1], conv_carry_ref[r:r + 1, cols]))
        conv_carry_ref[:, cols] = pre[ts - CONV_CARRY_ROWS:, :]
        out = cb_ref[:, cols]
        for k in range(CONV_WIDTH):
            out = out + taps[CONV_WIDTH - 1 - k] * cw_ref[k:k + 1, cols]
        return out

    def token_tile(r0):
        x = x_ref[0, r0:r0 + ts, :]
        h2 = _rms_scale(x) * (g2_ref[...] * (1.0 + sc2)) + sh2
        h2b = h2.astype(BF16)

        def up_project(c):
            return [_dot(h2b, w_up_ref[:, pl.ds(c0, FF_CHUNK)]) for c0 in (c * FF_CHUNK, d_ff + c * FF_CHUNK)]

        ahead = [up_project(i) for i in range(min(UP_LOOKAHEAD, n_chunks))]
        acc = jnp.zeros((ts, d), F32)
        for c in range(n_chunks):
            cur = ahead.pop(0)
            if c + UP_LOOKAHEAD < n_chunks:
                ahead.append(up_project(c + UP_LOOKAHEAD))
            gate = conv_cols(cur[0], c * FF_CHUNK)
            val = conv_cols(cur[1], d_ff + c * FF_CHUNK)
            f = gate * jax.nn.sigmoid(gate) * val
            acc = acc + _dot_ref(f.astype(BF16), w_down_ref.at[c * FF_CHUNK:(c + 1) * FF_CHUNK, 0:d])
        x2 = x + gt2 * acc
        if final_norm:
            x2 = _rms_scale(x2) * gf_ref[...]
        o_ref[0, r0:r0 + ts, :] = _deinterleave_rows(x2)

    for r0 in range(0, x_ref.shape[1], ts):
        token_tile(r0)


def _channel_call(x, mod, g2, w_up, conv_w, conv_b, w_down, g_final, final_norm):
    bsz, s_len, d = x.shape
    ts = CHANNEL_TILE * CHANNEL_TILES_PER_STEP
    assert s_len % ts == 0 and CHANNEL_TILE >= CONV_WIDTH * SUBLANES and w_down.shape[0] % FF_CHUNK == 0
    row = lambda a: a.reshape(1, -1)
    operands = [
        (x, pl.BlockSpec((1, ts, d), lambda b, j: (b, j, 0))),
        (mod, pl.BlockSpec((1, N_ADA, d), lambda b, j: (b, 0, 0))),
    ]
    for a in (row(g2), w_up, conv_w, row(conv_b), w_down, row(g_final)):
        operands.append((a, _resident(a.shape)))
    return pl.pallas_call(
        functools.partial(_channel_kernel, final_norm=final_norm),
        grid=(bsz, s_len // ts),
        in_specs=[spec for _, spec in operands],
        out_specs=pl.BlockSpec((1, ts, d), lambda b, j: (b, j, 0)),
        out_shape=jax.ShapeDtypeStruct(x.shape, F32),
        scratch_shapes=[pltpu.VMEM((CONV_CARRY_ROWS, conv_w.shape[1]), F32)],
        compiler_params=pltpu.CompilerParams(
            dimension_semantics=("arbitrary", "arbitrary"), vmem_limit_bytes=VMEM_LIMIT_BYTES),
        name="channel",
    )(*[a for a, _ in operands])


def kernel(x, c, w_ada, b_ada, g_norm1, w_in, ln_v_g, ln_v_b, w_spatial, b_spatial, w_pool, b_pool,
           pool_scale, w_proj_a, w_proj_b, w_gate, b_gate, w_out, g_norm2, w_up, conv_w, conv_b,
           w_down, g_final):
    depth = w_ada.shape[0]
    bsz, s_len, d = x.shape
    assert w_ada.shape[2] % ADA_TILE == 0
    gd = w_in.shape[2] // 3 // A_GROUPS
    for l in range(depth):
        mod = _ada_call(c, w_ada[l], b_ada[l]).reshape(bsz, N_ADA, d)
        bias_map = jnp.repeat(b_spatial[l].T, gd, axis=1)
        x, w_up_bf, w_down_bf = _mixer_call(
            x, mod, g_norm1[l], w_in[l], w_gate[l], b_gate[l], ln_v_g[l], ln_v_b[l], w_spatial[l],
            bias_map, w_pool[l], b_pool[l], pool_scale[l], w_proj_a[l], w_proj_b[l], w_out[l],
            w_up[l], w_down[l])
        x = _channel_call(
            x, mod, g_norm2[l], w_up_bf, conv_w[l], conv_b[l], w_down_bf, g_final,
            final_norm=(l == depth - 1))
    return x
```

```python
import functools
import math

import jax
import jax.numpy as jnp
from jax import lax
from jax.experimental import pallas as pl
from jax.experimental.pallas import tpu as pltpu

EPS = 1e-6
CHUNK = 64
GMLP_BLOCK = 128
A_GROUPS = 8
POOL_WINDOWS = (2, 4, 8, 16)
CONV_WIDTH = 3
N_ADA = 6

SUBLANES = 8
LANES = 128
BF16_TILE_ROWS = 16
POOL_HALO = 16
CONV_CARRY_ROWS = (CONV_WIDTH - 1) * SUBLANES
MXU_WIDTH = 256
MIXER_TILE = 512
MIXER_TILES_PER_STEP = 2
CHANNEL_TILE = 256
CHANNEL_TILES_PER_STEP = 2
FF_CHUNK = 256
UP_LOOKAHEAD = 5
ADA_TILE = 2048
VMEM_LIMIT_BYTES = 56 * 1024 * 1024

BF16 = jnp.bfloat16
F32 = jnp.float32


def _dot(a, b):
    return jnp.dot(a, b, preferred_element_type=F32)


def _dot_ref(a, w_ref):
    n = w_ref.shape[-1]
    assert n % MXU_WIDTH == 0
    return jnp.concatenate(
        [_dot(a, w_ref[:, c0:c0 + MXU_WIDTH]) for c0 in range(0, n, MXU_WIDTH)], axis=1)


def _gelu(x):
    return 0.5 * x * (1.0 + lax.erf(x * math.sqrt(0.5)))


def _rms_scale(x):
    ms = jnp.mean(x * x, axis=-1, keepdims=True)
    return x * lax.rsqrt(ms + EPS)


def _ada_kernel(c_ref, w_ref, b_ref, o_ref):
    c = c_ref[...]
    s = c * jax.nn.sigmoid(c)
    o_ref[...] = _dot(s.astype(BF16), w_ref[...].astype(BF16)) + b_ref[...]


def _ada_call(c, w_ada, b_ada):
    bsz, d = c.shape
    n = w_ada.shape[1]
    return pl.pallas_call(
        _ada_kernel,
        grid=(n // ADA_TILE,),
        in_specs=[
            pl.BlockSpec((bsz, d), lambda j: (0, 0)),
            pl.BlockSpec((d, ADA_TILE), lambda j: (0, j)),
            pl.BlockSpec((1, ADA_TILE), lambda j: (0, j)),
        ],
        out_specs=pl.BlockSpec((bsz, ADA_TILE), lambda j: (0, j)),
        out_shape=jax.ShapeDtypeStruct((bsz, n), F32),
        compiler_params=pltpu.CompilerParams(dimension_semantics=("arbitrary",)),
        name="ada_mod",
    )(c, w_ada, b_ada.reshape(1, n))


def _cast_block(src_ref, dst_ref):
    n = src_ref.shape[1]
    dst_ref[:, 0:n] = src_ref[...].astype(BF16)
    if dst_ref.shape[1] > n:
        dst_ref[:, n:] = jnp.zeros((dst_ref.shape[0], dst_ref.shape[1] - n), BF16)


def _mixer_kernel(x_ref, mod_ref, g1_ref, w_ref, b_gate_ref, lng_ref, lnb_ref,
                  wsp_ref, bsp_ref, wpool_ref, bpool_ref, pscale_ref, w_up_ref, w_down_ref,
                  o_ref, w_up_bf_ref, w_down_bf_ref, pool_carry_ref, *, in_width):
    ts, d = MIXER_TILE, x_ref.shape[2]
    j = pl.program_id(1)
    _cast_block(w_up_ref, w_up_bf_ref)
    _cast_block(w_down_ref, w_down_bf_ref)
    w_in_ref = w_ref.at[:, 0:in_width]
    w_gate_ref = w_ref.at[:, in_width:in_width + 2 * d]
    wpa_ref = w_ref.at[:, in_width + 2 * d:in_width + 3 * d]
    wpb_ref = w_ref.at[:, in_width + 3 * d:in_width + 4 * d]
    wout_ref = w_ref.at[:, in_width + 4 * d:in_width + 5 * d]

    @pl.when(j == 0)
    def _():
        pool_carry_ref[...] = jnp.zeros_like(pool_carry_ref)

    sh1 = mod_ref[0, 0:1, :]
    sc1 = mod_ref[0, 1:2, :]
    gt1 = mod_ref[0, 2:3, :]
    a_w = d
    gd = a_w // A_GROUPS
    p = lax.broadcasted_iota(jnp.int32, (GMLP_BLOCK, GMLP_BLOCK), 0)
    q = lax.broadcasted_iota(jnp.int32, (GMLP_BLOCK, GMLP_BLOCK), 1)
    allowed = (q // CHUNK) <= (p // CHUNK)
    w_masked = [jnp.where(allowed, wsp_ref[g], 0.0).astype(BF16) for g in range(A_GROUPS)]

    def token_tile(row0):
        x = x_ref[0, row0:row0 + ts, :]
        h = _rms_scale(x) * (g1_ref[...] * (1.0 + sc1)) + sh1
        hb = h.astype(BF16)

        v = _gelu(_dot_ref(hb, w_in_ref.at[:, a_w:2 * a_w]))
        hbp = _dot_ref(hb, w_in_ref.at[:, 2 * a_w:w_in_ref.shape[1]])
        gates = jax.nn.sigmoid(_dot_ref(hb, w_gate_ref) + b_gate_ref[...])
        u = _gelu(_dot_ref(hb, w_in_ref.at[:, 0:a_w]))
        mu = jnp.mean(v, axis=-1, keepdims=True)
        vc = v - mu
        var = jnp.mean(vc * vc, axis=-1, keepdims=True)
        vn = (vc * lax.rsqrt(var + EPS)) * lng_ref[...] + lnb_ref[...]
        vnb = vn.astype(BF16)

        ext = jnp.concatenate([pool_carry_ref[...], hbp], axis=0)
        pool_carry_ref[...] = hbp[ts - POOL_HALO:, :]
        bgd = hbp.shape[1] // len(POOL_WINDOWS)
        t = j * x_ref.shape[1] + row0 + lax.broadcasted_iota(jnp.int32, (ts, bgd), 0)
        yb_cols = []
        for gi, w in enumerate(POOL_WINDOWS):
            e = ext[:, gi * bgd:(gi + 1) * bgd]
            acc, span = e, 1
            while span < w:
                acc = acc + pltpu.roll(acc, span, 0)
                span *= 2
            win = acc[POOL_HALO:, :]
            cur = e[POOL_HALO:, :]
            count = jnp.minimum(t + 1, w).astype(F32)
            pooled = win / count - cur
            mixed = _dot(pooled.astype(BF16), wpool_ref[gi]) + bpool_ref[gi:gi + 1, :]
            yb_cols.append(mixed)
        y_b = jnp.concatenate(yb_cols, axis=1) * pscale_ref[...]
        branch_b = gates[:, d:] * _dot_ref(y_b.astype(BF16), wpb_ref)

        bias_map = bsp_ref[...]
        s_rows = []
        for n in range(ts // GMLP_BLOCK):
            b0 = n * GMLP_BLOCK
            cols = [_dot(w_masked[g], vnb[b0:b0 + GMLP_BLOCK, g * gd:(g + 1) * gd]) for g in range(A_GROUPS)]
            s_rows.append(jnp.concatenate(cols, axis=1) + bias_map)
        s = jnp.concatenate(s_rows, axis=0)
        y_a = u * s

        merged = gates[:, :d] * _dot_ref(y_a.astype(BF16), wpa_ref) + branch_b
        x1 = x + gt1 * _dot_ref(merged.astype(BF16), wout_ref)
        o_ref[0, row0:row0 + ts, :] = jnp.concatenate(
            [_interleave_rows(x1[r:r + CHANNEL_TILE, :]) for r in range(0, ts, CHANNEL_TILE)], axis=0)

    for row0 in range(0, x_ref.shape[1], ts):
        token_tile(row0)


def _resident(shape):
    nd = len(shape)
    return pl.BlockSpec(shape, lambda b, j: (0,) * nd, pipeline_mode=pl.Buffered(1))


def _pitch_pad(n_cols):
    return LANES if (n_cols // LANES) % SUBLANES == 0 else 0


def _pack_columns(ws):
    rows = ws[0].shape[0]
    pad = _pitch_pad(sum(w.shape[1] for w in ws))
    cols = list(ws) + ([jnp.zeros((rows, pad), ws[0].dtype)] if pad else [])
    return jnp.concatenate(cols, axis=1).astype(BF16)


def _row_blocking(n_rows, n_steps):
    for steps_per_block in range(1, n_steps + 1):
        if n_steps % steps_per_block == 0 and (n_rows * steps_per_block) % n_steps == 0:
            rows = n_rows * steps_per_block // n_steps
            if rows % BF16_TILE_ROWS == 0:
                return rows, steps_per_block
    raise ValueError(f"no bf16-aligned row blocking of {n_rows} rows over {n_steps} steps")


def _step_block(b, j, *, n_tiles, steps_per_block):
    return ((b * n_tiles + j) // steps_per_block, 0)


def _mixer_call(x, mod, g1, w_in, w_gate, b_gate, ln_g, ln_b, w_sp, bias_map, w_pool, b_pool,
                pool_scale, w_pa, w_pb, w_out, w_up, w_down):
    bsz, s_len, d = x.shape
    ts = MIXER_TILE * MIXER_TILES_PER_STEP
    n_tiles = s_len // ts
    assert s_len % ts == 0 and MIXER_TILE % GMLP_BLOCK == 0 and MIXER_TILE >= POOL_HALO
    assert MIXER_TILE % CHANNEL_TILE == 0
    assert w_gate.shape[1] == 2 * d and w_pa.shape == w_pb.shape == w_out.shape == (d, d)
    row = lambda a: a.reshape(1, -1)
    w_packed = _pack_columns([w_in, w_gate, w_pa, w_pb, w_out])
    operands = [
        (x, pl.BlockSpec((1, ts, d), lambda b, j: (b, j, 0))),
        (mod, pl.BlockSpec((1, N_ADA, d), lambda b, j: (b, 0, 0))),
    ]
    for a in (row(g1), w_packed, row(b_gate), row(ln_g), row(ln_b), w_sp, bias_map,
              w_pool.astype(BF16), b_pool, row(pool_scale)):
        operands.append((a, _resident(a.shape)))
    out_specs = [pl.BlockSpec((1, ts, d), lambda b, j: (b, j, 0))]
    out_shapes = [jax.ShapeDtypeStruct(x.shape, F32)]
    for w in (w_up, w_down):
        rows, steps_per_block = _row_blocking(w.shape[0], bsz * n_tiles)
        index_map = functools.partial(_step_block, n_tiles=n_tiles, steps_per_block=steps_per_block)
        operands.append((w, pl.BlockSpec((rows, w.shape[1]), index_map)))
        padded = w.shape[1] + _pitch_pad(w.shape[1])
        out_specs.append(pl.BlockSpec((rows, padded), index_map))
        out_shapes.append(jax.ShapeDtypeStruct((w.shape[0], padded), BF16))
    return pl.pallas_call(
        functools.partial(_mixer_kernel, in_width=w_in.shape[1]),
        grid=(bsz, n_tiles),
        in_specs=[spec for _, spec in operands],
        out_specs=out_specs,
        out_shape=out_shapes,
        scratch_shapes=[pltpu.VMEM((POOL_HALO, w_pool.shape[0] * w_pool.shape[1]), F32)],
        compiler_params=pltpu.CompilerParams(
            dimension_semantics=("arbitrary", "arbitrary"), vmem_limit_bytes=VMEM_LIMIT_BYTES),
        name="mixer",
    )(*[a for a, _ in operands])


def _interleave_rows(x):
    n, d = x.shape
    return jnp.swapaxes(x.reshape(SUBLANES, n // SUBLANES, d), 0, 1).reshape(n, d)


def _deinterleave_rows(y):
    n, d = y.shape
    return jnp.swapaxes(y.reshape(n // SUBLANES, SUBLANES, d), 0, 1).reshape(n, d)


def _delay_rows(a, prev_row):
    n = a.shape[0]
    wrapped = pltpu.roll(a[n - SUBLANES:, :], 1, 0)
    first = jnp.where(lax.broadcasted_iota(jnp.int32, wrapped.shape, 0) == 0, prev_row, wrapped)
    return jnp.concatenate([first, a[:n - SUBLANES, :]], axis=0)


def _channel_kernel(x_ref, mod_ref, g2_ref, w_up_ref, cw_ref, cb_ref, w_down_ref, gf_ref,
                    o_ref, conv_carry_ref, *, final_norm):
    ts, d = CHANNEL_TILE, x_ref.shape[2]
    d_ff = w_down_ref.shape[0]
    n_chunks = d_ff // FF_CHUNK

    @pl.when(pl.program_id(1) == 0)
    def _():
        conv_carry_ref[...] = jnp.zeros_like(conv_carry_ref)

    sh2 = mod_ref[0, 3:4, :]
    sc2 = mod_ref[0, 4:5, :]
    gt2 = mod_ref[0, 5:6, :]

    def conv_cols(pre, c0):
        cols = pl.ds(c0, FF_CHUNK)
        taps = [pre]
        for m in range(1, CONV_WIDTH):
            r = (CONV_WIDTH - 1 - m) * SUBLANES + SUBLANES - 1
            taps.append(_delay_rows(taps[-1], conv_carry_ref[r:r + 1, cols]))
        conv_carry_ref[:, cols] = pre[ts - CONV_CARRY_ROWS:, :]
        out = cb_ref[:, cols]
        for k in range(CONV_WIDTH):
            out = out + taps[CONV_WIDTH - 1 - k] * cw_ref[k:k + 1, cols]
        return out

    def token_tile(r0):
        x = x_ref[0, r0:r0 + ts, :]
        h2 = _rms_scale(x) * (g2_ref[...] * (1.0 + sc2)) + sh2
        h2b = h2.astype(BF16)

        def up_project(c):
            return [_dot(h2b, w_up_ref[:, pl.ds(c0, FF_CHUNK)]) for c0 in (c * FF_CHUNK, d_ff + c * FF_CHUNK)]

        ahead = [up_project(i) for i in range(min(UP_LOOKAHEAD, n_chunks))]
        acc = jnp.zeros((ts, d), F32)
        for c in range(n_chunks):
            cur = ahead.pop(0)
            if c + UP_LOOKAHEAD < n_chunks:
                ahead.append(up_project(c + UP_LOOKAHEAD))
            gate = conv_cols(cur[0], c * FF_CHUNK)
            val = conv_cols(cur[1], d_ff + c * FF_CHUNK)
            f = gate * jax.nn.sigmoid(gate) * val
            acc = acc + _dot_ref(f.astype(BF16), w_down_ref.at[c * FF_CHUNK:(c + 1) * FF_CHUNK, 0:d])
        x2 = x + gt2 * acc
        if final_norm:
            x2 = _rms_scale(x2) * gf_ref[...]
        o_ref[0, r0:r0 + ts, :] = _deinterleave_rows(x2)

    for r0 in range(0, x_ref.shape[1], ts):
        token_tile(r0)


def _channel_call(x, mod, g2, w_up, conv_w, conv_b, w_down, g_final, final_norm):
    bsz, s_len, d = x.shape
    ts = CHANNEL_TILE * CHANNEL_TILES_PER_STEP
    assert s_len % ts == 0 and CHANNEL_TILE >= CONV_WIDTH * SUBLANES and w_down.shape[0] % FF_CHUNK == 0
    row = lambda a: a.reshape(1, -1)
    operands = [
        (x, pl.BlockSpec((1, ts, d), lambda b, j: (b, j, 0))),
        (mod, pl.BlockSpec((1, N_ADA, d), lambda b, j: (b, 0, 0))),
    ]
    for a in (row(g2), w_up, conv_w, row(conv_b), w_down, row(g_final)):
        operands.append((a, _resident(a.shape)))
    return pl.pallas_call(
        functools.partial(_channel_kernel, final_norm=final_norm),
        grid=(bsz, s_len // ts),
        in_specs=[spec for _, spec in operands],
        out_specs=pl.BlockSpec((1, ts, d), lambda b, j: (b, j, 0)),
        out_shape=jax.ShapeDtypeStruct(x.shape, F32),
        scratch_shapes=[pltpu.VMEM((CONV_CARRY_ROWS, conv_w.shape[1]), F32)],
        compiler_params=pltpu.CompilerParams(
            dimension_semantics=("arbitrary", "arbitrary"), vmem_limit_bytes=VMEM_LIMIT_BYTES),
        name="channel",
    )(*[a for a, _ in operands])


def kernel(x, c, w_ada, b_ada, g_norm1, w_in, ln_v_g, ln_v_b, w_spatial, b_spatial, w_pool, b_pool,
           pool_scale, w_proj_a, w_proj_b, w_gate, b_gate, w_out, g_norm2, w_up, conv_w, conv_b,
           w_down, g_final):
    depth = w_ada.shape[0]
    bsz, s_len, d = x.shape
    assert w_ada.shape[2] % ADA_TILE == 0
    gd = w_in.shape[2] // 3 // A_GROUPS
    for l in range(depth):
        mod = _ada_call(c, w_ada[l], b_ada[l]).reshape(bsz, N_ADA, d)
        bias_map = jnp.repeat(b_spatial[l].T, gd, axis=1)
        x, w_up_bf, w_down_bf = _mixer_call(
            x, mod, g_norm1[l], w_in[l], w_gate[l], b_gate[l], ln_v_g[l], ln_v_b[l], w_spatial[l],
            bias_map, w_pool[l], b_pool[l], pool_scale[l], w_proj_a[l], w_proj_b[l], w_out[l],
            w_up[l], w_down[l])
        x = _channel_call(
            x, mod, g_norm2[l], w_up_bf, conv_w[l], conv_b[l], w_down_bf, g_final,
            final_norm=(l == depth - 1))
    return x
```

```python
import functools
import math

import jax
import jax.numpy as jnp
from jax import lax
from jax.experimental import pallas as pl
from jax.experimental.pallas import tpu as pltpu

EPS = 1e-6
CHUNK = 64
GMLP_BLOCK = 128
A_GROUPS = 8
POOL_WINDOWS = (2, 4, 8, 16)
CONV_WIDTH = 3
N_ADA = 6

SUBLANES = 8
LANES = 128
BF16_TILE_ROWS = 16
POOL_HALO = 16
CONV_CARRY_ROWS = (CONV_WIDTH - 1) * SUBLANES
MXU_WIDTH = 256
MIXER_TILE = 512
MIXER_TILES_PER_STEP = 2
CHANNEL_TILE = 256
CHANNEL_TILES_PER_STEP = 2
FF_CHUNK = 256
UP_LOOKAHEAD = 11
ADA_TILE = 2048
VMEM_LIMIT_BYTES = 56 * 1024 * 1024

BF16 = jnp.bfloat16
F32 = jnp.float32


def _dot(a, b):
    return jnp.dot(a, b, preferred_element_type=F32)


def _dot_ref(a, w_ref):
    n = w_ref.shape[-1]
    assert n % MXU_WIDTH == 0
    return jnp.concatenate(
        [_dot(a, w_ref[:, c0:c0 + MXU_WIDTH]) for c0 in range(0, n, MXU_WIDTH)], axis=1)


def _gelu(x):
    return 0.5 * x * (1.0 + lax.erf(x * math.sqrt(0.5)))


def _rms_scale(x):
    ms = jnp.mean(x * x, axis=-1, keepdims=True)
    return x * lax.rsqrt(ms + EPS)


def _ada_kernel(c_ref, w_ref, b_ref, o_ref):
    c = c_ref[...]
    s = c * jax.nn.sigmoid(c)
    o_ref[...] = _dot(s.astype(BF16), w_ref[...].astype(BF16)) + b_ref[...]


def _ada_call(c, w_ada, b_ada):
    bsz, d = c.shape
    n = w_ada.shape[1]
    return pl.pallas_call(
        _ada_kernel,
        grid=(n // ADA_TILE,),
        in_specs=[
            pl.BlockSpec((bsz, d), lambda j: (0, 0)),
            pl.BlockSpec((d, ADA_TILE), lambda j: (0, j)),
            pl.BlockSpec((1, ADA_TILE), lambda j: (0, j)),
        ],
        out_specs=pl.BlockSpec((bsz, ADA_TILE), lambda j: (0, j)),
        out_shape=jax.ShapeDtypeStruct((bsz, n), F32),
        compiler_params=pltpu.CompilerParams(dimension_semantics=("arbitrary",)),
        name="ada_mod",
    )(c, w_ada, b_ada.reshape(1, n))


def _cast_block(src_ref, dst_ref):
    n = src_ref.shape[1]
    dst_ref[:, 0:n] = src_ref[...].astype(BF16)
    if dst_ref.shape[1] > n:
        dst_ref[:, n:] = jnp.zeros((dst_ref.shape[0], dst_ref.shape[1] - n), BF16)


def _mixer_kernel(x_ref, mod_ref, g1_ref, w_ref, b_gate_ref, lng_ref, lnb_ref,
                  wsp_ref, bsp_ref, wpool_ref, bpool_ref, pscale_ref, w_up_ref, w_down_ref,
                  o_ref, w_up_bf_ref, w_down_bf_ref, pool_carry_ref, *, in_width):
    ts, d = MIXER_TILE, x_ref.shape[2]
    j = pl.program_id(1)
    _cast_block(w_up_ref, w_up_bf_ref)
    _cast_block(w_down_ref, w_down_bf_ref)
    w_in_ref = w_ref.at[:, 0:in_width]
    w_gate_ref = w_ref.at[:, in_width:in_width + 2 * d]
    wpa_ref = w_ref.at[:, in_width + 2 * d:in_width + 3 * d]
    wpb_ref = w_ref.at[:, in_width + 3 * d:in_width + 4 * d]
    wout_ref = w_ref.at[:, in_width + 4 * d:in_width + 5 * d]

    @pl.when(j == 0)
    def _():
        pool_carry_ref[...] = jnp.zeros_like(pool_carry_ref)

    sh1 = mod_ref[0, 0:1, :]
    sc1 = mod_ref[0, 1:2, :]
    gt1 = mod_ref[0, 2:3, :]
    a_w = d
    gd = a_w // A_GROUPS
    p = lax.broadcasted_iota(jnp.int32, (GMLP_BLOCK, GMLP_BLOCK), 0)
    q = lax.broadcasted_iota(jnp.int32, (GMLP_BLOCK, GMLP_BLOCK), 1)
    allowed = (q // CHUNK) <= (p // CHUNK)
    w_masked = [jnp.where(allowed, wsp_ref[g], 0.0).astype(BF16) for g in range(A_GROUPS)]

    def token_tile(row0):
        x = x_ref[0, row0:row0 + ts, :]
        h = _rms_scale(x) * (g1_ref[...] * (1.0 + sc1)) + sh1
        hb = h.astype(BF16)

        v = _gelu(_dot_ref(hb, w_in_ref.at[:, a_w:2 * a_w]))
        hbp = _dot_ref(hb, w_in_ref.at[:, 2 * a_w:w_in_ref.shape[1]])
        gates = jax.nn.sigmoid(_dot_ref(hb, w_gate_ref) + b_gate_ref[...])
        u = _gelu(_dot_ref(hb, w_in_ref.at[:, 0:a_w]))
        mu = jnp.mean(v, axis=-1, keepdims=True)
        vc = v - mu
        var = jnp.mean(vc * vc, axis=-1, keepdims=True)
        vn = (vc * lax.rsqrt(var + EPS)) * lng_ref[...] + lnb_ref[...]
        vnb = vn.astype(BF16)

        ext = jnp.concatenate([pool_carry_ref[...], hbp], axis=0)
        pool_carry_ref[...] = hbp[ts - POOL_HALO:, :]
        bgd = hbp.shape[1] // len(POOL_WINDOWS)
        t = j * x_ref.shape[1] + row0 + lax.broadcasted_iota(jnp.int32, (ts, bgd), 0)
        yb_cols = []
        for gi, w in enumerate(POOL_WINDOWS):
            e = ext[:, gi * bgd:(gi + 1) * bgd]
            acc, span = e, 1
            while span < w:
                acc = acc + pltpu.roll(acc, span, 0)
                span *= 2
            win = acc[POOL_HALO:, :]
            cur = e[POOL_HALO:, :]
            count = jnp.minimum(t + 1, w).astype(F32)
            pooled = win / count - cur
            mixed = _dot(pooled.astype(BF16), wpool_ref[gi]) + bpool_ref[gi:gi + 1, :]
            yb_cols.append(mixed)
        y_b = jnp.concatenate(yb_cols, axis=1) * pscale_ref[...]
        branch_b = gates[:, d:] * _dot_ref(y_b.astype(BF16), wpb_ref)

        bias_map = bsp_ref[...]
        s_rows = []
        for n in range(ts // GMLP_BLOCK):
            b0 = n * GMLP_BLOCK
            cols = [_dot(w_masked[g], vnb[b0:b0 + GMLP_BLOCK, g * gd:(g + 1) * gd]) for g in range(A_GROUPS)]
            s_rows.append(jnp.concatenate(cols, axis=1) + bias_map)
        s = jnp.concatenate(s_rows, axis=0)
        y_a = u * s

        merged = gates[:, :d] * _dot_ref(y_a.astype(BF16), wpa_ref) + branch_b
        x1 = x + gt1 * _dot_ref(merged.astype(BF16), wout_ref)
        o_ref[0, row0:row0 + ts, :] = jnp.concatenate(
            [_interleave_rows(x1[r:r + CHANNEL_TILE, :]) for r in range(0, ts, CHANNEL_TILE)], axis=0)

    for row0 in range(0, x_ref.shape[1], ts):
        token_tile(row0)


def _resident(shape):
    nd = len(shape)
    return pl.BlockSpec(shape, lambda b, j: (0,) * nd, pipeline_mode=pl.Buffered(1))


def _pitch_pad(n_cols):
    return LANES if (n_cols // LANES) % SUBLANES == 0 else 0


def _pack_columns(ws):
    rows = ws[0].shape[0]
    pad = _pitch_pad(sum(w.shape[1] for w in ws))
    cols = list(ws) + ([jnp.zeros((rows, pad), ws[0].dtype)] if pad else [])
    return jnp.concatenate(cols, axis=1).astype(BF16)


def _row_blocking(n_rows, n_steps):
    for steps_per_block in range(1, n_steps + 1):
        if n_steps % steps_per_block == 0 and (n_rows * steps_per_block) % n_steps == 0:
            rows = n_rows * steps_per_block // n_steps
            if rows % BF16_TILE_ROWS == 0:
                return rows, steps_per_block
    raise ValueError(f"no bf16-aligned row blocking of {n_rows} rows over {n_steps} steps")


def _step_block(b, j, *, n_tiles, steps_per_block):
    return ((b * n_tiles + j) // steps_per_block, 0)


def _mixer_call(x, mod, g1, w_in, w_gate, b_gate, ln_g, ln_b, w_sp, bias_map, w_pool, b_pool,
                pool_scale, w_pa, w_pb, w_out, w_up, w_down):
    bsz, s_len, d = x.shape
    ts = MIXER_TILE * MIXER_TILES_PER_STEP
    n_tiles = s_len // ts
    assert s_len % ts == 0 and MIXER_TILE % GMLP_BLOCK == 0 and MIXER_TILE >= POOL_HALO
    assert MIXER_TILE % CHANNEL_TILE == 0
    assert w_gate.shape[1] == 2 * d and w_pa.shape == w_pb.shape == w_out.shape == (d, d)
    row = lambda a: a.reshape(1, -1)
    w_packed = _pack_columns([w_in, w_gate, w_pa, w_pb, w_out])
    operands = [
        (x, pl.BlockSpec((1, ts, d), lambda b, j: (b, j, 0))),
        (mod, pl.BlockSpec((1, N_ADA, d), lambda b, j: (b, 0, 0))),
    ]
    for a in (row(g1), w_packed, row(b_gate), row(ln_g), row(ln_b), w_sp, bias_map,
              w_pool.astype(BF16), b_pool, row(pool_scale)):
        operands.append((a, _resident(a.shape)))
    out_specs = [pl.BlockSpec((1, ts, d), lambda b, j: (b, j, 0))]
    out_shapes = [jax.ShapeDtypeStruct(x.shape, F32)]
    for w in (w_up, w_down):
        rows, steps_per_block = _row_blocking(w.shape[0], bsz * n_tiles)
        index_map = functools.partial(_step_block, n_tiles=n_tiles, steps_per_block=steps_per_block)
        operands.append((w, pl.BlockSpec((rows, w.shape[1]), index_map)))
        padded = w.shape[1] + _pitch_pad(w.shape[1])
        out_specs.append(pl.BlockSpec((rows, padded), index_map))
        out_shapes.append(jax.ShapeDtypeStruct((w.shape[0], padded), BF16))
    return pl.pallas_call(
        functools.partial(_mixer_kernel, in_width=w_in.shape[1]),
        grid=(bsz, n_tiles),
        in_specs=[spec for _, spec in operands],
        out_specs=out_specs,
        out_shape=out_shapes,
        scratch_shapes=[pltpu.VMEM((POOL_HALO, w_pool.shape[0] * w_pool.shape[1]), F32)],
        compiler_params=pltpu.CompilerParams(
            dimension_semantics=("arbitrary", "arbitrary"), vmem_limit_bytes=VMEM_LIMIT_BYTES),
        name="mixer",
    )(*[a for a, _ in operands])


def _interleave_rows(x):
    n, d = x.shape
    return jnp.swapaxes(x.reshape(SUBLANES, n // SUBLANES, d), 0, 1).reshape(n, d)


def _deinterleave_rows(y):
    n, d = y.shape
    return jnp.swapaxes(y.reshape(n // SUBLANES, SUBLANES, d), 0, 1).reshape(n, d)


def _delay_rows(a, prev_row):
    n = a.shape[0]
    wrapped = pltpu.roll(a[n - SUBLANES:, :], 1, 0)
    first = jnp.where(lax.broadcasted_iota(jnp.int32, wrapped.shape, 0) == 0, prev_row, wrapped)
    return jnp.concatenate([first, a[:n - SUBLANES, :]], axis=0)


def _channel_kernel(x_ref, mod_ref, g2_ref, w_up_ref, cw_ref, cb_ref, w_down_ref, gf_ref,
                    o_ref, conv_carry_ref, *, final_norm):
    ts, d = CHANNEL_TILE, x_ref.shape[2]
    d_ff = w_down_ref.shape[0]
    n_chunks = d_ff // FF_CHUNK

    @pl.when(pl.program_id(1) == 0)
    def _():
        conv_carry_ref[...] = jnp.zeros_like(conv_carry_ref)

    sh2 = mod_ref[0, 3:4, :]
    sc2 = mod_ref[0, 4:5, :]
    gt2 = mod_ref[0, 5:6, :]

    def conv_cols(pre, c0):
        cols = pl.ds(c0, FF_CHUNK)
        taps = [pre]
        for m in range(1, CONV_WIDTH):
            r = (CONV_WIDTH - 1 - m) * SUBLANES + SUBLANES - 1
            taps.append(_delay_rows(taps[-1], conv_carry_ref[r:r + 1, cols]))
        conv_carry_ref[:, cols] = pre[ts - CONV_CARRY_ROWS:, :]
        out = cb_ref[:, cols]
        for k in range(CONV_WIDTH):
            out = out + taps[CONV_WIDTH - 1 - k] * cw_ref[k:k + 1, cols]
        return out

    def token_tile(r0):
        x = x_ref[0, r0:r0 + ts, :]
        h2 = _rms_scale(x) * (g2_ref[...] * (1.0 + sc2)) + sh2
        h2b = h2.astype(BF16)

        def up_project(c):
            return [_dot(h2b, w_up_ref[:, pl.ds(c0, FF_CHUNK)]) for c0 in (c * FF_CHUNK, d_ff + c * FF_CHUNK)]

        ahead = [up_project(i) for i in range(min(UP_LOOKAHEAD, n_chunks))]
        acc = jnp.zeros((ts, d), F32)
        for c in range(n_chunks):
            cur = ahead.pop(0)
            if c + UP_LOOKAHEAD < n_chunks:
                ahead.append(up_project(c + UP_LOOKAHEAD))
            gate = conv_cols(cur[0], c * FF_CHUNK)
            val = conv_cols(cur[1], d_ff + c * FF_CHUNK)
            f = gate * jax.nn.sigmoid(gate) * val
            acc = acc + _dot_ref(f.astype(BF16), w_down_ref.at[c * FF_CHUNK:(c + 1) * FF_CHUNK, 0:d])
        x2 = x + gt2 * acc
        if final_norm:
            x2 = _rms_scale(x2) * gf_ref[...]
        o_ref[0, r0:r0 + ts, :] = _deinterleave_rows(x2)

    for r0 in range(0, x_ref.shape[1], ts):
        token_tile(r0)


def _channel_call(x, mod, g2, w_up, conv_w, conv_b, w_down, g_final, final_norm):
    bsz, s_len, d = x.shape
    ts = CHANNEL_TILE * CHANNEL_TILES_PER_STEP
    assert s_len % ts == 0 and CHANNEL_TILE >= CONV_WIDTH * SUBLANES and w_down.shape[0] % FF_CHUNK == 0
    row = lambda a: a.reshape(1, -1)
    operands = [
        (x, pl.BlockSpec((1, ts, d), lambda b, j: (b, j, 0))),
        (mod, pl.BlockSpec((1, N_ADA, d), lambda b, j: (b, 0, 0))),
    ]
    for a in (row(g2), w_up, conv_w, row(conv_b), w_down, row(g_final)):
        operands.append((a, _resident(a.shape)))
    return pl.pallas_call(
        functools.partial(_channel_kernel, final_norm=final_norm),
        grid=(bsz, s_len // ts),
        in_specs=[spec for _, spec in operands],
        out_specs=pl.BlockSpec((1, ts, d), lambda b, j: (b, j, 0)),
        out_shape=jax.ShapeDtypeStruct(x.shape, F32),
        scratch_shapes=[pltpu.VMEM((CONV_CARRY_ROWS, conv_w.shape[1]), F32)],
        compiler_params=pltpu.CompilerParams(
            dimension_semantics=("arbitrary", "arbitrary"), vmem_limit_bytes=VMEM_LIMIT_BYTES),
        name="channel",
    )(*[a for a, _ in operands])


def kernel(x, c, w_ada, b_ada, g_norm1, w_in, ln_v_g, ln_v_b, w_spatial, b_spatial, w_pool, b_pool,
           pool_scale, w_proj_a, w_proj_b, w_gate, b_gate, w_out, g_norm2, w_up, conv_w, conv_b,
           w_down, g_final):
    depth = w_ada.shape[0]
    bsz, s_len, d = x.shape
    assert w_ada.shape[2] % ADA_TILE == 0
    gd = w_in.shape[2] // 3 // A_GROUPS
    for l in range(depth):
        mod = _ada_call(c, w_ada[l], b_ada[l]).reshape(bsz, N_ADA, d)
        bias_map = jnp.repeat(b_spatial[l].T, gd, axis=1)
        x, w_up_bf, w_down_bf = _mixer_call(
            x, mod, g_norm1[l], w_in[l], w_gate[l], b_gate[l], ln_v_g[l], ln_v_b[l], w_spatial[l],
            bias_map, w_pool[l], b_pool[l], pool_scale[l], w_proj_a[l], w_proj_b[l], w_out[l],
            w_up[l], w_down[l])
        x = _channel_call(
            x, mod, g_norm2[l], w_up_bf, conv_w[l], conv_b[l], w_down_bf, g_final,
            final_norm=(l == depth - 1))
    return x
```

```python
import functools
import math

import jax
import jax.numpy as jnp
from jax import lax
from jax.experimental import pallas as pl
from jax.experimental.pallas import tpu as pltpu

EPS = 1e-6
CHUNK = 64
GMLP_BLOCK = 128
A_GROUPS = 8
POOL_WINDOWS = (2, 4, 8, 16)
CONV_WIDTH = 3
N_ADA = 6

SUBLANES = 8
LANES = 128
BF16_TILE_ROWS = 16
POOL_HALO = 16
CONV_CARRY_ROWS = (CONV_WIDTH - 1) * SUBLANES
MXU_WIDTH = 256
MIXER_TILE = 256
MIXER_TILES_PER_STEP = 4
CHANNEL_TILE = 256
CHANNEL_TILES_PER_STEP = 2
FF_CHUNK = 256
UP_LOOKAHEAD = 5
ADA_TILE = 2048
VMEM_LIMIT_BYTES = 56 * 1024 * 1024

BF16 = jnp.bfloat16
F32 = jnp.float32


def _dot(a, b):
    return jnp.dot(a, b, preferred_element_type=F32)


def _dot_ref(a, w_ref):
    n = w_ref.shape[-1]
    assert n % MXU_WIDTH == 0
    return jnp.concatenate(
        [_dot(a, w_ref[:, c0:c0 + MXU_WIDTH]) for c0 in range(0, n, MXU_WIDTH)], axis=1)


def _gelu(x):
    return 0.5 * x * (1.0 + lax.erf(x * math.sqrt(0.5)))


def _rms_scale(x):
    ms = jnp.mean(x * x, axis=-1, keepdims=True)
    return x * lax.rsqrt(ms + EPS)


def _ada_kernel(c_ref, w_ref, b_ref, o_ref):
    c = c_ref[...]
    s = c * jax.nn.sigmoid(c)
    o_ref[...] = _dot(s.astype(BF16), w_ref[...].astype(BF16)) + b_ref[...]


def _ada_call(c, w_ada, b_ada):
    bsz, d = c.shape
    n = w_ada.shape[1]
    return pl.pallas_call(
        _ada_kernel,
        grid=(n // ADA_TILE,),
        in_specs=[
            pl.BlockSpec((bsz, d), lambda j: (0, 0)),
            pl.BlockSpec((d, ADA_TILE), lambda j: (0, j)),
            pl.BlockSpec((1, ADA_TILE), lambda j: (0, j)),
        ],
        out_specs=pl.BlockSpec((bsz, ADA_TILE), lambda j: (0, j)),
        out_shape=jax.ShapeDtypeStruct((bsz, n), F32),
        compiler_params=pltpu.CompilerParams(dimension_semantics=("arbitrary",)),
        name="ada_mod",
    )(c, w_ada, b_ada.reshape(1, n))


def _cast_block(src_ref, dst_ref):
    n = src_ref.shape[1]
    dst_ref[:, 0:n] = src_ref[...].astype(BF16)
    if dst_ref.shape[1] > n:
        dst_ref[:, n:] = jnp.zeros((dst_ref.shape[0], dst_ref.shape[1] - n), BF16)


def _mixer_kernel(x_ref, mod_ref, g1_ref, w_ref, b_gate_ref, lng_ref, lnb_ref,
                  wsp_ref, bsp_ref, wpool_ref, bpool_ref, pscale_ref, w_up_ref, w_down_ref,
                  o_ref, w_up_bf_ref, w_down_bf_ref, pool_carry_ref, *, in_width):
    ts, d = MIXER_TILE, x_ref.shape[2]
    j = pl.program_id(1)
    _cast_block(w_up_ref, w_up_bf_ref)
    _cast_block(w_down_ref, w_down_bf_ref)
    w_in_ref = w_ref.at[:, 0:in_width]
    w_gate_ref = w_ref.at[:, in_width:in_width + 2 * d]
    wpa_ref = w_ref.at[:, in_width + 2 * d:in_width + 3 * d]
    wpb_ref = w_ref.at[:, in_width + 3 * d:in_width + 4 * d]
    wout_ref = w_ref.at[:, in_width + 4 * d:in_width + 5 * d]

    @pl.when(j == 0)
    def _():
        pool_carry_ref[...] = jnp.zeros_like(pool_carry_ref)

    sh1 = mod_ref[0, 0:1, :]
    sc1 = mod_ref[0, 1:2, :]
    gt1 = mod_ref[0, 2:3, :]
    a_w = d
    gd = a_w // A_GROUPS
    p = lax.broadcasted_iota(jnp.int32, (GMLP_BLOCK, GMLP_BLOCK), 0)
    q = lax.broadcasted_iota(jnp.int32, (GMLP_BLOCK, GMLP_BLOCK), 1)
    allowed = (q // CHUNK) <= (p // CHUNK)
    w_masked = [jnp.where(allowed, wsp_ref[g], 0.0).astype(BF16) for g in range(A_GROUPS)]

    def token_tile(row0):
        x = x_ref[0, row0:row0 + ts, :]
        h = _rms_scale(x) * (g1_ref[...] * (1.0 + sc1)) + sh1
        hb = h.astype(BF16)

        v = _gelu(_dot_ref(hb, w_in_ref.at[:, a_w:2 * a_w]))
        hbp = _dot_ref(hb, w_in_ref.at[:, 2 * a_w:w_in_ref.shape[1]])
        gates = jax.nn.sigmoid(_dot_ref(hb, w_gate_ref) + b_gate_ref[...])
        u = _gelu(_dot_ref(hb, w_in_ref.at[:, 0:a_w]))
        mu = jnp.mean(v, axis=-1, keepdims=True)
        vc = v - mu
        var = jnp.mean(vc * vc, axis=-1, keepdims=True)
        vn = (vc * lax.rsqrt(var + EPS)) * lng_ref[...] + lnb_ref[...]
        vnb = vn.astype(BF16)

        ext = jnp.concatenate([pool_carry_ref[...], hbp], axis=0)
        pool_carry_ref[...] = hbp[ts - POOL_HALO:, :]
        bgd = hbp.shape[1] // len(POOL_WINDOWS)
        t = j * x_ref.shape[1] + row0 + lax.broadcasted_iota(jnp.int32, (ts, bgd), 0)
        yb_cols = []
        for gi, w in enumerate(POOL_WINDOWS):
            e = ext[:, gi * bgd:(gi + 1) * bgd]
            acc, span = e, 1
            while span < w:
                acc = acc + pltpu.roll(acc, span, 0)
                span *= 2
            win = acc[POOL_HALO:, :]
            cur = e[POOL_HALO:, :]
            count = jnp.minimum(t + 1, w).astype(F32)
            pooled = win / count - cur
            mixed = _dot(pooled.astype(BF16), wpool_ref[gi]) + bpool_ref[gi:gi + 1, :]
            yb_cols.append(mixed)
        y_b = jnp.concatenate(yb_cols, axis=1) * pscale_ref[...]
        branch_b = gates[:, d:] * _dot_ref(y_b.astype(BF16), wpb_ref)

        bias_map = bsp_ref[...]
        s_rows = []
        for n in range(ts // GMLP_BLOCK):
            b0 = n * GMLP_BLOCK
            cols = [_dot(w_masked[g], vnb[b0:b0 + GMLP_BLOCK, g * gd:(g + 1) * gd]) for g in range(A_GROUPS)]
            s_rows.append(jnp.concatenate(cols, axis=1) + bias_map)
        s = jnp.concatenate(s_rows, axis=0)
        y_a = u * s

        merged = gates[:, :d] * _dot_ref(y_a.astype(BF16), wpa_ref) + branch_b
        x1 = x + gt1 * _dot_ref(merged.astype(BF16), wout_ref)
        o_ref[0, row0:row0 + ts, :] = jnp.concatenate(
            [_interleave_rows(x1[r:r + CHANNEL_TILE, :]) for r in range(0, ts, CHANNEL_TILE)], axis=0)

    for row0 in range(0, x_ref.shape[1], ts):
        token_tile(row0)


def _resident(shape):
    nd = len(shape)
    return pl.BlockSpec(shape, lambda b, j: (0,) * nd, pipeline_mode=pl.Buffered(1))


def _pitch_pad(n_cols):
    return LANES if (n_cols // LANES) % SUBLANES == 0 else 0


def _pack_columns(ws):
    rows = ws[0].shape[0]
    pad = _pitch_pad(sum(w.shape[1] for w in ws))
    cols = list(ws) + ([jnp.zeros((rows, pad), ws[0].dtype)] if pad else [])
    return jnp.concatenate(cols, axis=1).astype(BF16)


def _row_blocking(n_rows, n_steps):
    for steps_per_block in range(1, n_steps + 1):
        if n_steps % steps_per_block == 0 and (n_rows * steps_per_block) % n_steps == 0:
            rows = n_rows * steps_per_block // n_steps
            if rows % BF16_TILE_ROWS == 0:
                return rows, steps_per_block
    raise ValueError(f"no bf16-aligned row blocking of {n_rows} rows over {n_steps} steps")


def _step_block(b, j, *, n_tiles, steps_per_block):
    return ((b * n_tiles + j) // steps_per_block, 0)


def _mixer_call(x, mod, g1, w_in, w_gate, b_gate, ln_g, ln_b, w_sp, bias_map, w_pool, b_pool,
                pool_scale, w_pa, w_pb, w_out, w_up, w_down):
    bsz, s_len, d = x.shape
    ts = MIXER_TILE * MIXER_TILES_PER_STEP
    n_tiles = s_len // ts
    assert s_len % ts == 0 and MIXER_TILE % GMLP_BLOCK == 0 and MIXER_TILE >= POOL_HALO
    assert MIXER_TILE % CHANNEL_TILE == 0
    assert w_gate.shape[1] == 2 * d and w_pa.shape == w_pb.shape == w_out.shape == (d, d)
    row = lambda a: a.reshape(1, -1)
    w_packed = _pack_columns([w_in, w_gate, w_pa, w_pb, w_out])
    operands = [
        (x, pl.BlockSpec((1, ts, d), lambda b, j: (b, j, 0))),
        (mod, pl.BlockSpec((1, N_ADA, d), lambda b, j: (b, 0, 0))),
    ]
    for a in (row(g1), w_packed, row(b_gate), row(ln_g), row(ln_b), w_sp, bias_map,
              w_pool.astype(BF16), b_pool, row(pool_scale)):
        operands.append((a, _resident(a.shape)))
    out_specs = [pl.BlockSpec((1, ts, d), lambda b, j: (b, j, 0))]
    out_shapes = [jax.ShapeDtypeStruct(x.shape, F32)]
    for w in (w_up, w_down):
        rows, steps_per_block = _row_blocking(w.shape[0], bsz * n_tiles)
        index_map = functools.partial(_step_block, n_tiles=n_tiles, steps_per_block=steps_per_block)
        operands.append((w, pl.BlockSpec((rows, w.shape[1]), index_map)))
        padded = w.shape[1] + _pitch_pad(w.shape[1])
        out_specs.append(pl.BlockSpec((rows, padded), index_map))
        out_shapes.append(jax.ShapeDtypeStruct((w.shape[0], padded), BF16))
    return pl.pallas_call(
        functools.partial(_mixer_kernel, in_width=w_in.shape[1]),
        grid=(bsz, n_tiles),
        in_specs=[spec for _, spec in operands],
        out_specs=out_specs,
        out_shape=out_shapes,
        scratch_shapes=[pltpu.VMEM((POOL_HALO, w_pool.shape[0] * w_pool.shape[1]), F32)],
        compiler_params=pltpu.CompilerParams(
            dimension_semantics=("arbitrary", "arbitrary"), vmem_limit_bytes=VMEM_LIMIT_BYTES),
        name="mixer",
    )(*[a for a, _ in operands])


def _interleave_rows(x):
    n, d = x.shape
    return jnp.swapaxes(x.reshape(SUBLANES, n // SUBLANES, d), 0, 1).reshape(n, d)


def _deinterleave_rows(y):
    n, d = y.shape
    return jnp.swapaxes(y.reshape(n // SUBLANES, SUBLANES, d), 0, 1).reshape(n, d)


def _delay_rows(a, prev_row):
    n = a.shape[0]
    wrapped = pltpu.roll(a[n - SUBLANES:, :], 1, 0)
    first = jnp.where(lax.broadcasted_iota(jnp.int32, wrapped.shape, 0) == 0, prev_row, wrapped)
    return jnp.concatenate([first, a[:n - SUBLANES, :]], axis=0)


def _channel_kernel(x_ref, mod_ref, g2_ref, w_up_ref, cw_ref, cb_ref, w_down_ref, gf_ref,
                    o_ref, conv_carry_ref, *, final_norm):
    ts, d = CHANNEL_TILE, x_ref.shape[2]
    d_ff = w_down_ref.shape[0]
    n_chunks = d_ff // FF_CHUNK

    @pl.when(pl.program_id(1) == 0)
    def _():
        conv_carry_ref[...] = jnp.zeros_like(conv_carry_ref)

    sh2 = mod_ref[0, 3:4, :]
    sc2 = mod_ref[0, 4:5, :]
    gt2 = mod_ref[0, 5:6, :]

    def conv_cols(pre, c0):
        cols = pl.ds(c0, FF_CHUNK)
        taps = [pre]
        for m in range(1, CONV_WIDTH):
            r = (CONV_WIDTH - 1 - m) * SUBLANES + SUBLANES - 1
            taps.append(_delay_rows(taps[-1], conv_carry_ref[r:r + 1, cols]))
        conv_carry_ref[:, cols] = pre[ts - CONV_CARRY_ROWS:, :]
        out = cb_ref[:, cols]
        for k in range(CONV_WIDTH):
            out = out + taps[CONV_WIDTH - 1 - k] * cw_ref[k:k + 1, cols]
        return out

    def token_tile(r0):
        x = x_ref[0, r0:r0 + ts, :]
        h2 = _rms_scale(x) * (g2_ref[...] * (1.0 + sc2)) + sh2
        h2b = h2.astype(BF16)

        def up_project(c):
            return [_dot(h2b, w_up_ref[:, pl.ds(c0, FF_CHUNK)]) for c0 in (c * FF_CHUNK, d_ff + c * FF_CHUNK)]

        ahead = [up_project(i) for i in range(min(UP_LOOKAHEAD, n_chunks))]
        acc = jnp.zeros((ts, d), F32)
        for c in range(n_chunks):
            cur = ahead.pop(0)
            if c + UP_LOOKAHEAD < n_chunks:
                ahead.append(up_project(c + UP_LOOKAHEAD))
            gate = conv_cols(cur[0], c * FF_CHUNK)
            val = conv_cols(cur[1], d_ff + c * FF_CHUNK)
            f = gate * jax.nn.sigmoid(gate) * val
            acc = acc + _dot_ref(f.astype(BF16), w_down_ref.at[c * FF_CHUNK:(c + 1) * FF_CHUNK, 0:d])
        x2 = x + gt2 * acc
        if final_norm:
            x2 = _rms_scale(x2) * gf_ref[...]
        o_ref[0, r0:r0 + ts, :] = _deinterleave_rows(x2)

    for r0 in range(0, x_ref.shape[1], ts):
        token_tile(r0)


def _channel_call(x, mod, g2, w_up, conv_w, conv_b, w_down, g_final, final_norm):
    bsz, s_len, d = x.shape
    ts = CHANNEL_TILE * CHANNEL_TILES_PER_STEP
    assert s_len % ts == 0 and CHANNEL_TILE >= CONV_WIDTH * SUBLANES and w_down.shape[0] % FF_CHUNK == 0
    row = lambda a: a.reshape(1, -1)
    operands = [
        (x, pl.BlockSpec((1, ts, d), lambda b, j: (b, j, 0))),
        (mod, pl.BlockSpec((1, N_ADA, d), lambda b, j: (b, 0, 0))),
    ]
    for a in (row(g2), w_up, conv_w, row(conv_b), w_down, row(g_final)):
        operands.append((a, _resident(a.shape)))
    return pl.pallas_call(
        functools.partial(_channel_kernel, final_norm=final_norm),
        grid=(bsz, s_len // ts),
        in_specs=[spec for _, spec in operands],
        out_specs=pl.BlockSpec((1, ts, d), lambda b, j: (b, j, 0)),
        out_shape=jax.ShapeDtypeStruct(x.shape, F32),
        scratch_shapes=[pltpu.VMEM((CONV_CARRY_ROWS, conv_w.shape[1]), F32)],
        compiler_params=pltpu.CompilerParams(
            dimension_semantics=("arbitrary", "arbitrary"), vmem_limit_bytes=VMEM_LIMIT_BYTES),
        name="channel",
    )(*[a for a, _ in operands])


def kernel(x, c, w_ada, b_ada, g_norm1, w_in, ln_v_g, ln_v_b, w_spatial, b_spatial, w_pool, b_pool,
           pool_scale, w_proj_a, w_proj_b, w_gate, b_gate, w_out, g_norm2, w_up, conv_w, conv_b,
           w_down, g_final):
    depth = w_ada.shape[0]
    bsz, s_len, d = x.shape
    assert w_ada.shape[2] % ADA_TILE == 0
    gd = w_in.shape[2] // 3 // A_GROUPS
    for l in range(depth):
        mod = _ada_call(c, w_ada[l], b_ada[l]).reshape(bsz, N_ADA, d)
        bias_map = jnp.repeat(b_spatial[l].T, gd, axis=1)
        x, w_up_bf, w_down_bf = _mixer_call(
            x, mod, g_norm1[l], w_in[l], w_gate[l], b_gate[l], ln_v_g[l], ln_v_b[l], w_spatial[l],
            bias_map, w_pool[l], b_pool[l], pool_scale[l], w_proj_a[l], w_proj_b[l], w_out[l],
            w_up[l], w_down[l])
        x = _channel_call(
            x, mod, g_norm2[l], w_up_bf, conv_w[l], conv_b[l], w_down_bf, g_final,
            final_norm=(l == depth - 1))
    return x
```

```python
import functools
import math

import jax
import jax.numpy as jnp
from jax import lax
from jax.experimental import pallas as pl
from jax.experimental.pallas import tpu as pltpu

EPS = 1e-6
CHUNK = 64
GMLP_BLOCK = 128
A_GROUPS = 8
POOL_WINDOWS = (2, 4, 8, 16)
CONV_WIDTH = 3
N_ADA = 6

SUBLANES = 8
LANES = 128
BF16_TILE_ROWS = 16
POOL_HALO = 16
CONV_CARRY_ROWS = (CONV_WIDTH - 1) * SUBLANES
MXU_WIDTH = 256
MIXER_TILE = 256
MIXER_TILES_PER_STEP = 4
MIXER_PREP_STEPS = 8
CHANNEL_TILE = 256
CHANNEL_TILES_PER_STEP = 2
FF_CHUNK = 256
UP_LOOKAHEAD = 5
ADA_TILE = 2048
VMEM_LIMIT_BYTES = 56 * 1024 * 1024

BF16 = jnp.bfloat16
F32 = jnp.float32


def _dot(a, b):
    return jnp.dot(a, b, preferred_element_type=F32)


def _dot_ref(a, w_ref):
    n = w_ref.shape[-1]
    assert n % MXU_WIDTH == 0
    return jnp.concatenate(
        [_dot(a, w_ref[:, c0:c0 + MXU_WIDTH]) for c0 in range(0, n, MXU_WIDTH)], axis=1)


def _gelu(x):
    return 0.5 * x * (1.0 + lax.erf(x * math.sqrt(0.5)))


def _rms_scale(x):
    ms = jnp.mean(x * x, axis=-1, keepdims=True)
    return x * lax.rsqrt(ms + EPS)


def _ada_kernel(c_ref, w_ref, b_ref, o_ref):
    c = c_ref[...]
    s = c * jax.nn.sigmoid(c)
    o_ref[...] = _dot(s.astype(BF16), w_ref[...].astype(BF16)) + b_ref[...]


def _ada_call(c, w_ada, b_ada):
    bsz, d = c.shape
    n = w_ada.shape[1]
    return pl.pallas_call(
        _ada_kernel,
        grid=(n // ADA_TILE,),
        in_specs=[
            pl.BlockSpec((bsz, d), lambda j: (0, 0)),
            pl.BlockSpec((d, ADA_TILE), lambda j: (0, j)),
            pl.BlockSpec((1, ADA_TILE), lambda j: (0, j)),
        ],
        out_specs=pl.BlockSpec((bsz, ADA_TILE), lambda j: (0, j)),
        out_shape=jax.ShapeDtypeStruct((bsz, n), F32),
        compiler_params=pltpu.CompilerParams(dimension_semantics=("arbitrary",)),
        name="ada_mod",
    )(c, w_ada, b_ada.reshape(1, n))


def _cast_block(src_ref, dst_ref):
    n = src_ref.shape[1]
    dst_ref[:, 0:n] = src_ref[...].astype(BF16)
    if dst_ref.shape[1] > n:
        dst_ref[:, n:] = jnp.zeros((dst_ref.shape[0], dst_ref.shape[1] - n), BF16)


def _mixer_kernel(x_ref, mod_ref, g1_ref, b_gate_ref, lng_ref, lnb_ref, wsp_ref, bsp_ref, wpool_ref,
                  bpool_ref, pscale_ref, w_in_f32, w_gate_f32, wpa_f32, wpb_f32, wout_f32,
                  w_up_ref, w_down_ref, o_ref, w_up_bf_ref, w_down_bf_ref, w_ref, pool_carry_ref,
                  *, n_prep, n_tiles):
    step = pl.program_id(0)

    @pl.when(step < n_prep)
    def _():
        rows = w_in_f32.shape[0]
        r0 = pl.multiple_of(step * rows, rows)
        off = 0
        for src in (w_in_f32, w_gate_f32, wpa_f32, wpb_f32, wout_f32):
            w_ref[pl.ds(r0, rows), off:off + src.shape[1]] = src[...].astype(BF16)
            off += src.shape[1]

    @pl.when(step >= n_prep)
    def _():
        _mixer_tokens(x_ref, mod_ref, g1_ref, b_gate_ref, lng_ref, lnb_ref, wsp_ref, bsp_ref, wpool_ref,
                      bpool_ref, pscale_ref, w_up_ref, w_down_ref, o_ref, w_up_bf_ref, w_down_bf_ref,
                      w_ref, pool_carry_ref, in_width=w_in_f32.shape[1], j=(step - n_prep) % n_tiles)


def _mixer_tokens(x_ref, mod_ref, g1_ref, b_gate_ref, lng_ref, lnb_ref, wsp_ref, bsp_ref, wpool_ref,
                  bpool_ref, pscale_ref, w_up_ref, w_down_ref, o_ref, w_up_bf_ref, w_down_bf_ref,
                  w_ref, pool_carry_ref, *, in_width, j):
    ts, d = MIXER_TILE, x_ref.shape[2]
    _cast_block(w_up_ref, w_up_bf_ref)
    _cast_block(w_down_ref, w_down_bf_ref)
    w_in_ref = w_ref.at[:, 0:in_width]
    w_gate_ref = w_ref.at[:, in_width:in_width + 2 * d]
    wpa_ref = w_ref.at[:, in_width + 2 * d:in_width + 3 * d]
    wpb_ref = w_ref.at[:, in_width + 3 * d:in_width + 4 * d]
    wout_ref = w_ref.at[:, in_width + 4 * d:in_width + 5 * d]

    @pl.when(j == 0)
    def _():
        pool_carry_ref[...] = jnp.zeros_like(pool_carry_ref)

    sh1 = mod_ref[0, 0:1, :]
    sc1 = mod_ref[0, 1:2, :]
    gt1 = mod_ref[0, 2:3, :]
    a_w = d
    gd = a_w // A_GROUPS
    p = lax.broadcasted_iota(jnp.int32, (GMLP_BLOCK, GMLP_BLOCK), 0)
    q = lax.broadcasted_iota(jnp.int32, (GMLP_BLOCK, GMLP_BLOCK), 1)
    allowed = (q // CHUNK) <= (p // CHUNK)
    w_masked = [jnp.where(allowed, wsp_ref[g], 0.0).astype(BF16) for g in range(A_GROUPS)]

    def token_tile(row0):
        x = x_ref[0, row0:row0 + ts, :]
        h = _rms_scale(x) * (g1_ref[...] * (1.0 + sc1)) + sh1
        hb = h.astype(BF16)

        v = _gelu(_dot_ref(hb, w_in_ref.at[:, a_w:2 * a_w]))
        hbp = _dot_ref(hb, w_in_ref.at[:, 2 * a_w:w_in_ref.shape[1]])
        gates = jax.nn.sigmoid(_dot_ref(hb, w_gate_ref) + b_gate_ref[...])
        u = _gelu(_dot_ref(hb, w_in_ref.at[:, 0:a_w]))
        mu = jnp.mean(v, axis=-1, keepdims=True)
        vc = v - mu
        var = jnp.mean(vc * vc, axis=-1, keepdims=True)
        vn = (vc * lax.rsqrt(var + EPS)) * lng_ref[...] + lnb_ref[...]
        vnb = vn.astype(BF16)

        ext = jnp.concatenate([pool_carry_ref[...], hbp], axis=0)
        pool_carry_ref[...] = hbp[ts - POOL_HALO:, :]
        bgd = hbp.shape[1] // len(POOL_WINDOWS)
        t = j * x_ref.shape[1] + row0 + lax.broadcasted_iota(jnp.int32, (ts, bgd), 0)
        yb_cols = []
        for gi, w in enumerate(POOL_WINDOWS):
            e = ext[:, gi * bgd:(gi + 1) * bgd]
            acc, span = e, 1
            while span < w:
                acc = acc + pltpu.roll(acc, span, 0)
                span *= 2
            win = acc[POOL_HALO:, :]
            cur = e[POOL_HALO:, :]
            count = jnp.minimum(t + 1, w).astype(F32)
            pooled = win / count - cur
            mixed = _dot(pooled.astype(BF16), wpool_ref[gi]) + bpool_ref[gi:gi + 1, :]
            yb_cols.append(mixed)
        y_b = jnp.concatenate(yb_cols, axis=1) * pscale_ref[...]
        branch_b = gates[:, d:] * _dot_ref(y_b.astype(BF16), wpb_ref)

        bias_map = bsp_ref[...]
        s_rows = []
        for n in range(ts // GMLP_BLOCK):
            b0 = n * GMLP_BLOCK
            cols = [_dot(w_masked[g], vnb[b0:b0 + GMLP_BLOCK, g * gd:(g + 1) * gd]) for g in range(A_GROUPS)]
            s_rows.append(jnp.concatenate(cols, axis=1) + bias_map)
        s = jnp.concatenate(s_rows, axis=0)
        y_a = u * s

        merged = gates[:, :d] * _dot_ref(y_a.astype(BF16), wpa_ref) + branch_b
        x1 = x + gt1 * _dot_ref(merged.astype(BF16), wout_ref)
        o_ref[0, row0:row0 + ts, :] = jnp.concatenate(
            [_interleave_rows(x1[r:r + CHANNEL_TILE, :]) for r in range(0, ts, CHANNEL_TILE)], axis=0)

    for row0 in range(0, x_ref.shape[1], ts):
        token_tile(row0)


def _resident(shape):
    nd = len(shape)
    return pl.BlockSpec(shape, lambda b, j: (0,) * nd, pipeline_mode=pl.Buffered(1))


def _pitch_pad(n_cols):
    return LANES if (n_cols // LANES) % SUBLANES == 0 else 0


def _row_blocking(n_rows, n_steps):
    for steps_per_block in range(1, n_steps + 1):
        if n_steps % steps_per_block == 0 and (n_rows * steps_per_block) % n_steps == 0:
            rows = n_rows * steps_per_block // n_steps
            if rows % BF16_TILE_ROWS == 0:
                return rows, steps_per_block
    raise ValueError(f"no bf16-aligned row blocking of {n_rows} rows over {n_steps} steps")


def _mixer_call(x, mod, g1, w_in, w_gate, b_gate, ln_g, ln_b, w_sp, bias_map, w_pool, b_pool,
                pool_scale, w_pa, w_pb, w_out, w_up, w_down):
    bsz, s_len, d = x.shape
    ts = MIXER_TILE * MIXER_TILES_PER_STEP
    n_tiles = s_len // ts
    n_prep = MIXER_PREP_STEPS
    assert s_len % ts == 0 and MIXER_TILE % GMLP_BLOCK == 0 and MIXER_TILE >= POOL_HALO
    assert MIXER_TILE % CHANNEL_TILE == 0
    assert w_gate.shape[1] == 2 * d and w_pa.shape == w_pb.shape == w_out.shape == (d, d)
    assert d % (n_prep * BF16_TILE_ROWS) == 0
    row = lambda a: a.reshape(1, -1)

    def token_step(t):
        return jnp.maximum(t - n_prep, 0)

    def token_block(t):
        return (token_step(t) // n_tiles, token_step(t) % n_tiles, 0)

    def resident(a):
        nd = a.ndim
        return pl.BlockSpec(a.shape, lambda t: (0,) * nd, pipeline_mode=pl.Buffered(1))

    operands = [
        (x, pl.BlockSpec((1, ts, d), token_block)),
        (mod, pl.BlockSpec((1, N_ADA, d), lambda t: (token_step(t) // n_tiles, 0, 0))),
    ]
    for a in (row(g1), row(b_gate), row(ln_g), row(ln_b), w_sp, bias_map, w_pool.astype(BF16), b_pool,
              row(pool_scale)):
        operands.append((a, resident(a)))
    mixer_weights = (w_in, w_gate, w_pa, w_pb, w_out)
    for w in mixer_weights:
        operands.append((w, pl.BlockSpec((d // n_prep, w.shape[1]), lambda t: (jnp.minimum(t, n_prep - 1), 0))))
    out_specs = [pl.BlockSpec((1, ts, d), token_block)]
    out_shapes = [jax.ShapeDtypeStruct(x.shape, F32)]
    for w in (w_up, w_down):
        rows, steps_per_block = _row_blocking(w.shape[0], bsz * n_tiles)
        index_map = lambda t, spb=steps_per_block: (token_step(t) // spb, 0)
        operands.append((w, pl.BlockSpec((rows, w.shape[1]), index_map)))
        padded = w.shape[1] + _pitch_pad(w.shape[1])
        out_specs.append(pl.BlockSpec((rows, padded), index_map))
        out_shapes.append(jax.ShapeDtypeStruct((w.shape[0], padded), BF16))
    packed_width = sum(w.shape[1] for w in mixer_weights)
    packed_width += _pitch_pad(packed_width)
    return pl.pallas_call(
        functools.partial(_mixer_kernel, n_prep=n_prep, n_tiles=n_tiles),
        grid=(n_prep + bsz * n_tiles,),
        in_specs=[spec for _, spec in operands],
        out_specs=out_specs,
        out_shape=out_shapes,
        scratch_shapes=[pltpu.VMEM((d, packed_width), BF16),
                        pltpu.VMEM((POOL_HALO, w_pool.shape[0] * w_pool.shape[1]), F32)],
        compiler_params=pltpu.CompilerParams(
            dimension_semantics=("arbitrary",), vmem_limit_bytes=VMEM_LIMIT_BYTES),
        name="mixer",
    )(*[a for a, _ in operands])


def _interleave_rows(x):
    n, d = x.shape
    return jnp.swapaxes(x.reshape(SUBLANES, n // SUBLANES, d), 0, 1).reshape(n, d)


def _deinterleave_rows(y):
    n, d = y.shape
    return jnp.swapaxes(y.reshape(n // SUBLANES, SUBLANES, d), 0, 1).reshape(n, d)


def _delay_rows(a, prev_row):
    n = a.shape[0]
    wrapped = pltpu.roll(a[n - SUBLANES:, :], 1, 0)
    first = jnp.where(lax.broadcasted_iota(jnp.int32, wrapped.shape, 0) == 0, prev_row, wrapped)
    return jnp.concatenate([first, a[:n - SUBLANES, :]], axis=0)


def _channel_kernel(x_ref, mod_ref, g2_ref, w_up_ref, cw_ref, cb_ref, w_down_ref, gf_ref,
                    o_ref, conv_carry_ref, *, final_norm):
    ts, d = CHANNEL_TILE, x_ref.shape[2]
    d_ff = w_down_ref.shape[0]
    n_chunks = d_ff // FF_CHUNK

    @pl.when(pl.program_id(1) == 0)
    def _():
        conv_carry_ref[...] = jnp.zeros_like(conv_carry_ref)

    sh2 = mod_ref[0, 3:4, :]
    sc2 = mod_ref[0, 4:5, :]
    gt2 = mod_ref[0, 5:6, :]

    def conv_cols(pre, c0):
        cols = pl.ds(c0, FF_CHUNK)
        taps = [pre]
        for m in range(1, CONV_WIDTH):
            r = (CONV_WIDTH - 1 - m) * SUBLANES + SUBLANES - 1
            taps.append(_delay_rows(taps[-1], conv_carry_ref[r:r + 1, cols]))
        conv_carry_ref[:, cols] = pre[ts - CONV_CARRY_ROWS:, :]
        out = cb_ref[:, cols]
        for k in range(CONV_WIDTH):
            out = out + taps[CONV_WIDTH - 1 - k] * cw_ref[k:k + 1, cols]
        return out

    def token_tile(r0):
        x = x_ref[0, r0:r0 + ts, :]
        h2 = _rms_scale(x) * (g2_ref[...] * (1.0 + sc2)) + sh2
        h2b = h2.astype(BF16)

        def up_project(c):
            return [_dot(h2b, w_up_ref[:, pl.ds(c0, FF_CHUNK)]) for c0 in (c * FF_CHUNK, d_ff + c * FF_CHUNK)]

        ahead = [up_project(i) for i in range(min(UP_LOOKAHEAD, n_chunks))]
        acc = jnp.zeros((ts, d), F32)
        for c in range(n_chunks):
            cur = ahead.pop(0)
            if c + UP_LOOKAHEAD < n_chunks:
                ahead.append(up_project(c + UP_LOOKAHEAD))
            gate = conv_cols(cur[0], c * FF_CHUNK)
            val = conv_cols(cur[1], d_ff + c * FF_CHUNK)
            f = gate * jax.nn.sigmoid(gate) * val
            acc = acc + _dot_ref(f.astype(BF16), w_down_ref.at[c * FF_CHUNK:(c + 1) * FF_CHUNK, 0:d])
        x2 = x + gt2 * acc
        if final_norm:
            x2 = _rms_scale(x2) * gf_ref[...]
        o_ref[0, r0:r0 + ts, :] = _deinterleave_rows(x2)

    for r0 in range(0, x_ref.shape[1], ts):
        token_tile(r0)


def _channel_call(x, mod, g2, w_up, conv_w, conv_b, w_down, g_final, final_norm):
    bsz, s_len, d = x.shape
    ts = CHANNEL_TILE * CHANNEL_TILES_PER_STEP
    assert s_len % ts == 0 and CHANNEL_TILE >= CONV_WIDTH * SUBLANES and w_down.shape[0] % FF_CHUNK == 0
    row = lambda a: a.reshape(1, -1)
    operands = [
        (x, pl.BlockSpec((1, ts, d), lambda b, j: (b, j, 0))),
        (mod, pl.BlockSpec((1, N_ADA, d), lambda b, j: (b, 0, 0))),
    ]
    for a in (row(g2), w_up, conv_w, row(conv_b), w_down, row(g_final)):
        operands.append((a, _resident(a.shape)))
    return pl.pallas_call(
        functools.partial(_channel_kernel, final_norm=final_norm),
        grid=(bsz, s_len // ts),
        in_specs=[spec for _, spec in operands],
        out_specs=pl.BlockSpec((1, ts, d), lambda b, j: (b, j, 0)),
        out_shape=jax.ShapeDtypeStruct(x.shape, F32),
        scratch_shapes=[pltpu.VMEM((CONV_CARRY_ROWS, conv_w.shape[1]), F32)],
        compiler_params=pltpu.CompilerParams(
            dimension_semantics=("arbitrary", "arbitrary"), vmem_limit_bytes=VMEM_LIMIT_BYTES),
        name="channel",
    )(*[a for a, _ in operands])


def kernel(x, c, w_ada, b_ada, g_norm1, w_in, ln_v_g, ln_v_b, w_spatial, b_spatial, w_pool, b_pool,
           pool_scale, w_proj_a, w_proj_b, w_gate, b_gate, w_out, g_norm2, w_up, conv_w, conv_b,
           w_down, g_final):
    depth = w_ada.shape[0]
    bsz, s_len, d = x.shape
    assert w_ada.shape[2] % ADA_TILE == 0
    gd = w_in.shape[2] // 3 // A_GROUPS
    for l in range(depth):
        mod = _ada_call(c, w_ada[l], b_ada[l]).reshape(bsz, N_ADA, d)
        bias_map = jnp.repeat(b_spatial[l].T, gd, axis=1)
        x, w_up_bf, w_down_bf = _mixer_call(
            x, mod, g_norm1[l], w_in[l], w_gate[l], b_gate[l], ln_v_g[l], ln_v_b[l], w_spatial[l],
            bias_map, w_pool[l], b_pool[l], pool_scale[l], w_proj_a[l], w_proj_b[l], w_out[l],
            w_up[l], w_down[l])
        x = _channel_call(
            x, mod, g_norm2[l], w_up_bf, conv_w[l], conv_b[l], w_down_bf, g_final,
            final_norm=(l == depth - 1))
    return x
```

```python
import functools
import math

import jax
import jax.numpy as jnp
from jax import lax
from jax.experimental import pallas as pl
from jax.experimental.pallas import tpu as pltpu

EPS = 1e-6
CHUNK = 64
GMLP_BLOCK = 128
A_GROUPS = 8
POOL_WINDOWS = (2, 4, 8, 16)
CONV_WIDTH = 3
N_ADA = 6

SUBLANES = 8
LANES = 128
BF16_TILE_ROWS = 16
POOL_HALO = 16
CONV_CARRY_ROWS = (CONV_WIDTH - 1) * SUBLANES
MXU_WIDTH = 256
MIXER_TILE = 256
MIXER_TILES_PER_STEP = 4
MIXER_PREP_STEPS = 8
CHANNEL_TILE = 256
CHANNEL_TILES_PER_STEP = 2
FF_CHUNK = 256
UP_LOOKAHEAD = 5
ADA_TILE = 2048
VMEM_LIMIT_BYTES = 56 * 1024 * 1024

BF16 = jnp.bfloat16
F32 = jnp.float32


def _dot(a, b):
    return jnp.dot(a, b, preferred_element_type=F32)


def _dot_ref(a, w_ref):
    n = w_ref.shape[-1]
    assert n % MXU_WIDTH == 0
    return jnp.concatenate(
        [_dot(a, w_ref[:, c0:c0 + MXU_WIDTH]) for c0 in range(0, n, MXU_WIDTH)], axis=1)


def _gelu(x):
    return 0.5 * x * (1.0 + lax.erf(x * math.sqrt(0.5)))


def _rms_scale(x):
    ms = jnp.mean(x * x, axis=-1, keepdims=True)
    return x * lax.rsqrt(ms + EPS)


def _ada_kernel(c_ref, w_ref, b_ref, o_ref):
    c = c_ref[...]
    s = c * jax.nn.sigmoid(c)
    o_ref[...] = _dot(s.astype(BF16), w_ref[...].astype(BF16)) + b_ref[...]


def _ada_call(c, w_ada, b_ada):
    bsz, d = c.shape
    n = w_ada.shape[1]
    return pl.pallas_call(
        _ada_kernel,
        grid=(n // ADA_TILE,),
        in_specs=[
            pl.BlockSpec((bsz, d), lambda j: (0, 0)),
            pl.BlockSpec((d, ADA_TILE), lambda j: (0, j)),
            pl.BlockSpec((1, ADA_TILE), lambda j: (0, j)),
        ],
        out_specs=pl.BlockSpec((bsz, ADA_TILE), lambda j: (0, j)),
        out_shape=jax.ShapeDtypeStruct((bsz, n), F32),
        compiler_params=pltpu.CompilerParams(dimension_semantics=("arbitrary",)),
        name="ada_mod",
    )(c, w_ada, b_ada.reshape(1, n))


def _cast_block(src_ref, dst_ref):
    n = src_ref.shape[1]
    dst_ref[:, 0:n] = src_ref[...].astype(BF16)
    if dst_ref.shape[1] > n:
        dst_ref[:, n:] = jnp.zeros((dst_ref.shape[0], dst_ref.shape[1] - n), BF16)


def _mixer_kernel(x_ref, mod_ref, w_in_f32, w_gate_f32, wpa_f32, wpb_f32, wout_f32, w_up_ref, w_down_ref,
                  wsp_ref, bsp_ref, wpool_ref, vec_ref, o_ref, w_up_bf_ref, w_down_bf_ref, w_ref,
                  pool_carry_ref, *, n_prep, n_tiles):
    step = pl.program_id(0)

    @pl.when(step < n_prep)
    def _():
        rows = w_in_f32.shape[0]
        r0 = pl.multiple_of(step * rows, rows)
        off = 0
        for src in (w_in_f32, w_gate_f32, wpa_f32, wpb_f32, wout_f32):
            w_ref[pl.ds(r0, rows), off:off + src.shape[1]] = src[...].astype(BF16)
            off += src.shape[1]

    @pl.when(step >= n_prep)
    def _():
        _mixer_tokens(x_ref, mod_ref, w_up_ref, w_down_ref, wsp_ref, bsp_ref, wpool_ref, vec_ref, o_ref,
                      w_up_bf_ref, w_down_bf_ref, w_ref, pool_carry_ref,
                      in_width=w_in_f32.shape[1], j=(step - n_prep) % n_tiles)


def _mixer_tokens(x_ref, mod_ref, w_up_ref, w_down_ref, wsp_ref, bsp_ref, wpool_ref, vec_ref, o_ref,
                  w_up_bf_ref, w_down_bf_ref, w_ref, pool_carry_ref, *, in_width, j):
    ts, d = MIXER_TILE, x_ref.shape[2]
    g1_ref, lng_ref, lnb_ref, pscale_ref, bpool_ref = (vec_ref.at[r:r + 1, :] for r in range(5))
    b_gate = jnp.concatenate([vec_ref[5:6, :], vec_ref[6:7, :]], axis=1)
    _cast_block(w_up_ref, w_up_bf_ref)
    _cast_block(w_down_ref, w_down_bf_ref)
    w_in_ref = w_ref.at[:, 0:in_width]
    w_gate_ref = w_ref.at[:, in_width:in_width + 2 * d]
    wpa_ref = w_ref.at[:, in_width + 2 * d:in_width + 3 * d]
    wpb_ref = w_ref.at[:, in_width + 3 * d:in_width + 4 * d]
    wout_ref = w_ref.at[:, in_width + 4 * d:in_width + 5 * d]

    @pl.when(j == 0)
    def _():
        pool_carry_ref[...] = jnp.zeros_like(pool_carry_ref)

    sh1 = mod_ref[0, 0:1, :]
    sc1 = mod_ref[0, 1:2, :]
    gt1 = mod_ref[0, 2:3, :]
    a_w = d
    gd = a_w // A_GROUPS
    p = lax.broadcasted_iota(jnp.int32, (GMLP_BLOCK, GMLP_BLOCK), 0)
    q = lax.broadcasted_iota(jnp.int32, (GMLP_BLOCK, GMLP_BLOCK), 1)
    allowed = (q // CHUNK) <= (p // CHUNK)
    w_masked = [jnp.where(allowed, wsp_ref[g], 0.0).astype(BF16) for g in range(A_GROUPS)]

    def token_tile(row0):
        x = x_ref[0, row0:row0 + ts, :]
        h = _rms_scale(x) * (g1_ref[...] * (1.0 + sc1)) + sh1
        hb = h.astype(BF16)

        v = _gelu(_dot_ref(hb, w_in_ref.at[:, a_w:2 * a_w]))
        hbp = _dot_ref(hb, w_in_ref.at[:, 2 * a_w:w_in_ref.shape[1]])
        gates = jax.nn.sigmoid(_dot_ref(hb, w_gate_ref) + b_gate)
        u = _gelu(_dot_ref(hb, w_in_ref.at[:, 0:a_w]))
        mu = jnp.mean(v, axis=-1, keepdims=True)
        vc = v - mu
        var = jnp.mean(vc * vc, axis=-1, keepdims=True)
        vn = (vc * lax.rsqrt(var + EPS)) * lng_ref[...] + lnb_ref[...]
        vnb = vn.astype(BF16)

        ext = jnp.concatenate([pool_carry_ref[...], hbp], axis=0)
        pool_carry_ref[...] = hbp[ts - POOL_HALO:, :]
        bgd = hbp.shape[1] // len(POOL_WINDOWS)
        t = j * x_ref.shape[1] + row0 + lax.broadcasted_iota(jnp.int32, (ts, bgd), 0)
        yb_cols = []
        for gi, w in enumerate(POOL_WINDOWS):
            e = ext[:, gi * bgd:(gi + 1) * bgd]
            acc, span = e, 1
            while span < w:
                acc = acc + pltpu.roll(acc, span, 0)
                span *= 2
            win = acc[POOL_HALO:, :]
            cur = e[POOL_HALO:, :]
            count = jnp.minimum(t + 1, w).astype(F32)
            pooled = win / count - cur
            mixed = _dot(pooled.astype(BF16), wpool_ref[gi]) + bpool_ref[:, gi * bgd:(gi + 1) * bgd]
            yb_cols.append(mixed)
        y_b = jnp.concatenate(yb_cols, axis=1) * pscale_ref[...]
        branch_b = gates[:, d:] * _dot_ref(y_b.astype(BF16), wpb_ref)

        bias_map = bsp_ref[...]
        s_rows = []
        for n in range(ts // GMLP_BLOCK):
            b0 = n * GMLP_BLOCK
            cols = [_dot(w_masked[g], vnb[b0:b0 + GMLP_BLOCK, g * gd:(g + 1) * gd]) for g in range(A_GROUPS)]
            s_rows.append(jnp.concatenate(cols, axis=1) + bias_map)
        s = jnp.concatenate(s_rows, axis=0)
        y_a = u * s

        merged = gates[:, :d] * _dot_ref(y_a.astype(BF16), wpa_ref) + branch_b
        x1 = x + gt1 * _dot_ref(merged.astype(BF16), wout_ref)
        o_ref[0, row0:row0 + ts, :] = jnp.concatenate(
            [_interleave_rows(x1[r:r + CHANNEL_TILE, :]) for r in range(0, ts, CHANNEL_TILE)], axis=0)

    for row0 in range(0, x_ref.shape[1], ts):
        token_tile(row0)


def _resident(shape):
    nd = len(shape)
    return pl.BlockSpec(shape, lambda b, j: (0,) * nd, pipeline_mode=pl.Buffered(1))


def _pitch_pad(n_cols):
    return LANES if (n_cols // LANES) % SUBLANES == 0 else 0


def _row_blocking(n_rows, n_steps):
    for steps_per_block in range(1, n_steps + 1):
        if n_steps % steps_per_block == 0 and (n_rows * steps_per_block) % n_steps == 0:
            rows = n_rows * steps_per_block // n_steps
            if rows % BF16_TILE_ROWS == 0:
                return rows, steps_per_block
    raise ValueError(f"no bf16-aligned row blocking of {n_rows} rows over {n_steps} steps")


def _mixer_call(x, mod, g1, w_in, w_gate, b_gate, ln_g, ln_b, w_sp, bias_map, w_pool, b_pool,
                pool_scale, w_pa, w_pb, w_out, w_up, w_down):
    bsz, s_len, d = x.shape
    ts = MIXER_TILE * MIXER_TILES_PER_STEP
    n_tiles = s_len // ts
    n_prep = MIXER_PREP_STEPS
    assert s_len % ts == 0 and MIXER_TILE % GMLP_BLOCK == 0 and MIXER_TILE >= POOL_HALO
    assert MIXER_TILE % CHANNEL_TILE == 0
    assert w_gate.shape[1] == 2 * d and w_pa.shape == w_pb.shape == w_out.shape == (d, d)
    assert d % (n_prep * BF16_TILE_ROWS) == 0
    row = lambda a: a.reshape(1, -1)

    def token_step(t):
        return jnp.maximum(t - n_prep, 0)

    def token_block(t):
        return (token_step(t) // n_tiles, token_step(t) % n_tiles, 0)

    def resident(a):
        nd = a.ndim
        return pl.BlockSpec(a.shape, lambda t: (0,) * nd, pipeline_mode=pl.Buffered(1))

    operands = [
        (x, pl.BlockSpec((1, ts, d), token_block)),
        (mod, pl.BlockSpec((1, N_ADA, d), lambda t: (token_step(t) // n_tiles, 0, 0))),
    ]
    mixer_weights = (w_in, w_gate, w_pa, w_pb, w_out)
    for w in mixer_weights:
        operands.append((w, pl.BlockSpec((d // n_prep, w.shape[1]), lambda t: (jnp.minimum(t, n_prep - 1), 0))))
    out_specs = [pl.BlockSpec((1, ts, d), token_block)]
    out_shapes = [jax.ShapeDtypeStruct(x.shape, F32)]
    for w in (w_up, w_down):
        rows, steps_per_block = _row_blocking(w.shape[0], bsz * n_tiles)
        index_map = lambda t, spb=steps_per_block: (token_step(t) // spb, 0)
        operands.append((w, pl.BlockSpec((rows, w.shape[1]), index_map)))
        padded = w.shape[1] + _pitch_pad(w.shape[1])
        out_specs.append(pl.BlockSpec((rows, padded), index_map))
        out_shapes.append(jax.ShapeDtypeStruct((w.shape[0], padded), BF16))
    assert b_pool.size == d and b_gate.size == 2 * d
    vec = jnp.concatenate([row(g1), row(ln_g), row(ln_b), row(pool_scale), row(b_pool),
                           b_gate.reshape(2, d), jnp.zeros((1, d), F32)], axis=0)
    for a in (w_sp, bias_map, w_pool.astype(BF16), vec):
        operands.append((a, resident(a)))
    packed_width = sum(w.shape[1] for w in mixer_weights)
    packed_width += _pitch_pad(packed_width)
    return pl.pallas_call(
        functools.partial(_mixer_kernel, n_prep=n_prep, n_tiles=n_tiles),
        grid=(n_prep + bsz * n_tiles,),
        in_specs=[spec for _, spec in operands],
        out_specs=out_specs,
        out_shape=out_shapes,
        scratch_shapes=[pltpu.VMEM((d, packed_width), BF16),
                        pltpu.VMEM((POOL_HALO, w_pool.shape[0] * w_pool.shape[1]), F32)],
        compiler_params=pltpu.CompilerParams(
            dimension_semantics=("arbitrary",), vmem_limit_bytes=VMEM_LIMIT_BYTES),
        name="mixer",
    )(*[a for a, _ in operands])


def _interleave_rows(x):
    n, d = x.shape
    return jnp.swapaxes(x.reshape(SUBLANES, n // SUBLANES, d), 0, 1).reshape(n, d)


def _deinterleave_rows(y):
    n, d = y.shape
    return jnp.swapaxes(y.reshape(n // SUBLANES, SUBLANES, d), 0, 1).reshape(n, d)


def _delay_rows(a, prev_row):
    n = a.shape[0]
    wrapped = pltpu.roll(a[n - SUBLANES:, :], 1, 0)
    first = jnp.where(lax.broadcasted_iota(jnp.int32, wrapped.shape, 0) == 0, prev_row, wrapped)
    return jnp.concatenate([first, a[:n - SUBLANES, :]], axis=0)


def _channel_kernel(x_ref, mod_ref, w_up_ref, w_down_ref, conv_ref, vec_ref,
                    o_ref, conv_carry_ref, *, final_norm):
    cw_ref, cb_ref = conv_ref.at[0:CONV_WIDTH, :], conv_ref.at[CONV_WIDTH:CONV_WIDTH + 1, :]
    g2_ref, gf_ref = vec_ref.at[0:1, :], vec_ref.at[1:2, :]
    ts, d = CHANNEL_TILE, x_ref.shape[2]
    d_ff = w_down_ref.shape[0]
    n_chunks = d_ff // FF_CHUNK

    @pl.when(pl.program_id(1) == 0)
    def _():
        conv_carry_ref[...] = jnp.zeros_like(conv_carry_ref)

    sh2 = mod_ref[0, 3:4, :]
    sc2 = mod_ref[0, 4:5, :]
    gt2 = mod_ref[0, 5:6, :]

    def conv_cols(pre, c0):
        cols = pl.ds(c0, FF_CHUNK)
        taps = [pre]
        for m in range(1, CONV_WIDTH):
            r = (CONV_WIDTH - 1 - m) * SUBLANES + SUBLANES - 1
            taps.append(_delay_rows(taps[-1], conv_carry_ref[r:r + 1, cols]))
        conv_carry_ref[:, cols] = pre[ts - CONV_CARRY_ROWS:, :]
        out = cb_ref[:, cols]
        for k in range(CONV_WIDTH):
            out = out + taps[CONV_WIDTH - 1 - k] * cw_ref[k:k + 1, cols]
        return out

    def token_tile(r0):
        x = x_ref[0, r0:r0 + ts, :]
        h2 = _rms_scale(x) * (g2_ref[...] * (1.0 + sc2)) + sh2
        h2b = h2.astype(BF16)

        def up_project(c):
            return [_dot(h2b, w_up_ref[:, pl.ds(c0, FF_CHUNK)]) for c0 in (c * FF_CHUNK, d_ff + c * FF_CHUNK)]

        ahead = [up_project(i) for i in range(min(UP_LOOKAHEAD, n_chunks))]
        acc = jnp.zeros((ts, d), F32)
        for c in range(n_chunks):
            cur = ahead.pop(0)
            if c + UP_LOOKAHEAD < n_chunks:
                ahead.append(up_project(c + UP_LOOKAHEAD))
            gate = conv_cols(cur[0], c * FF_CHUNK)
            val = conv_cols(cur[1], d_ff + c * FF_CHUNK)
            f = gate * jax.nn.sigmoid(gate) * val
            acc = acc + _dot_ref(f.astype(BF16), w_down_ref.at[c * FF_CHUNK:(c + 1) * FF_CHUNK, 0:d])
        x2 = x + gt2 * acc
        if final_norm:
            x2 = _rms_scale(x2) * gf_ref[...]
        o_ref[0, r0:r0 + ts, :] = _deinterleave_rows(x2)

    for r0 in range(0, x_ref.shape[1], ts):
        token_tile(r0)


def _channel_call(x, mod, g2, w_up, conv_w, conv_b, w_down, g_final, final_norm):
    bsz, s_len, d = x.shape
    ts = CHANNEL_TILE * CHANNEL_TILES_PER_STEP
    assert s_len % ts == 0 and CHANNEL_TILE >= CONV_WIDTH * SUBLANES and w_down.shape[0] % FF_CHUNK == 0
    row = lambda a: a.reshape(1, -1)
    operands = [
        (x, pl.BlockSpec((1, ts, d), lambda b, j: (b, j, 0))),
        (mod, pl.BlockSpec((1, N_ADA, d), lambda b, j: (b, 0, 0))),
    ]
    conv = jnp.concatenate(
        [conv_w, row(conv_b), jnp.zeros((SUBLANES - CONV_WIDTH - 1, conv_w.shape[1]), F32)], axis=0)
    vec = jnp.concatenate([row(g2), row(g_final), jnp.zeros((SUBLANES - 2, d), F32)], axis=0)
    for a in (w_up, w_down, conv, vec):
        operands.append((a, _resident(a.shape)))
    return pl.pallas_call(
        functools.partial(_channel_kernel, final_norm=final_norm),
        grid=(bsz, s_len // ts),
        in_specs=[spec for _, spec in operands],
        out_specs=pl.BlockSpec((1, ts, d), lambda b, j: (b, j, 0)),
        out_shape=jax.ShapeDtypeStruct(x.shape, F32),
        scratch_shapes=[pltpu.VMEM((CONV_CARRY_ROWS, conv_w.shape[1]), F32)],
        compiler_params=pltpu.CompilerParams(
            dimension_semantics=("arbitrary", "arbitrary"), vmem_limit_bytes=VMEM_LIMIT_BYTES),
        name="channel",
    )(*[a for a, _ in operands])


def kernel(x, c, w_ada, b_ada, g_norm1, w_in, ln_v_g, ln_v_b, w_spatial, b_spatial, w_pool, b_pool,
           pool_scale, w_proj_a, w_proj_b, w_gate, b_gate, w_out, g_norm2, w_up, conv_w, conv_b,
           w_down, g_final):
    depth = w_ada.shape[0]
    bsz, s_len, d = x.shape
    assert w_ada.shape[2] % ADA_TILE == 0
    gd = w_in.shape[2] // 3 // A_GROUPS
    for l in range(depth):
        mod = _ada_call(c, w_ada[l], b_ada[l]).reshape(bsz, N_ADA, d)
        bias_map = jnp.repeat(b_spatial[l].T, gd, axis=1)
        x, w_up_bf, w_down_bf = _mixer_call(
            x, mod, g_norm1[l], w_in[l], w_gate[l], b_gate[l], ln_v_g[l], ln_v_b[l], w_spatial[l],
            bias_map, w_pool[l], b_pool[l], pool_scale[l], w_proj_a[l], w_proj_b[l], w_out[l],
            w_up[l], w_down[l])
        x = _channel_call(
            x, mod, g_norm2[l], w_up_bf, conv_w[l], conv_b[l], w_down_bf, g_final,
            final_norm=(l == depth - 1))
    return x
```

```python
import functools
import math

import jax
import jax.numpy as jnp
from jax import lax
from jax.experimental import pallas as pl
from jax.experimental.pallas import tpu as pltpu

EPS = 1e-6
CHUNK = 64
GMLP_BLOCK = 128
A_GROUPS = 8
POOL_WINDOWS = (2, 4, 8, 16)
CONV_WIDTH = 3
N_ADA = 6

SUBLANES = 8
LANES = 128
BF16_TILE_ROWS = 16
POOL_HALO = 16
CONV_CARRY_ROWS = (CONV_WIDTH - 1) * SUBLANES
MXU_WIDTH = 256
MIXER_TILE = 256
MIXER_TILES_PER_STEP = 4
MIXER_PREP_STEPS = 8
CHANNEL_TILE = 256
CHANNEL_TILES_PER_STEP = 4
FF_CHUNK = 256
UP_LOOKAHEAD = 5
ADA_TILE = 2048
VMEM_LIMIT_BYTES = 56 * 1024 * 1024

BF16 = jnp.bfloat16
F32 = jnp.float32


def _dot(a, b):
    return jnp.dot(a, b, preferred_element_type=F32)


def _dot_ref(a, w_ref):
    n = w_ref.shape[-1]
    assert n % MXU_WIDTH == 0
    return jnp.concatenate(
        [_dot(a, w_ref[:, c0:c0 + MXU_WIDTH]) for c0 in range(0, n, MXU_WIDTH)], axis=1)


def _gelu(x):
    return 0.5 * x * (1.0 + lax.erf(x * math.sqrt(0.5)))


def _rms_scale(x):
    ms = jnp.mean(x * x, axis=-1, keepdims=True)
    return x * lax.rsqrt(ms + EPS)


def _ada_kernel(c_ref, w_ref, b_ref, o_ref):
    c = c_ref[...]
    s = c * jax.nn.sigmoid(c)
    o_ref[...] = _dot(s.astype(BF16), w_ref[...].astype(BF16)) + b_ref[...]


def _ada_call(c, w_ada, b_ada):
    bsz, d = c.shape
    n = w_ada.shape[1]
    return pl.pallas_call(
        _ada_kernel,
        grid=(n // ADA_TILE,),
        in_specs=[
            pl.BlockSpec((bsz, d), lambda j: (0, 0)),
            pl.BlockSpec((d, ADA_TILE), lambda j: (0, j)),
            pl.BlockSpec((1, ADA_TILE), lambda j: (0, j)),
        ],
        out_specs=pl.BlockSpec((bsz, ADA_TILE), lambda j: (0, j)),
        out_shape=jax.ShapeDtypeStruct((bsz, n), F32),
        compiler_params=pltpu.CompilerParams(dimension_semantics=("arbitrary",)),
        name="ada_mod",
    )(c, w_ada, b_ada.reshape(1, n))


def _cast_block(src_ref, dst_ref):
    n = src_ref.shape[1]
    dst_ref[:, 0:n] = src_ref[...].astype(BF16)
    if dst_ref.shape[1] > n:
        dst_ref[:, n:] = jnp.zeros((dst_ref.shape[0], dst_ref.shape[1] - n), BF16)


def _mixer_kernel(x_ref, mod_ref, g1_ref, b_gate_ref, lng_ref, lnb_ref, wsp_ref, bsp_ref, wpool_ref,
                  bpool_ref, pscale_ref, w_in_f32, w_gate_f32, wpa_f32, wpb_f32, wout_f32,
                  w_up_ref, w_down_ref, o_ref, w_up_bf_ref, w_down_bf_ref, w_ref, pool_carry_ref,
                  *, n_prep, n_tiles):
    step = pl.program_id(0)

    @pl.when(step < n_prep)
    def _():
        rows = w_in_f32.shape[0]
        r0 = pl.multiple_of(step * rows, rows)
        off = 0
        for src in (w_in_f32, w_gate_f32, wpa_f32, wpb_f32, wout_f32):
            w_ref[pl.ds(r0, rows), off:off + src.shape[1]] = src[...].astype(BF16)
            off += src.shape[1]

    @pl.when(step >= n_prep)
    def _():
        _mixer_tokens(x_ref, mod_ref, g1_ref, b_gate_ref, lng_ref, lnb_ref, wsp_ref, bsp_ref, wpool_ref,
                      bpool_ref, pscale_ref, w_up_ref, w_down_ref, o_ref, w_up_bf_ref, w_down_bf_ref,
                      w_ref, pool_carry_ref, in_width=w_in_f32.shape[1], j=(step - n_prep) % n_tiles)


def _mixer_tokens(x_ref, mod_ref, g1_ref, b_gate_ref, lng_ref, lnb_ref, wsp_ref, bsp_ref, wpool_ref,
                  bpool_ref, pscale_ref, w_up_ref, w_down_ref, o_ref, w_up_bf_ref, w_down_bf_ref,
                  w_ref, pool_carry_ref, *, in_width, j):
    ts, d = MIXER_TILE, x_ref.shape[2]
    _cast_block(w_up_ref, w_up_bf_ref)
    _cast_block(w_down_ref, w_down_bf_ref)
    w_in_ref = w_ref.at[:, 0:in_width]
    w_gate_ref = w_ref.at[:, in_width:in_width + 2 * d]
    wpa_ref = w_ref.at[:, in_width + 2 * d:in_width + 3 * d]
    wpb_ref = w_ref.at[:, in_width + 3 * d:in_width + 4 * d]
    wout_ref = w_ref.at[:, in_width + 4 * d:in_width + 5 * d]

    @pl.when(j == 0)
    def _():
        pool_carry_ref[...] = jnp.zeros_like(pool_carry_ref)

    sh1 = mod_ref[0, 0:1, :]
    sc1 = mod_ref[0, 1:2, :]
    gt1 = mod_ref[0, 2:3, :]
    a_w = d
    gd = a_w // A_GROUPS
    p = lax.broadcasted_iota(jnp.int32, (GMLP_BLOCK, GMLP_BLOCK), 0)
    q = lax.broadcasted_iota(jnp.int32, (GMLP_BLOCK, GMLP_BLOCK), 1)
    allowed = (q // CHUNK) <= (p // CHUNK)
    w_masked = [jnp.where(allowed, wsp_ref[g], 0.0).astype(BF16) for g in range(A_GROUPS)]

    def token_tile(row0):
        x = x_ref[0, row0:row0 + ts, :]
        h = _rms_scale(x) * (g1_ref[...] * (1.0 + sc1)) + sh1
        hb = h.astype(BF16)

        v = _gelu(_dot_ref(hb, w_in_ref.at[:, a_w:2 * a_w]))
        hbp = _dot_ref(hb, w_in_ref.at[:, 2 * a_w:w_in_ref.shape[1]])
        gates = jax.nn.sigmoid(_dot_ref(hb, w_gate_ref) + b_gate_ref[...])
        u = _gelu(_dot_ref(hb, w_in_ref.at[:, 0:a_w]))
        mu = jnp.mean(v, axis=-1, keepdims=True)
        vc = v - mu
        var = jnp.mean(vc * vc, axis=-1, keepdims=True)
        vn = (vc * lax.rsqrt(var + EPS)) * lng_ref[...] + lnb_ref[...]
        vnb = vn.astype(BF16)

        ext = jnp.concatenate([pool_carry_ref[...], hbp], axis=0)
        pool_carry_ref[...] = hbp[ts - POOL_HALO:, :]
        bgd = hbp.shape[1] // len(POOL_WINDOWS)
        t = j * x_ref.shape[1] + row0 + lax.broadcasted_iota(jnp.int32, (ts, bgd), 0)
        yb_cols = []
        for gi, w in enumerate(POOL_WINDOWS):
            e = ext[:, gi * bgd:(gi + 1) * bgd]
            acc, span = e, 1
            while span < w:
                acc = acc + pltpu.roll(acc, span, 0)
                span *= 2
            win = acc[POOL_HALO:, :]
            cur = e[POOL_HALO:, :]
            count = jnp.minimum(t + 1, w).astype(F32)
            pooled = win / count - cur
            mixed = _dot(pooled.astype(BF16), wpool_ref[gi]) + bpool_ref[gi:gi + 1, :]
            yb_cols.append(mixed)
        y_b = jnp.concatenate(yb_cols, axis=1) * pscale_ref[...]
        branch_b = gates[:, d:] * _dot_ref(y_b.astype(BF16), wpb_ref)

        bias_map = bsp_ref[...]
        s_rows = []
        for n in range(ts // GMLP_BLOCK):
            b0 = n * GMLP_BLOCK
            cols = [_dot(w_masked[g], vnb[b0:b0 + GMLP_BLOCK, g * gd:(g + 1) * gd]) for g in range(A_GROUPS)]
            s_rows.append(jnp.concatenate(cols, axis=1) + bias_map)
        s = jnp.concatenate(s_rows, axis=0)
        y_a = u * s

        merged = gates[:, :d] * _dot_ref(y_a.astype(BF16), wpa_ref) + branch_b
        x1 = x + gt1 * _dot_ref(merged.astype(BF16), wout_ref)
        o_ref[0, row0:row0 + ts, :] = jnp.concatenate(
            [_interleave_rows(x1[r:r + CHANNEL_TILE, :]) for r in range(0, ts, CHANNEL_TILE)], axis=0)

    for row0 in range(0, x_ref.shape[1], ts):
        token_tile(row0)


def _resident(shape):
    nd = len(shape)
    return pl.BlockSpec(shape, lambda b, j: (0,) * nd, pipeline_mode=pl.Buffered(1))


def _pitch_pad(n_cols):
    return LANES if (n_cols // LANES) % SUBLANES == 0 else 0


def _row_blocking(n_rows, n_steps):
    for steps_per_block in range(1, n_steps + 1):
        if n_steps % steps_per_block == 0 and (n_rows * steps_per_block) % n_steps == 0:
            rows = n_rows * steps_per_block // n_steps
            if rows % BF16_TILE_ROWS == 0:
                return rows, steps_per_block
    raise ValueError(f"no bf16-aligned row blocking of {n_rows} rows over {n_steps} steps")


def _mixer_call(x, mod, g1, w_in, w_gate, b_gate, ln_g, ln_b, w_sp, bias_map, w_pool, b_pool,
                pool_scale, w_pa, w_pb, w_out, w_up, w_down):
    bsz, s_len, d = x.shape
    ts = MIXER_TILE * MIXER_TILES_PER_STEP
    n_tiles = s_len // ts
    n_prep = MIXER_PREP_STEPS
    assert s_len % ts == 0 and MIXER_TILE % GMLP_BLOCK == 0 and MIXER_TILE >= POOL_HALO
    assert MIXER_TILE % CHANNEL_TILE == 0
    assert w_gate.shape[1] == 2 * d and w_pa.shape == w_pb.shape == w_out.shape == (d, d)
    assert d % (n_prep * BF16_TILE_ROWS) == 0
    row = lambda a: a.reshape(1, -1)

    def token_step(t):
        return jnp.maximum(t - n_prep, 0)

    def token_block(t):
        return (token_step(t) // n_tiles, token_step(t) % n_tiles, 0)

    def resident(a):
        nd = a.ndim
        return pl.BlockSpec(a.shape, lambda t: (0,) * nd, pipeline_mode=pl.Buffered(1))

    operands = [
        (x, pl.BlockSpec((1, ts, d), token_block)),
        (mod, pl.BlockSpec((1, N_ADA, d), lambda t: (token_step(t) // n_tiles, 0, 0))),
    ]
    for a in (row(g1), row(b_gate), row(ln_g), row(ln_b), w_sp, bias_map, w_pool.astype(BF16), b_pool,
              row(pool_scale)):
        operands.append((a, resident(a)))
    mixer_weights = (w_in, w_gate, w_pa, w_pb, w_out)
    for w in mixer_weights:
        operands.append((w, pl.BlockSpec((d // n_prep, w.shape[1]), lambda t: (jnp.minimum(t, n_prep - 1), 0))))
    out_specs = [pl.BlockSpec((1, ts, d), token_block)]
    out_shapes = [jax.ShapeDtypeStruct(x.shape, F32)]
    for w in (w_up, w_down):
        rows, steps_per_block = _row_blocking(w.shape[0], bsz * n_tiles)
        index_map = lambda t, spb=steps_per_block: (token_step(t) // spb, 0)
        operands.append((w, pl.BlockSpec((rows, w.shape[1]), index_map)))
        padded = w.shape[1] + _pitch_pad(w.shape[1])
        out_specs.append(pl.BlockSpec((rows, padded), index_map))
        out_shapes.append(jax.ShapeDtypeStruct((w.shape[0], padded), BF16))
    packed_width = sum(w.shape[1] for w in mixer_weights)
    packed_width += _pitch_pad(packed_width)
    return pl.pallas_call(
        functools.partial(_mixer_kernel, n_prep=n_prep, n_tiles=n_tiles),
        grid=(n_prep + bsz * n_tiles,),
        in_specs=[spec for _, spec in operands],
        out_specs=out_specs,
        out_shape=out_shapes,
        scratch_shapes=[pltpu.VMEM((d, packed_width), BF16),
                        pltpu.VMEM((POOL_HALO, w_pool.shape[0] * w_pool.shape[1]), F32)],
        compiler_params=pltpu.CompilerParams(
            dimension_semantics=("arbitrary",), vmem_limit_bytes=VMEM_LIMIT_BYTES),
        name="mixer",
    )(*[a for a, _ in operands])


def _interleave_rows(x):
    n, d = x.shape
    return jnp.swapaxes(x.reshape(SUBLANES, n // SUBLANES, d), 0, 1).reshape(n, d)


def _deinterleave_rows(y):
    n, d = y.shape
    return jnp.swapaxes(y.reshape(n // SUBLANES, SUBLANES, d), 0, 1).reshape(n, d)


def _delay_rows(a, prev_row):
    n = a.shape[0]
    wrapped = pltpu.roll(a[n - SUBLANES:, :], 1, 0)
    first = jnp.where(lax.broadcasted_iota(jnp.int32, wrapped.shape, 0) == 0, prev_row, wrapped)
    return jnp.concatenate([first, a[:n - SUBLANES, :]], axis=0)


def _channel_kernel(x_ref, mod_ref, g2_ref, w_up_ref, cw_ref, cb_ref, w_down_ref, gf_ref,
                    o_ref, conv_carry_ref, *, final_norm):
    ts, d = CHANNEL_TILE, x_ref.shape[2]
    d_ff = w_down_ref.shape[0]
    n_chunks = d_ff // FF_CHUNK

    @pl.when(pl.program_id(1) == 0)
    def _():
        conv_carry_ref[...] = jnp.zeros_like(conv_carry_ref)

    sh2 = mod_ref[0, 3:4, :]
    sc2 = mod_ref[0, 4:5, :]
    gt2 = mod_ref[0, 5:6, :]

    def conv_cols(pre, c0):
        cols = pl.ds(c0, FF_CHUNK)
        taps = [pre]
        for m in range(1, CONV_WIDTH):
            r = (CONV_WIDTH - 1 - m) * SUBLANES + SUBLANES - 1
            taps.append(_delay_rows(taps[-1], conv_carry_ref[r:r + 1, cols]))
        conv_carry_ref[:, cols] = pre[ts - CONV_CARRY_ROWS:, :]
        out = cb_ref[:, cols]
        for k in range(CONV_WIDTH):
            out = out + taps[CONV_WIDTH - 1 - k] * cw_ref[k:k + 1, cols]
        return out

    def token_tile(r0):
        x = x_ref[0, r0:r0 + ts, :]
        h2 = _rms_scale(x) * (g2_ref[...] * (1.0 + sc2)) + sh2
        h2b = h2.astype(BF16)

        def up_project(c):
            return [_dot(h2b, w_up_ref[:, pl.ds(c0, FF_CHUNK)]) for c0 in (c * FF_CHUNK, d_ff + c * FF_CHUNK)]

        ahead = [up_project(i) for i in range(min(UP_LOOKAHEAD, n_chunks))]
        acc = jnp.zeros((ts, d), F32)
        for c in range(n_chunks):
            cur = ahead.pop(0)
            if c + UP_LOOKAHEAD < n_chunks:
                ahead.append(up_project(c + UP_LOOKAHEAD))
            gate = conv_cols(cur[0], c * FF_CHUNK)
            val = conv_cols(cur[1], d_ff + c * FF_CHUNK)
            f = gate * jax.nn.sigmoid(gate) * val
            acc = acc + _dot_ref(f.astype(BF16), w_down_ref.at[c * FF_CHUNK:(c + 1) * FF_CHUNK, 0:d])
        x2 = x + gt2 * acc
        if final_norm:
            x2 = _rms_scale(x2) * gf_ref[...]
        o_ref[0, r0:r0 + ts, :] = _deinterleave_rows(x2)

    for r0 in range(0, x_ref.shape[1], ts):
        token_tile(r0)


def _channel_call(x, mod, g2, w_up, conv_w, conv_b, w_down, g_final, final_norm):
    bsz, s_len, d = x.shape
    ts = CHANNEL_TILE * CHANNEL_TILES_PER_STEP
    assert s_len % ts == 0 and CHANNEL_TILE >= CONV_WIDTH * SUBLANES and w_down.shape[0] % FF_CHUNK == 0
    row = lambda a: a.reshape(1, -1)
    operands = [
        (x, pl.BlockSpec((1, ts, d), lambda b, j: (b, j, 0))),
        (mod, pl.BlockSpec((1, N_ADA, d), lambda b, j: (b, 0, 0))),
    ]
    for a in (row(g2), w_up, conv_w, row(conv_b), w_down, row(g_final)):
        operands.append((a, _resident(a.shape)))
    return pl.pallas_call(
        functools.partial(_channel_kernel, final_norm=final_norm),
        grid=(bsz, s_len // ts),
        in_specs=[spec for _, spec in operands],
        out_specs=pl.BlockSpec((1, ts, d), lambda b, j: (b, j, 0)),
        out_shape=jax.ShapeDtypeStruct(x.shape, F32),
        scratch_shapes=[pltpu.VMEM((CONV_CARRY_ROWS, conv_w.shape[1]), F32)],
        compiler_params=pltpu.CompilerParams(
            dimension_semantics=("arbitrary", "arbitrary"), vmem_limit_bytes=VMEM_LIMIT_BYTES),
        name="channel",
    )(*[a for a, _ in operands])


def kernel(x, c, w_ada, b_ada, g_norm1, w_in, ln_v_g, ln_v_b, w_spatial, b_spatial, w_pool, b_pool,
           pool_scale, w_proj_a, w_proj_b, w_gate, b_gate, w_out, g_norm2, w_up, conv_w, conv_b,
           w_down, g_final):
    depth = w_ada.shape[0]
    bsz, s_len, d = x.shape
    assert w_ada.shape[2] % ADA_TILE == 0
    gd = w_in.shape[2] // 3 // A_GROUPS
    for l in range(depth):
        mod = _ada_call(c, w_ada[l], b_ada[l]).reshape(bsz, N_ADA, d)
        bias_map = jnp.repeat(b_spatial[l].T, gd, axis=1)
        x, w_up_bf, w_down_bf = _mixer_call(
            x, mod, g_norm1[l], w_in[l], w_gate[l], b_gate[l], ln_v_g[l], ln_v_b[l], w_spatial[l],
            bias_map, w_pool[l], b_pool[l], pool_scale[l], w_proj_a[l], w_proj_b[l], w_out[l],
            w_up[l], w_down[l])
        x = _channel_call(
            x, mod, g_norm2[l], w_up_bf, conv_w[l], conv_b[l], w_down_bf, g_final,
            final_norm=(l == depth - 1))
    return x
```

```python
import functools
import math

import jax
import jax.numpy as jnp
from jax import lax
from jax.experimental import pallas as pl
from jax.experimental.pallas import tpu as pltpu

EPS = 1e-6
CHUNK = 64
GMLP_BLOCK = 128
A_GROUPS = 8
POOL_WINDOWS = (2, 4, 8, 16)
CONV_WIDTH = 3
N_ADA = 6

SUBLANES = 8
LANES = 128
BF16_TILE_ROWS = 16
POOL_HALO = 16
CONV_CARRY_ROWS = (CONV_WIDTH - 1) * SUBLANES
MXU_WIDTH = 256
MIXER_TILE = 256
MIXER_TILES_PER_STEP = 4
MIXER_PREP_STEPS = 8
CHANNEL_TILE = 256
CHANNEL_TILES_PER_STEP = 2
FF_CHUNK = 256
UP_LOOKAHEAD = 5
ADA_TILE = 1024
ADA_STREAMS = 2
VMEM_LIMIT_BYTES = 56 * 1024 * 1024

BF16 = jnp.bfloat16
F32 = jnp.float32


def _dot(a, b):
    return jnp.dot(a, b, preferred_element_type=F32)


def _dot_ref(a, w_ref):
    n = w_ref.shape[-1]
    assert n % MXU_WIDTH == 0
    return jnp.concatenate(
        [_dot(a, w_ref[:, c0:c0 + MXU_WIDTH]) for c0 in range(0, n, MXU_WIDTH)], axis=1)


def _gelu(x):
    return 0.5 * x * (1.0 + lax.erf(x * math.sqrt(0.5)))


def _rms_scale(x):
    ms = jnp.mean(x * x, axis=-1, keepdims=True)
    return x * lax.rsqrt(ms + EPS)


def _ada_kernel(c_ref, *refs):
    c = c_ref[...]
    s = (c * jax.nn.sigmoid(c)).astype(BF16)
    o_ref = refs[2 * ADA_STREAMS]
    for k in range(ADA_STREAMS):
        w_ref, b_ref = refs[2 * k], refs[2 * k + 1]
        o_ref[k] = _dot(s, w_ref[...].astype(BF16)) + b_ref[...]


def _ada_call(c, w_ada, b_ada):
    bsz, d = c.shape
    n = w_ada.shape[1]
    steps = n // (ADA_TILE * ADA_STREAMS)
    assert n % (ADA_TILE * ADA_STREAMS) == 0
    b_row = b_ada.reshape(1, n)
    operands, in_specs = [c], [pl.BlockSpec((bsz, d), lambda j: (0, 0))]
    for k in range(ADA_STREAMS):
        col_block = lambda j, k=k: (0, k * steps + j)
        operands += [w_ada, b_row]
        in_specs += [pl.BlockSpec((d, ADA_TILE), col_block), pl.BlockSpec((1, ADA_TILE), col_block)]
    return pl.pallas_call(
        _ada_kernel,
        grid=(steps,),
        in_specs=in_specs,
        out_specs=pl.BlockSpec((ADA_STREAMS, bsz, ADA_TILE), lambda j: (0, 0, j)),
        out_shape=jax.ShapeDtypeStruct((ADA_STREAMS, bsz, n // ADA_STREAMS), F32),
        compiler_params=pltpu.CompilerParams(dimension_semantics=("arbitrary",)),
        name="ada_mod",
    )(*operands)


def _cast_block(src_ref, dst_ref):
    n = src_ref.shape[1]
    dst_ref[:, 0:n] = src_ref[...].astype(BF16)
    if dst_ref.shape[1] > n:
        dst_ref[:, n:] = jnp.zeros((dst_ref.shape[0], dst_ref.shape[1] - n), BF16)


def _mixer_kernel(x_ref, mod_ref, g1_ref, b_gate_ref, lng_ref, lnb_ref, wsp_ref, bsp_ref, wpool_ref,
                  bpool_ref, pscale_ref, w_in_f32, w_gate_f32, wpa_f32, wpb_f32, wout_f32,
                  w_up_ref, w_down_ref, o_ref, w_up_bf_ref, w_down_bf_ref, w_ref, pool_carry_ref,
                  *, n_prep, n_tiles):
    step = pl.program_id(0)

    @pl.when(step < n_prep)
    def _():
        rows = w_in_f32.shape[0]
        r0 = pl.multiple_of(step * rows, rows)
        off = 0
        for src in (w_in_f32, w_gate_f32, wpa_f32, wpb_f32, wout_f32):
            w_ref[pl.ds(r0, rows), off:off + src.shape[1]] = src[...].astype(BF16)
            off += src.shape[1]

    @pl.when(step >= n_prep)
    def _():
        _mixer_tokens(x_ref, mod_ref, g1_ref, b_gate_ref, lng_ref, lnb_ref, wsp_ref, bsp_ref, wpool_ref,
                      bpool_ref, pscale_ref, w_up_ref, w_down_ref, o_ref, w_up_bf_ref, w_down_bf_ref,
                      w_ref, pool_carry_ref, in_width=w_in_f32.shape[1], j=(step - n_prep) % n_tiles)


def _mixer_tokens(x_ref, mod_ref, g1_ref, b_gate_ref, lng_ref, lnb_ref, wsp_ref, bsp_ref, wpool_ref,
                  bpool_ref, pscale_ref, w_up_ref, w_down_ref, o_ref, w_up_bf_ref, w_down_bf_ref,
                  w_ref, pool_carry_ref, *, in_width, j):
    ts, d = MIXER_TILE, x_ref.shape[2]
    _cast_block(w_up_ref, w_up_bf_ref)
    _cast_block(w_down_ref, w_down_bf_ref)
    w_in_ref = w_ref.at[:, 0:in_width]
    w_gate_ref = w_ref.at[:, in_width:in_width + 2 * d]
    wpa_ref = w_ref.at[:, in_width + 2 * d:in_width + 3 * d]
    wpb_ref = w_ref.at[:, in_width + 3 * d:in_width + 4 * d]
    wout_ref = w_ref.at[:, in_width + 4 * d:in_width + 5 * d]

    @pl.when(j == 0)
    def _():
        pool_carry_ref[...] = jnp.zeros_like(pool_carry_ref)

    sh1 = mod_ref[0, 0:1, :]
    sc1 = mod_ref[0, 1:2, :]
    gt1 = mod_ref[0, 2:3, :]
    a_w = d
    gd = a_w // A_GROUPS
    p = lax.broadcasted_iota(jnp.int32, (GMLP_BLOCK, GMLP_BLOCK), 0)
    q = lax.broadcasted_iota(jnp.int32, (GMLP_BLOCK, GMLP_BLOCK), 1)
    allowed = (q // CHUNK) <= (p // CHUNK)
    w_masked = [jnp.where(allowed, wsp_ref[g], 0.0).astype(BF16) for g in range(A_GROUPS)]

    def token_tile(row0):
        x = x_ref[0, row0:row0 + ts, :]
        h = _rms_scale(x) * (g1_ref[...] * (1.0 + sc1)) + sh1
        hb = h.astype(BF16)

        v = _gelu(_dot_ref(hb, w_in_ref.at[:, a_w:2 * a_w]))
        hbp = _dot_ref(hb, w_in_ref.at[:, 2 * a_w:w_in_ref.shape[1]])
        gates = jax.nn.sigmoid(_dot_ref(hb, w_gate_ref) + b_gate_ref[...])
        u = _gelu(_dot_ref(hb, w_in_ref.at[:, 0:a_w]))
        mu = jnp.mean(v, axis=-1, keepdims=True)
        vc = v - mu
        var = jnp.mean(vc * vc, axis=-1, keepdims=True)
        vn = (vc * lax.rsqrt(var + EPS)) * lng_ref[...] + lnb_ref[...]
        vnb = vn.astype(BF16)

        ext = jnp.concatenate([pool_carry_ref[...], hbp], axis=0)
        pool_carry_ref[...] = hbp[ts - POOL_HALO:, :]
        bgd = hbp.shape[1] // len(POOL_WINDOWS)
        t = j * x_ref.shape[1] + row0 + lax.broadcasted_iota(jnp.int32, (ts, bgd), 0)
        yb_cols = []
        for gi, w in enumerate(POOL_WINDOWS):
            e = ext[:, gi * bgd:(gi + 1) * bgd]
            acc, span = e, 1
            while span < w:
                acc = acc + pltpu.roll(acc, span, 0)
                span *= 2
            win = acc[POOL_HALO:, :]
            cur = e[POOL_HALO:, :]
            count = jnp.minimum(t + 1, w).astype(F32)
            pooled = win / count - cur
            mixed = _dot(pooled.astype(BF16), wpool_ref[gi]) + bpool_ref[gi:gi + 1, :]
            yb_cols.append(mixed)
        y_b = jnp.concatenate(yb_cols, axis=1) * pscale_ref[...]
        branch_b = gates[:, d:] * _dot_ref(y_b.astype(BF16), wpb_ref)

        bias_map = bsp_ref[...]
        s_rows = []
        for n in range(ts // GMLP_BLOCK):
            b0 = n * GMLP_BLOCK
            cols = [_dot(w_masked[g], vnb[b0:b0 + GMLP_BLOCK, g * gd:(g + 1) * gd]) for g in range(A_GROUPS)]
            s_rows.append(jnp.concatenate(cols, axis=1) + bias_map)
        s = jnp.concatenate(s_rows, axis=0)
        y_a = u * s

        merged = gates[:, :d] * _dot_ref(y_a.astype(BF16), wpa_ref) + branch_b
        x1 = x + gt1 * _dot_ref(merged.astype(BF16), wout_ref)
        o_ref[0, row0:row0 + ts, :] = jnp.concatenate(
            [_interleave_rows(x1[r:r + CHANNEL_TILE, :]) for r in range(0, ts, CHANNEL_TILE)], axis=0)

    for row0 in range(0, x_ref.shape[1], ts):
        token_tile(row0)


def _resident(shape):
    nd = len(shape)
    return pl.BlockSpec(shape, lambda b, j: (0,) * nd, pipeline_mode=pl.Buffered(1))


def _pitch_pad(n_cols):
    return LANES if (n_cols // LANES) % SUBLANES == 0 else 0


def _row_blocking(n_rows, n_steps):
    for steps_per_block in range(1, n_steps + 1):
        if n_steps % steps_per_block == 0 and (n_rows * steps_per_block) % n_steps == 0:
            rows = n_rows * steps_per_block // n_steps
            if rows % BF16_TILE_ROWS == 0:
                return rows, steps_per_block
    raise ValueError(f"no bf16-aligned row blocking of {n_rows} rows over {n_steps} steps")


def _mixer_call(x, mod, g1, w_in, w_gate, b_gate, ln_g, ln_b, w_sp, bias_map, w_pool, b_pool,
                pool_scale, w_pa, w_pb, w_out, w_up, w_down):
    bsz, s_len, d = x.shape
    ts = MIXER_TILE * MIXER_TILES_PER_STEP
    n_tiles = s_len // ts
    n_prep = MIXER_PREP_STEPS
    assert s_len % ts == 0 and MIXER_TILE % GMLP_BLOCK == 0 and MIXER_TILE >= POOL_HALO
    assert MIXER_TILE % CHANNEL_TILE == 0
    assert w_gate.shape[1] == 2 * d and w_pa.shape == w_pb.shape == w_out.shape == (d, d)
    assert d % (n_prep * BF16_TILE_ROWS) == 0
    row = lambda a: a.reshape(1, -1)

    def token_step(t):
        return jnp.maximum(t - n_prep, 0)

    def token_block(t):
        return (token_step(t) // n_tiles, token_step(t) % n_tiles, 0)

    def resident(a):
        nd = a.ndim
        return pl.BlockSpec(a.shape, lambda t: (0,) * nd, pipeline_mode=pl.Buffered(1))

    operands = [
        (x, pl.BlockSpec((1, ts, d), token_block)),
        (mod, pl.BlockSpec((1, N_ADA, d), lambda t: (token_step(t) // n_tiles, 0, 0))),
    ]
    for a in (row(g1), row(b_gate), row(ln_g), row(ln_b), w_sp, bias_map, w_pool.astype(BF16), b_pool,
              row(pool_scale)):
        operands.append((a, resident(a)))
    mixer_weights = (w_in, w_gate, w_pa, w_pb, w_out)
    for w in mixer_weights:
        operands.append((w, pl.BlockSpec((d // n_prep, w.shape[1]), lambda t: (jnp.minimum(t, n_prep - 1), 0))))
    out_specs = [pl.BlockSpec((1, ts, d), token_block)]
    out_shapes = [jax.ShapeDtypeStruct(x.shape, F32)]
    for w in (w_up, w_down):
        rows, steps_per_block = _row_blocking(w.shape[0], bsz * n_tiles)
        index_map = lambda t, spb=steps_per_block: (token_step(t) // spb, 0)
        operands.append((w, pl.BlockSpec((rows, w.shape[1]), index_map)))
        padded = w.shape[1] + _pitch_pad(w.shape[1])
        out_specs.append(pl.BlockSpec((rows, padded), index_map))
        out_shapes.append(jax.ShapeDtypeStruct((w.shape[0], padded), BF16))
    packed_width = sum(w.shape[1] for w in mixer_weights)
    packed_width += _pitch_pad(packed_width)
    return pl.pallas_call(
        functools.partial(_mixer_kernel, n_prep=n_prep, n_tiles=n_tiles),
        grid=(n_prep + bsz * n_tiles,),
        in_specs=[spec for _, spec in operands],
        out_specs=out_specs,
        out_shape=out_shapes,
        scratch_shapes=[pltpu.VMEM((d, packed_width), BF16),
                        pltpu.VMEM((POOL_HALO, w_pool.shape[0] * w_pool.shape[1]), F32)],
        compiler_params=pltpu.CompilerParams(
            dimension_semantics=("arbitrary",), vmem_limit_bytes=VMEM_LIMIT_BYTES),
        name="mixer",
    )(*[a for a, _ in operands])


def _interleave_rows(x):
    n, d = x.shape
    return jnp.swapaxes(x.reshape(SUBLANES, n // SUBLANES, d), 0, 1).reshape(n, d)


def _deinterleave_rows(y):
    n, d = y.shape
    return jnp.swapaxes(y.reshape(n // SUBLANES, SUBLANES, d), 0, 1).reshape(n, d)


def _delay_rows(a, prev_row):
    n = a.shape[0]
    wrapped = pltpu.roll(a[n - SUBLANES:, :], 1, 0)
    first = jnp.where(lax.broadcasted_iota(jnp.int32, wrapped.shape, 0) == 0, prev_row, wrapped)
    return jnp.concatenate([first, a[:n - SUBLANES, :]], axis=0)


def _channel_kernel(x_ref, mod_ref, g2_ref, w_up_ref, cw_ref, cb_ref, w_down_ref, gf_ref,
                    o_ref, conv_carry_ref, *, final_norm):
    ts, d = CHANNEL_TILE, x_ref.shape[2]
    d_ff = w_down_ref.shape[0]
    n_chunks = d_ff // FF_CHUNK

    @pl.when(pl.program_id(1) == 0)
    def _():
        conv_carry_ref[...] = jnp.zeros_like(conv_carry_ref)

    sh2 = mod_ref[0, 3:4, :]
    sc2 = mod_ref[0, 4:5, :]
    gt2 = mod_ref[0, 5:6, :]

    def conv_cols(pre, c0):
        cols = pl.ds(c0, FF_CHUNK)
        taps = [pre]
        for m in range(1, CONV_WIDTH):
            r = (CONV_WIDTH - 1 - m) * SUBLANES + SUBLANES - 1
            taps.append(_delay_rows(taps[-1], conv_carry_ref[r:r + 1, cols]))
        conv_carry_ref[:, cols] = pre[ts - CONV_CARRY_ROWS:, :]
        out = cb_ref[:, cols]
        for k in range(CONV_WIDTH):
            out = out + taps[CONV_WIDTH - 1 - k] * cw_ref[k:k + 1, cols]
        return out

    def token_tile(r0):
        x = x_ref[0, r0:r0 + ts, :]
        h2 = _rms_scale(x) * (g2_ref[...] * (1.0 + sc2)) + sh2
        h2b = h2.astype(BF16)

        def up_project(c):
            return [_dot(h2b, w_up_ref[:, pl.ds(c0, FF_CHUNK)]) for c0 in (c * FF_CHUNK, d_ff + c * FF_CHUNK)]

        ahead = [up_project(i) for i in range(min(UP_LOOKAHEAD, n_chunks))]
        acc = jnp.zeros((ts, d), F32)
        for c in range(n_chunks):
            cur = ahead.pop(0)
            if c + UP_LOOKAHEAD < n_chunks:
                ahead.append(up_project(c + UP_LOOKAHEAD))
            gate = conv_cols(cur[0], c * FF_CHUNK)
            val = conv_cols(cur[1], d_ff + c * FF_CHUNK)
            f = gate * jax.nn.sigmoid(gate) * val
            acc = acc + _dot_ref(f.astype(BF16), w_down_ref.at[c * FF_CHUNK:(c + 1) * FF_CHUNK, 0:d])
        x2 = x + gt2 * acc
        if final_norm:
            x2 = _rms_scale(x2) * gf_ref[...]
        o_ref[0, r0:r0 + ts, :] = _deinterleave_rows(x2)

    for r0 in range(0, x_ref.shape[1], ts):
        token_tile(r0)


def _channel_call(x, mod, g2, w_up, conv_w, conv_b, w_down, g_final, final_norm):
    bsz, s_len, d = x.shape
    ts = CHANNEL_TILE * CHANNEL_TILES_PER_STEP
    assert s_len % ts == 0 and CHANNEL_TILE >= CONV_WIDTH * SUBLANES and w_down.shape[0] % FF_CHUNK == 0
    row = lambda a: a.reshape(1, -1)
    operands = [
        (x, pl.BlockSpec((1, ts, d), lambda b, j: (b, j, 0))),
        (mod, pl.BlockSpec((1, N_ADA, d), lambda b, j: (b, 0, 0))),
    ]
    for a in (row(g2), w_up, conv_w, row(conv_b), w_down, row(g_final)):
        operands.append((a, _resident(a.shape)))
    return pl.pallas_call(
        functools.partial(_channel_kernel, final_norm=final_norm),
        grid=(bsz, s_len // ts),
        in_specs=[spec for _, spec in operands],
        out_specs=pl.BlockSpec((1, ts, d), lambda b, j: (b, j, 0)),
        out_shape=jax.ShapeDtypeStruct(x.shape, F32),
        scratch_shapes=[pltpu.VMEM((CONV_CARRY_ROWS, conv_w.shape[1]), F32)],
        compiler_params=pltpu.CompilerParams(
            dimension_semantics=("arbitrary", "arbitrary"), vmem_limit_bytes=VMEM_LIMIT_BYTES),
        name="channel",
    )(*[a for a, _ in operands])


def kernel(x, c, w_ada, b_ada, g_norm1, w_in, ln_v_g, ln_v_b, w_spatial, b_spatial, w_pool, b_pool,
           pool_scale, w_proj_a, w_proj_b, w_gate, b_gate, w_out, g_norm2, w_up, conv_w, conv_b,
           w_down, g_final):
    depth = w_ada.shape[0]
    bsz, s_len, d = x.shape
    assert N_ADA % ADA_STREAMS == 0 and (N_ADA // ADA_STREAMS * d) % ADA_TILE == 0
    gd =w_in.shape[2] // 3 // A_GROUPS
    for l in range(depth):
        mod = _ada_call(c, w_ada[l], b_ada[l])
        mod = mod.reshape(ADA_STREAMS, bsz, N_ADA // ADA_STREAMS, d).swapaxes(0, 1).reshape(bsz, N_ADA, d)
        bias_map = jnp.repeat(b_spatial[l].T, gd, axis=1)
        x, w_up_bf, w_down_bf = _mixer_call(
            x, mod, g_norm1[l], w_in[l], w_gate[l], b_gate[l], ln_v_g[l], ln_v_b[l], w_spatial[l],
            bias_map, w_pool[l], b_pool[l], pool_scale[l], w_proj_a[l], w_proj_b[l], w_out[l],
            w_up[l], w_down[l])
        x = _channel_call(
            x, mod, g_norm2[l], w_up_bf, conv_w[l], conv_b[l], w_down_bf, g_final,
            final_norm=(l == depth - 1))
    return x
```

```python
import functools
import math

import jax
import jax.numpy as jnp
from jax import lax
from jax.experimental import pallas as pl
from jax.experimental.pallas import tpu as pltpu

EPS = 1e-6
CHUNK = 64
GMLP_BLOCK = 128
A_GROUPS = 8
POOL_WINDOWS = (2, 4, 8, 16)
CONV_WIDTH = 3
N_ADA = 6

SUBLANES = 8
LANES = 128
BF16_TILE_ROWS = 16
POOL_HALO = 16
CONV_CARRY_ROWS = (CONV_WIDTH - 1) * SUBLANES
MXU_WIDTH = 256
MIXER_TILE = 256
MIXER_TILES_PER_STEP = 4
MIXER_PREP_STEPS = 8
CHANNEL_TILE = 256
CHANNEL_TILES_PER_STEP = 2
FF_CHUNK = 256
UP_LOOKAHEAD = 5
ADA_TILE = 2048
VMEM_LIMIT_BYTES = 56 * 1024 * 1024

BF16 = jnp.bfloat16
F32 = jnp.float32


def _dot(a, b):
    return jnp.dot(a, b, preferred_element_type=F32)


def _dot_ref(a, w_ref):
    n = w_ref.shape[-1]
    assert n % MXU_WIDTH == 0
    return jnp.concatenate(
        [_dot(a, w_ref[:, c0:c0 + MXU_WIDTH]) for c0 in range(0, n, MXU_WIDTH)], axis=1)


def _gelu(x):
    return 0.5 * x * (1.0 + lax.erf(x * math.sqrt(0.5)))


def _rms_scale(x):
    ms = jnp.mean(x * x, axis=-1, keepdims=True)
    return x * lax.rsqrt(ms + EPS)


def _ada_kernel(c_ref, w_ref, b_ref, o_ref):
    c = c_ref[...]
    s = c * jax.nn.sigmoid(c)
    o_ref[...] = _dot(s.astype(BF16), w_ref[...].astype(BF16)) + b_ref[...]


def _ada_call(c, w_ada, b_ada):
    bsz, d = c.shape
    n = w_ada.shape[1]
    return pl.pallas_call(
        _ada_kernel,
        grid=(n // ADA_TILE,),
        in_specs=[
            pl.BlockSpec((bsz, d), lambda j: (0, 0)),
            pl.BlockSpec((d, ADA_TILE), lambda j: (0, j)),
            pl.BlockSpec((1, ADA_TILE), lambda j: (0, j)),
        ],
        out_specs=pl.BlockSpec((bsz, ADA_TILE), lambda j: (0, j)),
        out_shape=jax.ShapeDtypeStruct((bsz, n), F32),
        compiler_params=pltpu.CompilerParams(dimension_semantics=("arbitrary",)),
        name="ada_mod",
    )(c, w_ada, b_ada.reshape(1, n))


def _cast_block(src_ref, dst_ref):
    n = src_ref.shape[1]
    dst_ref[:, 0:n] = src_ref[...].astype(BF16)
    if dst_ref.shape[1] > n:
        dst_ref[:, n:] = jnp.zeros((dst_ref.shape[0], dst_ref.shape[1] - n), BF16)


def _mixer_kernel(x_ref, mod_ref, g1_ref, b_gate_ref, lng_ref, lnb_ref, wsp_ref, bsp_ref, wpool_ref,
                  bpool_ref, pscale_ref, w_in_f32, w_gate_f32, wpa_f32, wpb_f32, wout_f32,
                  w_up_ref, w_down_ref, o_ref, w_up_bf_ref, w_down_bf_ref, w_ref, pool_carry_ref,
                  *, n_prep, n_tiles):
    step = pl.program_id(0)

    @pl.when(step < n_prep)
    def _():
        rows = w_in_f32.shape[0]
        r0 = pl.multiple_of(step * rows, rows)
        off = 0
        for src in (w_in_f32, w_gate_f32, wpa_f32, wpb_f32, wout_f32):
            w_ref[pl.ds(r0, rows), off:off + src.shape[1]] = src[...].astype(BF16)
            off += src.shape[1]

    @pl.when(step >= n_prep)
    def _():
        _mixer_tokens(x_ref, mod_ref, g1_ref, b_gate_ref, lng_ref, lnb_ref, wsp_ref, bsp_ref, wpool_ref,
                      bpool_ref, pscale_ref, w_up_ref, w_down_ref, o_ref, w_up_bf_ref, w_down_bf_ref,
                      w_ref, pool_carry_ref, in_width=w_in_f32.shape[1], j=(step - n_prep) % n_tiles)


def _mixer_tokens(x_ref, mod_ref, g1_ref, b_gate_ref, lng_ref, lnb_ref, wsp_ref, bsp_ref, wpool_ref,
                  bpool_ref, pscale_ref, w_up_ref, w_down_ref, o_ref, w_up_bf_ref, w_down_bf_ref,
                  w_ref, pool_carry_ref, *, in_width, j):
    ts, d = MIXER_TILE, x_ref.shape[2]
    _cast_block(w_up_ref, w_up_bf_ref)
    _cast_block(w_down_ref, w_down_bf_ref)
    w_in_ref = w_ref.at[:, 0:in_width]
    w_gate_ref = w_ref.at[:, in_width:in_width + 2 * d]
    wpa_ref = w_ref.at[:, in_width + 2 * d:in_width + 3 * d]
    wpb_ref = w_ref.at[:, in_width + 3 * d:in_width + 4 * d]
    wout_ref = w_ref.at[:, in_width + 4 * d:in_width + 5 * d]

    @pl.when(j == 0)
    def _():
        pool_carry_ref[...] = jnp.zeros_like(pool_carry_ref)

    sh1 = mod_ref[0, 0:1, :]
    sc1 = mod_ref[0, 1:2, :]
    gt1 = mod_ref[0, 2:3, :]
    a_w = d
    gd = a_w // A_GROUPS
    p = lax.broadcasted_iota(jnp.int32, (GMLP_BLOCK, GMLP_BLOCK), 0)
    q = lax.broadcasted_iota(jnp.int32, (GMLP_BLOCK, GMLP_BLOCK), 1)
    allowed = (q // CHUNK) <= (p // CHUNK)
    w_masked = [jnp.where(allowed, wsp_ref[g], 0.0).astype(BF16) for g in range(A_GROUPS)]

    def token_tile(row0):
        x = x_ref[0, row0:row0 + ts, :]
        h = _rms_scale(x) * (g1_ref[...] * (1.0 + sc1)) + sh1
        hb = h.astype(BF16)

        v = _gelu(_dot_ref(hb, w_in_ref.at[:, a_w:2 * a_w]))
        hbp = _dot_ref(hb, w_in_ref.at[:, 2 * a_w:w_in_ref.shape[1]])
        gates = jax.nn.sigmoid(_dot_ref(hb, w_gate_ref) + b_gate_ref[...])
        u = _gelu(_dot_ref(hb, w_in_ref.at[:, 0:a_w]))
        mu = jnp.mean(v, axis=-1, keepdims=True)
        vc = v - mu
        var = jnp.mean(vc * vc, axis=-1, keepdims=True)
        vn = (vc * lax.rsqrt(var + EPS)) * lng_ref[...] + lnb_ref[...]
        vnb = vn.astype(BF16)

        ext = jnp.concatenate([pool_carry_ref[...], hbp], axis=0)
        pool_carry_ref[...] = hbp[ts - POOL_HALO:, :]
        bgd = hbp.shape[1] // len(POOL_WINDOWS)
        t = j * x_ref.shape[1] + row0 + lax.broadcasted_iota(jnp.int32, (ts, bgd), 0)
        yb_cols = []
        for gi, w in enumerate(POOL_WINDOWS):
            e = ext[:, gi * bgd:(gi + 1) * bgd]
            acc, span = e, 1
            while span < w:
                acc = acc + pltpu.roll(acc, span, 0)
                span *= 2
            win = acc[POOL_HALO:, :]
            cur = e[POOL_HALO:, :]
            count = jnp.minimum(t + 1, w).astype(F32)
            pooled = win / count - cur
            mixed = _dot(pooled.astype(BF16), wpool_ref[gi]) + bpool_ref[gi:gi + 1, :]
            yb_cols.append(mixed)
        y_b = jnp.concatenate(yb_cols, axis=1) * pscale_ref[...]
        branch_b = gates[:, d:] * _dot_ref(y_b.astype(BF16), wpb_ref)

        bias_map = bsp_ref[...]
        s_rows = []
        for n in range(ts // GMLP_BLOCK):
            b0 = n * GMLP_BLOCK
            cols = [_dot(w_masked[g], vnb[b0:b0 + GMLP_BLOCK, g * gd:(g + 1) * gd]) for g in range(A_GROUPS)]
            s_rows.append(jnp.concatenate(cols, axis=1) + bias_map)
        s = jnp.concatenate(s_rows, axis=0)
        y_a = u * s

        merged = gates[:, :d] * _dot_ref(y_a.astype(BF16), wpa_ref) + branch_b
        x1 = x + gt1 * _dot_ref(merged.astype(BF16), wout_ref)
        o_ref[0, row0:row0 + ts, :] = jnp.concatenate(
            [_interleave_rows(x1[r:r + CHANNEL_TILE, :]) for r in range(0, ts, CHANNEL_TILE)], axis=0)

    for row0 in range(0, x_ref.shape[1], ts):
        token_tile(row0)


def _resident(shape):
    nd = len(shape)
    return pl.BlockSpec(shape, lambda b, j: (0,) * nd, pipeline_mode=pl.Buffered(1))


def _pitch_pad(n_cols):
    return LANES if (n_cols // LANES) % SUBLANES == 0 else 0


def _row_blocking(n_rows, n_steps):
    for steps_per_block in range(1, n_steps + 1):
        if n_steps % steps_per_block == 0 and (n_rows * steps_per_block) % n_steps == 0:
            rows = n_rows * steps_per_block // n_steps
            if rows % BF16_TILE_ROWS == 0:
                return rows, steps_per_block
    raise ValueError(f"no bf16-aligned row blocking of {n_rows} rows over {n_steps} steps")


def _mixer_call(x, mod, g1, w_in, w_gate, b_gate, ln_g, ln_b, w_sp, bias_map, w_pool, b_pool,
                pool_scale, w_pa, w_pb, w_out, w_up, w_down):
    bsz, s_len, d = x.shape
    ts = MIXER_TILE * MIXER_TILES_PER_STEP
    n_tiles = s_len // ts
    n_prep = MIXER_PREP_STEPS
    assert s_len % ts == 0 and MIXER_TILE % GMLP_BLOCK == 0 and MIXER_TILE >= POOL_HALO
    assert MIXER_TILE % CHANNEL_TILE == 0
    assert w_gate.shape[1] == 2 * d and w_pa.shape == w_pb.shape == w_out.shape == (d, d)
    assert d % (n_prep * BF16_TILE_ROWS) == 0
    row = lambda a: a.reshape(1, -1)

    def token_step(t):
        return jnp.maximum(t - n_prep, 0)

    def token_block(t):
        return (token_step(t) // n_tiles, token_step(t) % n_tiles, 0)

    def resident(a):
        nd = a.ndim
        return pl.BlockSpec(a.shape, lambda t: (0,) * nd, pipeline_mode=pl.Buffered(1))

    operands = [
        (x, pl.BlockSpec((1, ts, d), token_block)),
        (mod, pl.BlockSpec((1, N_ADA, d), lambda t: (token_step(t) // n_tiles, 0, 0))),
    ]
    for a in (row(g1), row(b_gate), row(ln_g), row(ln_b), w_sp, bias_map, w_pool.astype(BF16), b_pool,
              row(pool_scale)):
        operands.append((a, resident(a)))
    mixer_weights = (w_in, w_gate, w_pa, w_pb, w_out)
    for w in mixer_weights:
        operands.append((w, pl.BlockSpec((d // n_prep, w.shape[1]), lambda t: (jnp.minimum(t, n_prep - 1), 0))))
    out_specs = [pl.BlockSpec((1, ts, d), token_block)]
    out_shapes = [jax.ShapeDtypeStruct(x.shape, F32)]
    for w in (w_up, w_down):
        rows, steps_per_block = _row_blocking(w.shape[0], bsz * n_tiles)
        index_map = lambda t, spb=steps_per_block: (token_step(t) // spb, 0)
        operands.append((w, pl.BlockSpec((rows, w.shape[1]), index_map)))
        padded = w.shape[1] + _pitch_pad(w.shape[1])
        out_specs.append(pl.BlockSpec((rows, padded), index_map))
        out_shapes.append(jax.ShapeDtypeStruct((w.shape[0], padded), BF16))
    packed_width = sum(w.shape[1] for w in mixer_weights)
    packed_width += _pitch_pad(packed_width)
    return pl.pallas_call(
        functools.partial(_mixer_kernel, n_prep=n_prep, n_tiles=n_tiles),
        grid=(n_prep + bsz * n_tiles,),
        in_specs=[spec for _, spec in operands],
        out_specs=out_specs,
        out_shape=out_shapes,
        scratch_shapes=[pltpu.VMEM((d, packed_width), BF16),
                        pltpu.VMEM((POOL_HALO, w_pool.shape[0] * w_pool.shape[1]), F32)],
        compiler_params=pltpu.CompilerParams(
            dimension_semantics=("arbitrary",), vmem_limit_bytes=VMEM_LIMIT_BYTES),
        name="mixer",
    )(*[a for a, _ in operands])


def _interleave_rows(x):
    n, d = x.shape
    return jnp.swapaxes(x.reshape(SUBLANES, n // SUBLANES, d), 0, 1).reshape(n, d)


def _deinterleave_rows(y):
    n, d = y.shape
    return jnp.swapaxes(y.reshape(n // SUBLANES, SUBLANES, d), 0, 1).reshape(n, d)


def _delay_rows(a, prev_row):
    n = a.shape[0]
    wrapped = pltpu.roll(a[n - SUBLANES:, :], 1, 0)
    first = jnp.where(lax.broadcasted_iota(jnp.int32, wrapped.shape, 0) == 0, prev_row, wrapped)
    return jnp.concatenate([first, a[:n - SUBLANES, :]], axis=0)


def _channel_kernel(x_ref, mod_ref, xn_ref, modn_ref, g2_ref, w_up_ref, cw_ref, cb_ref, w_down_ref, gf_ref,
                    o_ref, conv_carry_ref, h2b_staged_ref, pre_staged_ref, *, final_norm):
    ts, d = CHANNEL_TILE, x_ref.shape[2]
    d_ff = w_down_ref.shape[0]
    n_chunks = d_ff // FF_CHUNK
    n_staged = min(UP_LOOKAHEAD, n_chunks)

    def tile_input(x, m_ref):
        return (_rms_scale(x) * (g2_ref[...] * (1.0 + m_ref[0, 4:5, :])) + m_ref[0, 3:4, :]).astype(BF16)

    def up_project(h2b, c):
        return [_dot(h2b, w_up_ref[:, pl.ds(c0, FF_CHUNK)]) for c0 in (c * FF_CHUNK, d_ff + c * FF_CHUNK)]

    def stage_head(x, m_ref):
        h2b = tile_input(x, m_ref)
        h2b_staged_ref[...] = h2b
        for i in range(n_staged):
            pre_staged_ref[2 * i], pre_staged_ref[2 * i + 1] = up_project(h2b, i)

    @pl.when((pl.program_id(0) == 0) & (pl.program_id(1) == 0))
    def _():
        stage_head(x_ref[0, 0:ts, :], mod_ref)

    @pl.when(pl.program_id(1) == 0)
    def _():
        conv_carry_ref[...] = jnp.zeros_like(conv_carry_ref)

    gt2 = mod_ref[0, 5:6, :]

    def conv_cols(pre, c0):
        cols = pl.ds(c0, FF_CHUNK)
        taps = [pre]
        for m in range(1, CONV_WIDTH):
            r = (CONV_WIDTH - 1 - m) * SUBLANES + SUBLANES - 1
            taps.append(_delay_rows(taps[-1], conv_carry_ref[r:r + 1, cols]))
        conv_carry_ref[:, cols] = pre[ts - CONV_CARRY_ROWS:, :]
        out = cb_ref[:, cols]
        for k in range(CONV_WIDTH):
            out = out + taps[CONV_WIDTH - 1 - k] * cw_ref[k:k + 1, cols]
        return out

    def token_tile(r0, staged):
        x = x_ref[0, r0:r0 + ts, :]
        if staged:
            h2b = h2b_staged_ref[...]
            ahead = []
        else:
            h2b = tile_input(x, mod_ref)
            ahead = [up_project(h2b, i) for i in range(n_staged)]
        acc = jnp.zeros((ts, d), F32)
        for c in range(n_chunks):
            if staged and c < n_staged:
                cur = [pre_staged_ref[2 * c], pre_staged_ref[2 * c + 1]]
            else:
                cur = ahead.pop(0)
            if c + UP_LOOKAHEAD < n_chunks:
                ahead.append(up_project(h2b, c + UP_LOOKAHEAD))
            gate = conv_cols(cur[0], c * FF_CHUNK)
            val = conv_cols(cur[1], d_ff + c * FF_CHUNK)
            f = gate * jax.nn.sigmoid(gate) * val
            acc = acc + _dot_ref(f.astype(BF16), w_down_ref.at[c * FF_CHUNK:(c + 1) * FF_CHUNK, 0:d])
        x2 = x + gt2 * acc
        if final_norm:
            x2 = _rms_scale(x2) * gf_ref[...]
        o_ref[0, r0:r0 + ts, :] = _deinterleave_rows(x2)

    for r0 in range(0, x_ref.shape[1], ts):
        token_tile(r0, staged=(r0 == 0))
    stage_head(xn_ref[0], modn_ref)


def _channel_call(x, mod, g2, w_up, conv_w, conv_b, w_down, g_final, final_norm):
    bsz, s_len, d = x.shape
    ts = CHANNEL_TILE * CHANNEL_TILES_PER_STEP
    assert s_len % ts == 0 and CHANNEL_TILE >= CONV_WIDTH * SUBLANES and w_down.shape[0] % FF_CHUNK == 0
    row = lambda a: a.reshape(1, -1)
    n_steps = s_len // ts

    def next_step(b, j):
        t = jnp.minimum(b * n_steps + j + 1, bsz * n_steps - 1)
        return t // n_steps, t % n_steps

    operands = [
        (x, pl.BlockSpec((1, ts, d), lambda b, j: (b, j, 0))),
        (mod, pl.BlockSpec((1, N_ADA, d), lambda b, j: (b, 0, 0))),
        (x, pl.BlockSpec((1, CHANNEL_TILE, d),
                         lambda b, j: (next_step(b, j)[0], next_step(b, j)[1] * CHANNEL_TILES_PER_STEP, 0))),
        (mod, pl.BlockSpec((1, N_ADA, d), lambda b, j: (next_step(b, j)[0], 0, 0))),
    ]
    for a in (row(g2), w_up, conv_w, row(conv_b), w_down, row(g_final)):
        operands.append((a, _resident(a.shape)))
    n_staged = min(UP_LOOKAHEAD, w_down.shape[0] // FF_CHUNK)
    return pl.pallas_call(
        functools.partial(_channel_kernel, final_norm=final_norm),
        grid=(bsz, n_steps),
        in_specs=[spec for _, spec in operands],
        out_specs=pl.BlockSpec((1, ts, d), lambda b, j: (b, j, 0)),
        out_shape=jax.ShapeDtypeStruct(x.shape, F32),
        scratch_shapes=[pltpu.VMEM((CONV_CARRY_ROWS, conv_w.shape[1]), F32),
                        pltpu.VMEM((CHANNEL_TILE, d), BF16),
                        pltpu.VMEM((2 * n_staged, CHANNEL_TILE, FF_CHUNK), F32)],
        compiler_params=pltpu.CompilerParams(
            dimension_semantics=("arbitrary", "arbitrary"), vmem_limit_bytes=VMEM_LIMIT_BYTES),
        name="channel",
    )(*[a for a, _ in operands])


def kernel(x, c, w_ada, b_ada, g_norm1, w_in, ln_v_g, ln_v_b, w_spatial, b_spatial, w_pool, b_pool,
           pool_scale, w_proj_a, w_proj_b, w_gate, b_gate, w_out, g_norm2, w_up, conv_w, conv_b,
           w_down, g_final):
    depth = w_ada.shape[0]
    bsz, s_len, d = x.shape
    assert w_ada.shape[2] % ADA_TILE == 0
    gd = w_in.shape[2] // 3 // A_GROUPS
    for l in range(depth):
        mod = _ada_call(c, w_ada[l], b_ada[l]).reshape(bsz, N_ADA, d)
        bias_map = jnp.repeat(b_spatial[l].T, gd, axis=1)
        x, w_up_bf, w_down_bf = _mixer_call(
            x, mod, g_norm1[l], w_in[l], w_gate[l], b_gate[l], ln_v_g[l], ln_v_b[l], w_spatial[l],
            bias_map, w_pool[l], b_pool[l], pool_scale[l], w_proj_a[l], w_proj_b[l], w_out[l],
            w_up[l], w_down[l])
        x = _channel_call(
            x, mod, g_norm2[l], w_up_bf, conv_w[l], conv_b[l], w_down_bf, g_final,
            final_norm=(l == depth - 1))
    return x
```

```python
import functools
import math

import jax
import jax.numpy as jnp
from jax import lax
from jax.experimental import pallas as pl
from jax.experimental.pallas import tpu as pltpu

EPS = 1e-6
CHUNK = 64
GMLP_BLOCK = 128
A_GROUPS = 8
POOL_WINDOWS = (2, 4, 8, 16)
CONV_WIDTH = 3
N_ADA = 6

SUBLANES = 8
LANES = 128
BF16_TILE_ROWS = 16
POOL_HALO = 16
CONV_CARRY_ROWS = (CONV_WIDTH - 1) * SUBLANES
MXU_WIDTH = 256
MIXER_TILE = 256
MIXER_TILES_PER_STEP = 4
MIXER_PREP_STEPS = 16
CHANNEL_TILE = 256
CHANNEL_TILES_PER_STEP = 2
FF_CHUNK = 256
UP_LOOKAHEAD = 5
ADA_TILE = 512
VMEM_LIMIT_BYTES = 56 * 1024 * 1024

BF16 = jnp.bfloat16
F32 = jnp.float32


def _dot(a, b):
    return jnp.dot(a, b, preferred_element_type=F32)


def _dot_ref(a, w_ref):
    n = w_ref.shape[-1]
    assert n % MXU_WIDTH == 0
    return jnp.concatenate(
        [_dot(a, w_ref[:, c0:c0 + MXU_WIDTH]) for c0 in range(0, n, MXU_WIDTH)], axis=1)


def _gelu(x):
    return 0.5 * x * (1.0 + lax.erf(x * math.sqrt(0.5)))


def _rms_scale(x):
    ms = jnp.mean(x * x, axis=-1, keepdims=True)
    return x * lax.rsqrt(ms + EPS)


def _cast_block(src_ref, dst_ref):
    n = src_ref.shape[1]
    dst_ref[:, 0:n] = src_ref[...].astype(BF16)
    if dst_ref.shape[1] > n:
        dst_ref[:, n:] = jnp.zeros((dst_ref.shape[0], dst_ref.shape[1] - n), BF16)


def _mixer_kernel(x_ref, c_ref, w_ada_ref, b_ada_ref, g1_ref, b_gate_ref, lng_ref, lnb_ref, wsp_ref, bsp_ref,
                  wpool_ref, bpool_ref, pscale_ref, w_in_f32, w_gate_f32, wpa_f32, wpb_f32, wout_f32,
                  w_up_ref, w_down_ref, o_ref, w_up_bf_ref, w_down_bf_ref, mod_out_ref, w_ref, mod_ref,
                  pool_carry_ref, *, n_prep, n_tiles):
    step = pl.program_id(0)

    @pl.when(step < mod_ref.shape[0])
    def _():
        c = c_ref[...]
        s = (c * jax.nn.sigmoid(c)).astype(BF16)
        mod_ref[step] = _dot(s, w_ada_ref[...].astype(BF16)) + b_ada_ref[...]

    @pl.when(step < n_prep)
    def _():
        rows = w_in_f32.shape[0]
        r0 = pl.multiple_of(step * rows, rows)
        off = 0
        for src in (w_in_f32, w_gate_f32, wpa_f32, wpb_f32, wout_f32):
            w_ref[pl.ds(r0, rows), off:off + src.shape[1]] = src[...].astype(BF16)
            off += src.shape[1]

    @pl.when(step >= n_prep)
    def _():
        @pl.when(step == n_prep)
        def _():
            mod_out_ref[...] = mod_ref[...]

        _mixer_tokens(x_ref, mod_ref, g1_ref, b_gate_ref, lng_ref, lnb_ref, wsp_ref, bsp_ref, wpool_ref,
                      bpool_ref, pscale_ref, w_up_ref, w_down_ref, o_ref, w_up_bf_ref, w_down_bf_ref,
                      w_ref, pool_carry_ref, in_width=w_in_f32.shape[1], j=(step - n_prep) % n_tiles,
                      b=(step - n_prep) // n_tiles)


def _mixer_tokens(x_ref, mod_ref, g1_ref, b_gate_ref, lng_ref, lnb_ref, wsp_ref, bsp_ref, wpool_ref,
                  bpool_ref, pscale_ref, w_up_ref, w_down_ref, o_ref, w_up_bf_ref, w_down_bf_ref,
                  w_ref, pool_carry_ref, *, in_width, j, b):
    ts, d = MIXER_TILE, x_ref.shape[2]
    _cast_block(w_up_ref, w_up_bf_ref)
    _cast_block(w_down_ref, w_down_bf_ref)
    w_in_ref = w_ref.at[:, 0:in_width]
    w_gate_ref = w_ref.at[:, in_width:in_width + 2 * d]
    wpa_ref = w_ref.at[:, in_width + 2 * d:in_width + 3 * d]
    wpb_ref = w_ref.at[:, in_width + 3 * d:in_width + 4 * d]
    wout_ref = w_ref.at[:, in_width + 4 * d:in_width + 5 * d]

    @pl.when(j == 0)
    def _():
        pool_carry_ref[...] = jnp.zeros_like(pool_carry_ref)

    def mod_row(k):
        per = d // ADA_TILE
        return jnp.concatenate([mod_ref[k * per + i, pl.ds(b, 1), :] for i in range(per)], axis=1)

    sh1, sc1, gt1 = mod_row(0), mod_row(1), mod_row(2)
    a_w = d
    gd = a_w // A_GROUPS
    p = lax.broadcasted_iota(jnp.int32, (GMLP_BLOCK, GMLP_BLOCK), 0)
    q = lax.broadcasted_iota(jnp.int32, (GMLP_BLOCK, GMLP_BLOCK), 1)
    allowed = (q // CHUNK) <= (p // CHUNK)
    w_masked = [jnp.where(allowed, wsp_ref[g], 0.0).astype(BF16) for g in range(A_GROUPS)]

    def token_tile(row0):
        x = x_ref[0, row0:row0 + ts, :]
        h = _rms_scale(x) * (g1_ref[...] * (1.0 + sc1)) + sh1
        hb = h.astype(BF16)

        v = _gelu(_dot_ref(hb, w_in_ref.at[:, a_w:2 * a_w]))
        hbp = _dot_ref(hb, w_in_ref.at[:, 2 * a_w:w_in_ref.shape[1]])
        gates = jax.nn.sigmoid(_dot_ref(hb, w_gate_ref) + b_gate_ref[...])
        u = _gelu(_dot_ref(hb, w_in_ref.at[:, 0:a_w]))
        mu = jnp.mean(v, axis=-1, keepdims=True)
        vc = v - mu
        var = jnp.mean(vc * vc, axis=-1, keepdims=True)
        vn = (vc * lax.rsqrt(var + EPS)) * lng_ref[...] + lnb_ref[...]
        vnb = vn.astype(BF16)

        ext = jnp.concatenate([pool_carry_ref[...], hbp], axis=0)
        pool_carry_ref[...] = hbp[ts - POOL_HALO:, :]
        bgd = hbp.shape[1] // len(POOL_WINDOWS)
        t = j * x_ref.shape[1] + row0 + lax.broadcasted_iota(jnp.int32, (ts, bgd), 0)
        yb_cols = []
        for gi, w in enumerate(POOL_WINDOWS):
            e = ext[:, gi * bgd:(gi + 1) * bgd]
            acc, span = e, 1
            while span < w:
                acc = acc + pltpu.roll(acc, span, 0)
                span *= 2
            win = acc[POOL_HALO:, :]
            cur = e[POOL_HALO:, :]
            count = jnp.minimum(t + 1, w).astype(F32)
            pooled = win / count - cur
            mixed = _dot(pooled.astype(BF16), wpool_ref[gi]) + bpool_ref[gi:gi + 1, :]
            yb_cols.append(mixed)
        y_b = jnp.concatenate(yb_cols, axis=1) * pscale_ref[...]
        branch_b = gates[:, d:] * _dot_ref(y_b.astype(BF16), wpb_ref)

        bias_map = bsp_ref[...]
        s_rows = []
        for n in range(ts // GMLP_BLOCK):
            b0 = n * GMLP_BLOCK
            cols = [_dot(w_masked[g], vnb[b0:b0 + GMLP_BLOCK, g * gd:(g + 1) * gd]) for g in range(A_GROUPS)]
            s_rows.append(jnp.concatenate(cols, axis=1) + bias_map)
        s = jnp.concatenate(s_rows, axis=0)
        y_a = u * s

        merged = gates[:, :d] * _dot_ref(y_a.astype(BF16), wpa_ref) + branch_b
        x1 = x + gt1 * _dot_ref(merged.astype(BF16), wout_ref)
        o_ref[0, row0:row0 + ts, :] = jnp.concatenate(
            [_interleave_rows(x1[r:r + CHANNEL_TILE, :]) for r in range(0, ts, CHANNEL_TILE)], axis=0)

    for row0 in range(0, x_ref.shape[1], ts):
        token_tile(row0)


def _resident(shape):
    nd = len(shape)
    return pl.BlockSpec(shape, lambda b, j: (0,) * nd, pipeline_mode=pl.Buffered(1))


def _pitch_pad(n_cols):
    return LANES if (n_cols // LANES) % SUBLANES == 0 else 0


def _row_blocking(n_rows, n_steps):
    for steps_per_block in range(1, n_steps + 1):
        if n_steps % steps_per_block == 0 and (n_rows * steps_per_block) % n_steps == 0:
            rows = n_rows * steps_per_block // n_steps
            if rows % BF16_TILE_ROWS == 0:
                return rows, steps_per_block
    raise ValueError(f"no bf16-aligned row blocking of {n_rows} rows over {n_steps} steps")


def _mixer_call(x, c, w_ada, b_ada, g1, w_in, w_gate, b_gate, ln_g, ln_b, w_sp, bias_map, w_pool, b_pool,
                pool_scale, w_pa, w_pb, w_out, w_up, w_down):
    bsz, s_len, d = x.shape
    ts = MIXER_TILE * MIXER_TILES_PER_STEP
    n_tiles = s_len // ts
    n_prep = MIXER_PREP_STEPS
    assert s_len % ts == 0 and MIXER_TILE % GMLP_BLOCK == 0 and MIXER_TILE >= POOL_HALO
    assert MIXER_TILE % CHANNEL_TILE == 0
    assert w_gate.shape[1] == 2 * d and w_pa.shape == w_pb.shape == w_out.shape == (d, d)
    assert d % (n_prep * BF16_TILE_ROWS) == 0
    row = lambda a: a.reshape(1, -1)

    def token_step(t):
        return jnp.maximum(t - n_prep, 0)

    def token_block(t):
        return (token_step(t) // n_tiles, token_step(t) % n_tiles, 0)

    def resident(a):
        nd = a.ndim
        return pl.BlockSpec(a.shape, lambda t: (0,) * nd, pipeline_mode=pl.Buffered(1))

    n_ada = w_ada.shape[1] // ADA_TILE
    assert w_ada.shape[1] % ADA_TILE == 0 and d % ADA_TILE == 0 and n_ada <= n_prep
    ada_block = lambda t: (0, jnp.minimum(t, n_ada - 1))
    operands = [
        (x, pl.BlockSpec((1, ts, d), token_block)),
        (c, pl.BlockSpec(c.shape, lambda t: (0, 0))),
        (w_ada, pl.BlockSpec((d, ADA_TILE), ada_block)),
        (row(b_ada), pl.BlockSpec((1, ADA_TILE), ada_block)),
    ]
    for a in (row(g1), row(b_gate), row(ln_g), row(ln_b), w_sp, bias_map, w_pool.astype(BF16), b_pool,
              row(pool_scale)):
        operands.append((a, resident(a)))
    mixer_weights = (w_in, w_gate, w_pa, w_pb, w_out)
    for w in mixer_weights:
        operands.append((w, pl.BlockSpec((d // n_prep, w.shape[1]), lambda t: (jnp.minimum(t, n_prep - 1), 0))))
    out_specs = [pl.BlockSpec((1, ts, d), token_block)]
    out_shapes = [jax.ShapeDtypeStruct(x.shape, F32)]
    for w in (w_up, w_down):
        rows, steps_per_block = _row_blocking(w.shape[0], bsz * n_tiles)
        index_map = lambda t, spb=steps_per_block: (token_step(t) // spb, 0)
        operands.append((w, pl.BlockSpec((rows, w.shape[1]), index_map)))
        padded = w.shape[1] + _pitch_pad(w.shape[1])
        out_specs.append(pl.BlockSpec((rows, padded), index_map))
        out_shapes.append(jax.ShapeDtypeStruct((w.shape[0], padded), BF16))
    out_specs.append(pl.BlockSpec((n_ada, bsz, ADA_TILE), lambda t: (0, 0, 0)))
    out_shapes.append(jax.ShapeDtypeStruct((n_ada, bsz, ADA_TILE), F32))
    packed_width = sum(w.shape[1] for w in mixer_weights)
    packed_width += _pitch_pad(packed_width)
    return pl.pallas_call(
        functools.partial(_mixer_kernel, n_prep=n_prep, n_tiles=n_tiles),
        grid=(n_prep + bsz * n_tiles,),
        in_specs=[spec for _, spec in operands],
        out_specs=out_specs,
        out_shape=out_shapes,
        scratch_shapes=[pltpu.VMEM((d, packed_width), BF16),
                        pltpu.VMEM((n_ada, bsz, ADA_TILE), F32),
                        pltpu.VMEM((POOL_HALO, w_pool.shape[0] * w_pool.shape[1]), F32)],
        compiler_params=pltpu.CompilerParams(
            dimension_semantics=("arbitrary",), vmem_limit_bytes=VMEM_LIMIT_BYTES),
        name="mixer",
    )(*[a for a, _ in operands])


def _interleave_rows(x):
    n, d = x.shape
    return jnp.swapaxes(x.reshape(SUBLANES, n // SUBLANES, d), 0, 1).reshape(n, d)


def _deinterleave_rows(y):
    n, d = y.shape
    return jnp.swapaxes(y.reshape(n // SUBLANES, SUBLANES, d), 0, 1).reshape(n, d)


def _delay_rows(a, prev_row):
    n = a.shape[0]
    wrapped = pltpu.roll(a[n - SUBLANES:, :], 1, 0)
    first = jnp.where(lax.broadcasted_iota(jnp.int32, wrapped.shape, 0) == 0, prev_row, wrapped)
    return jnp.concatenate([first, a[:n - SUBLANES, :]], axis=0)


def _channel_kernel(x_ref, mod_ref, g2_ref, w_up_ref, cw_ref, cb_ref, w_down_ref, gf_ref,
                    o_ref, conv_carry_ref, *, final_norm):
    ts, d = CHANNEL_TILE, x_ref.shape[2]
    d_ff = w_down_ref.shape[0]
    n_chunks = d_ff // FF_CHUNK

    @pl.when(pl.program_id(1) == 0)
    def _():
        conv_carry_ref[...] = jnp.zeros_like(conv_carry_ref)

    sh2 = mod_ref[0, 3:4, :]
    sc2 = mod_ref[0, 4:5, :]
    gt2 = mod_ref[0, 5:6, :]

    def conv_cols(pre, c0):
        cols = pl.ds(c0, FF_CHUNK)
        taps = [pre]
        for m in range(1, CONV_WIDTH):
            r = (CONV_WIDTH - 1 - m) * SUBLANES + SUBLANES - 1
            taps.append(_delay_rows(taps[-1], conv_carry_ref[r:r + 1, cols]))
        conv_carry_ref[:, cols] = pre[ts - CONV_CARRY_ROWS:, :]
        out = cb_ref[:, cols]
        for k in range(CONV_WIDTH):
            out = out + taps[CONV_WIDTH - 1 - k] * cw_ref[k:k + 1, cols]
        return out

    def token_tile(r0):
        x = x_ref[0, r0:r0 + ts, :]
        h2 = _rms_scale(x) * (g2_ref[...] * (1.0 + sc2)) + sh2
        h2b = h2.astype(BF16)

        def up_project(c):
            return [_dot(h2b, w_up_ref[:, pl.ds(c0, FF_CHUNK)]) for c0 in (c * FF_CHUNK, d_ff + c * FF_CHUNK)]

        ahead = [up_project(i) for i in range(min(UP_LOOKAHEAD, n_chunks))]
        acc = jnp.zeros((ts, d), F32)
        for c in range(n_chunks):
            cur = ahead.pop(0)
            if c + UP_LOOKAHEAD < n_chunks:
                ahead.append(up_project(c + UP_LOOKAHEAD))
            gate = conv_cols(cur[0], c * FF_CHUNK)
            val = conv_cols(cur[1], d_ff + c * FF_CHUNK)
            f = gate * jax.nn.sigmoid(gate) * val
            acc = acc + _dot_ref(f.astype(BF16), w_down_ref.at[c * FF_CHUNK:(c + 1) * FF_CHUNK, 0:d])
        x2 = x + gt2 * acc
        if final_norm:
            x2 = _rms_scale(x2) * gf_ref[...]
        o_ref[0, r0:r0 + ts, :] = _deinterleave_rows(x2)

    for r0 in range(0, x_ref.shape[1], ts):
        token_tile(r0)


def _channel_call(x, mod, g2, w_up, conv_w, conv_b, w_down, g_final, final_norm):
    bsz, s_len, d = x.shape
    ts = CHANNEL_TILE * CHANNEL_TILES_PER_STEP
    assert s_len % ts == 0 and CHANNEL_TILE >= CONV_WIDTH * SUBLANES and w_down.shape[0] % FF_CHUNK == 0
    row = lambda a: a.reshape(1, -1)
    operands = [
        (x, pl.BlockSpec((1, ts, d), lambda b, j: (b, j, 0))),
        (mod, pl.BlockSpec((1, N_ADA, d), lambda b, j: (b, 0, 0))),
    ]
    for a in (row(g2), w_up, conv_w, row(conv_b), w_down, row(g_final)):
        operands.append((a, _resident(a.shape)))
    return pl.pallas_call(
        functools.partial(_channel_kernel, final_norm=final_norm),
        grid=(bsz, s_len // ts),
        in_specs=[spec for _, spec in operands],
        out_specs=pl.BlockSpec((1, ts, d), lambda b, j: (b, j, 0)),
        out_shape=jax.ShapeDtypeStruct(x.shape, F32),
        scratch_shapes=[pltpu.VMEM((CONV_CARRY_ROWS, conv_w.shape[1]), F32)],
        compiler_params=pltpu.CompilerParams(
            dimension_semantics=("arbitrary", "arbitrary"), vmem_limit_bytes=VMEM_LIMIT_BYTES),
        name="channel",
    )(*[a for a, _ in operands])


def kernel(x, c, w_ada, b_ada, g_norm1, w_in, ln_v_g, ln_v_b, w_spatial, b_spatial, w_pool, b_pool,
           pool_scale, w_proj_a, w_proj_b, w_gate, b_gate, w_out, g_norm2, w_up, conv_w, conv_b,
           w_down, g_final):
    depth = w_ada.shape[0]
    bsz, s_len, d = x.shape
    gd = w_in.shape[2] // 3 // A_GROUPS
    for l in range(depth):
        bias_map = jnp.repeat(b_spatial[l].T, gd, axis=1)
        x, w_up_bf, w_down_bf, mod = _mixer_call(
            x, c, w_ada[l], b_ada[l], g_norm1[l], w_in[l], w_gate[l], b_gate[l], ln_v_g[l], ln_v_b[l], w_spatial[l],
            bias_map, w_pool[l], b_pool[l], pool_scale[l], w_proj_a[l], w_proj_b[l], w_out[l],
            w_up[l], w_down[l])
        mod = jnp.swapaxes(mod, 0, 1).reshape(bsz, N_ADA, d)
        x = _channel_call(
            x, mod, g_norm2[l], w_up_bf, conv_w[l], conv_b[l], w_down_bf, g_final,
            final_norm=(l == depth - 1))
    return x
```

```python
import functools
import math

import jax
import jax.numpy as jnp
from jax import lax
from jax.experimental import pallas as pl
from jax.experimental.pallas import tpu as pltpu

EPS = 1e-6
CHUNK = 64
GMLP_BLOCK = 128
A_GROUPS = 8
POOL_WINDOWS = (2, 4, 8, 16)
CONV_WIDTH = 3
N_ADA = 6

SUBLANES = 8
LANES = 128
BF16_TILE_ROWS = 16
POOL_HALO = 16
CONV_CARRY_ROWS = (CONV_WIDTH - 1) * SUBLANES
MXU_WIDTH = 256
MIXER_TILE = 256
MIXER_TILES_PER_STEP = 4
MIXER_PREP_STEPS = 16
CHANNEL_TILE = 256
CHANNEL_TILES_PER_STEP = 2
FF_CHUNK = 256
UP_LOOKAHEAD = 5
ADA_TILE = 512
VMEM_LIMIT_BYTES = 56 * 1024 * 1024

BF16 = jnp.bfloat16
F32 = jnp.float32


def _dot(a, b):
    return jnp.dot(a, b, preferred_element_type=F32)


def _dot_ref(a, w_ref):
    n = w_ref.shape[-1]
    assert n % MXU_WIDTH == 0
    return jnp.concatenate(
        [_dot(a, w_ref[:, c0:c0 + MXU_WIDTH]) for c0 in range(0, n, MXU_WIDTH)], axis=1)


def _gelu(x):
    return 0.5 * x * (1.0 + lax.erf(x * math.sqrt(0.5)))


def _rms_scale(x):
    ms = jnp.mean(x * x, axis=-1, keepdims=True)
    return x * lax.rsqrt(ms + EPS)


def _cast_block(src_ref, dst_ref):
    n = src_ref.shape[1]
    dst_ref[:, 0:n] = src_ref[...].astype(BF16)
    if dst_ref.shape[1] > n:
        dst_ref[:, n:] = jnp.zeros((dst_ref.shape[0], dst_ref.shape[1] - n), BF16)


def _mixer_kernel(x_ref, c_ref, w_ada_ref, b_ada_ref, g1_ref, b_gate_ref, lng_ref, lnb_ref, wsp_ref, bsp_ref,
                  wpool_f32, bpool_ref, pscale_ref, w_in_f32, w_gate_f32, wpa_f32, wpb_f32, wout_f32,
                  w_up_ref, w_down_ref, o_ref, w_up_bf_ref, w_down_bf_ref, mod_out_ref, w_ref, wpool_ref, mod_ref,
                  pool_carry_ref, *, n_prep, n_tiles):
    step = pl.program_id(0)

    @pl.when(step == 0)
    def _():
        wpool_ref[...] = wpool_f32[...].astype(BF16)

    @pl.when(step < mod_ref.shape[0])
    def _():
        c = c_ref[...]
        s = (c * jax.nn.sigmoid(c)).astype(BF16)
        mod_ref[step] = _dot(s, w_ada_ref[...].astype(BF16)) + b_ada_ref[...]

    @pl.when(step < n_prep)
    def _():
        rows = w_in_f32.shape[0]
        r0 = pl.multiple_of(step * rows, rows)
        off = 0
        for src in (w_in_f32, w_gate_f32, wpa_f32, wpb_f32, wout_f32):
            w_ref[pl.ds(r0, rows), off:off + src.shape[1]] = src[...].astype(BF16)
            off += src.shape[1]

    @pl.when(step >= n_prep)
    def _():
        @pl.when(step == n_prep)
        def _():
            mod_out_ref[...] = mod_ref[...]

        _mixer_tokens(x_ref, mod_ref, g1_ref, b_gate_ref, lng_ref, lnb_ref, wsp_ref, bsp_ref, wpool_ref,
                      bpool_ref, pscale_ref, w_up_ref, w_down_ref, o_ref, w_up_bf_ref, w_down_bf_ref,
                      w_ref, pool_carry_ref, in_width=w_in_f32.shape[1], j=(step - n_prep) % n_tiles,
                      b=(step - n_prep) // n_tiles)


def _mixer_tokens(x_ref, mod_ref, g1_ref, b_gate_ref, lng_ref, lnb_ref, wsp_ref, bsp_ref, wpool_ref,
                  bpool_ref, pscale_ref, w_up_ref, w_down_ref, o_ref, w_up_bf_ref, w_down_bf_ref,
                  w_ref, pool_carry_ref, *, in_width, j, b):
    ts, d = MIXER_TILE, x_ref.shape[2]
    _cast_block(w_up_ref, w_up_bf_ref)
    _cast_block(w_down_ref, w_down_bf_ref)
    w_in_ref = w_ref.at[:, 0:in_width]
    w_gate_ref = w_ref.at[:, in_width:in_width + 2 * d]
    wpa_ref = w_ref.at[:, in_width + 2 * d:in_width + 3 * d]
    wpb_ref = w_ref.at[:, in_width + 3 * d:in_width + 4 * d]
    wout_ref = w_ref.at[:, in_width + 4 * d:in_width + 5 * d]

    @pl.when(j == 0)
    def _():
        pool_carry_ref[...] = jnp.zeros_like(pool_carry_ref)

    def mod_row(k):
        per = d // ADA_TILE
        return jnp.concatenate([mod_ref[k * per + i, pl.ds(b, 1), :] for i in range(per)], axis=1)

    sh1, sc1, gt1 = mod_row(0), mod_row(1), mod_row(2)
    a_w = d
    gd = a_w // A_GROUPS
    p = lax.broadcasted_iota(jnp.int32, (GMLP_BLOCK, GMLP_BLOCK), 0)
    q = lax.broadcasted_iota(jnp.int32, (GMLP_BLOCK, GMLP_BLOCK), 1)
    allowed = (q // CHUNK) <= (p // CHUNK)
    w_masked = [jnp.where(allowed, wsp_ref[g], 0.0).astype(BF16) for g in range(A_GROUPS)]

    def token_tile(row0):
        x = x_ref[0, row0:row0 + ts, :]
        h = _rms_scale(x) * (g1_ref[...] * (1.0 + sc1)) + sh1
        hb = h.astype(BF16)

        v = _gelu(_dot_ref(hb, w_in_ref.at[:, a_w:2 * a_w]))
        hbp = _dot_ref(hb, w_in_ref.at[:, 2 * a_w:w_in_ref.shape[1]])
        gates = jax.nn.sigmoid(_dot_ref(hb, w_gate_ref) + b_gate_ref[...])
        u = _gelu(_dot_ref(hb, w_in_ref.at[:, 0:a_w]))
        mu = jnp.mean(v, axis=-1, keepdims=True)
        vc = v - mu
        var = jnp.mean(vc * vc, axis=-1, keepdims=True)
        vn = (vc * lax.rsqrt(var + EPS)) * lng_ref[...] + lnb_ref[...]
        vnb = vn.astype(BF16)

        ext = jnp.concatenate([pool_carry_ref[...], hbp], axis=0)
        pool_carry_ref[...] = hbp[ts - POOL_HALO:, :]
        bgd = hbp.shape[1] // len(POOL_WINDOWS)
        t = j * x_ref.shape[1] + row0 + lax.broadcasted_iota(jnp.int32, (ts, bgd), 0)
        yb_cols = []
        for gi, w in enumerate(POOL_WINDOWS):
            e = ext[:, gi * bgd:(gi + 1) * bgd]
            acc, span = e, 1
            while span < w:
                acc = acc + pltpu.roll(acc, span, 0)
                span *= 2
            win = acc[POOL_HALO:, :]
            cur = e[POOL_HALO:, :]
            count = jnp.minimum(t + 1, w).astype(F32)
            pooled = win / count - cur
            mixed = _dot(pooled.astype(BF16), wpool_ref[gi]) + bpool_ref[gi:gi + 1, :]
            yb_cols.append(mixed)
        y_b = jnp.concatenate(yb_cols, axis=1) * pscale_ref[...]
        branch_b = gates[:, d:] * _dot_ref(y_b.astype(BF16), wpb_ref)

        bias_map = bsp_ref[...]
        s_rows = []
        for n in range(ts // GMLP_BLOCK):
            b0 = n * GMLP_BLOCK
            cols = [_dot(w_masked[g], vnb[b0:b0 + GMLP_BLOCK, g * gd:(g + 1) * gd]) for g in range(A_GROUPS)]
            s_rows.append(jnp.concatenate(cols, axis=1) + bias_map)
        s = jnp.concatenate(s_rows, axis=0)
        y_a = u * s

        merged = gates[:, :d] * _dot_ref(y_a.astype(BF16), wpa_ref) + branch_b
        x1 = x + gt1 * _dot_ref(merged.astype(BF16), wout_ref)
        o_ref[0, row0:row0 + ts, :] = jnp.concatenate(
            [_interleave_rows(x1[r:r + CHANNEL_TILE, :]) for r in range(0, ts, CHANNEL_TILE)], axis=0)

    for row0 in range(0, x_ref.shape[1], ts):
        token_tile(row0)


def _resident(shape):
    nd = len(shape)
    return pl.BlockSpec(shape, lambda b, j: (0,) * nd, pipeline_mode=pl.Buffered(1))


def _pitch_pad(n_cols):
    return LANES if (n_cols // LANES) % SUBLANES == 0 else 0


def _row_blocking(n_rows, n_steps):
    for steps_per_block in range(1, n_steps + 1):
        if n_steps % steps_per_block == 0 and (n_rows * steps_per_block) % n_steps == 0:
            rows = n_rows * steps_per_block // n_steps
            if rows % BF16_TILE_ROWS == 0:
                return rows, steps_per_block
    raise ValueError(f"no bf16-aligned row blocking of {n_rows} rows over {n_steps} steps")


def _mixer_call(x, c, w_ada, b_ada, g1, w_in, w_gate, b_gate, ln_g, ln_b, w_sp, bias_map, w_pool, b_pool,
                pool_scale, w_pa, w_pb, w_out, w_up, w_down):
    bsz, s_len, d = x.shape
    ts = MIXER_TILE * MIXER_TILES_PER_STEP
    n_tiles = s_len // ts
    n_prep = MIXER_PREP_STEPS
    assert s_len % ts == 0 and MIXER_TILE % GMLP_BLOCK == 0 and MIXER_TILE >= POOL_HALO
    assert MIXER_TILE % CHANNEL_TILE == 0
    assert w_gate.shape[1] == 2 * d and w_pa.shape == w_pb.shape == w_out.shape == (d, d)
    assert d % (n_prep * BF16_TILE_ROWS) == 0
    row = lambda a: a.reshape(1, -1)

    def token_step(t):
        return jnp.maximum(t - n_prep, 0)

    def token_block(t):
        return (token_step(t) // n_tiles, token_step(t) % n_tiles, 0)

    def resident(a):
        nd = a.ndim
        return pl.BlockSpec(a.shape, lambda t: (0,) * nd, pipeline_mode=pl.Buffered(1))

    n_ada = w_ada.shape[1] // ADA_TILE
    assert w_ada.shape[1] % ADA_TILE == 0 and d % ADA_TILE == 0 and n_ada <= n_prep
    ada_block = lambda t: (0, jnp.minimum(t, n_ada - 1))
    operands = [
        (x, pl.BlockSpec((1, ts, d), token_block)),
        (c, pl.BlockSpec(c.shape, lambda t: (0, 0))),
        (w_ada, pl.BlockSpec((d, ADA_TILE), ada_block)),
        (row(b_ada), pl.BlockSpec((1, ADA_TILE), ada_block)),
    ]
    for a in (row(g1), row(b_gate), row(ln_g), row(ln_b), w_sp, bias_map, w_pool, b_pool,
              row(pool_scale)):
        operands.append((a, resident(a)))
    mixer_weights = (w_in, w_gate, w_pa, w_pb, w_out)
    for w in mixer_weights:
        operands.append((w, pl.BlockSpec((d // n_prep, w.shape[1]), lambda t: (jnp.minimum(t, n_prep - 1), 0))))
    out_specs = [pl.BlockSpec((1, ts, d), token_block)]
    out_shapes = [jax.ShapeDtypeStruct(x.shape, F32)]
    for w in (w_up, w_down):
        rows, steps_per_block = _row_blocking(w.shape[0], bsz * n_tiles)
        index_map = lambda t, spb=steps_per_block: (token_step(t) // spb, 0)
        operands.append((w, pl.BlockSpec((rows, w.shape[1]), index_map)))
        padded = w.shape[1] + _pitch_pad(w.shape[1])
        out_specs.append(pl.BlockSpec((rows, padded), index_map))
        out_shapes.append(jax.ShapeDtypeStruct((w.shape[0], padded), BF16))
    out_specs.append(pl.BlockSpec((n_ada, bsz, ADA_TILE), lambda t: (0, 0, 0)))
    out_shapes.append(jax.ShapeDtypeStruct((n_ada, bsz, ADA_TILE), F32))
    packed_width = sum(w.shape[1] for w in mixer_weights)
    packed_width += _pitch_pad(packed_width)
    return pl.pallas_call(
        functools.partial(_mixer_kernel, n_prep=n_prep, n_tiles=n_tiles),
        grid=(n_prep + bsz * n_tiles,),
        in_specs=[spec for _, spec in operands],
        out_specs=out_specs,
        out_shape=out_shapes,
        scratch_shapes=[pltpu.VMEM((d, packed_width), BF16),
                        pltpu.VMEM(w_pool.shape, BF16),
                        pltpu.VMEM((n_ada, bsz, ADA_TILE), F32),
                        pltpu.VMEM((POOL_HALO, w_pool.shape[0] * w_pool.shape[1]), F32)],
        compiler_params=pltpu.CompilerParams(
            dimension_semantics=("arbitrary",), vmem_limit_bytes=VMEM_LIMIT_BYTES),
        name="mixer",
    )(*[a for a, _ in operands])


def _interleave_rows(x):
    n, d = x.shape
    return jnp.swapaxes(x.reshape(SUBLANES, n // SUBLANES, d), 0, 1).reshape(n, d)


def _deinterleave_rows(y):
    n, d = y.shape
    return jnp.swapaxes(y.reshape(n // SUBLANES, SUBLANES, d), 0, 1).reshape(n, d)


def _delay_rows(a, prev_row):
    n = a.shape[0]
    wrapped = pltpu.roll(a[n - SUBLANES:, :], 1, 0)
    first = jnp.where(lax.broadcasted_iota(jnp.int32, wrapped.shape, 0) == 0, prev_row, wrapped)
    return jnp.concatenate([first, a[:n - SUBLANES, :]], axis=0)


def _channel_kernel(x_ref, mod_ref, g2_ref, w_up_ref, cw_ref, cb_ref, w_down_ref, gf_ref,
                    o_ref, conv_carry_ref, *, final_norm):
    ts, d = CHANNEL_TILE, x_ref.shape[2]
    d_ff = w_down_ref.shape[0]
    n_chunks = d_ff // FF_CHUNK

    @pl.when(pl.program_id(1) == 0)
    def _():
        conv_carry_ref[...] = jnp.zeros_like(conv_carry_ref)

    sh2 = mod_ref[0, 3:4, :]
    sc2 = mod_ref[0, 4:5, :]
    gt2 = mod_ref[0, 5:6, :]

    def conv_cols(pre, c0):
        cols = pl.ds(c0, FF_CHUNK)
        taps = [pre]
        for m in range(1, CONV_WIDTH):
            r = (CONV_WIDTH - 1 - m) * SUBLANES + SUBLANES - 1
            taps.append(_delay_rows(taps[-1], conv_carry_ref[r:r + 1, cols]))
        conv_carry_ref[:, cols] = pre[ts - CONV_CARRY_ROWS:, :]
        out = cb_ref[:, cols]
        for k in range(CONV_WIDTH):
            out = out + taps[CONV_WIDTH - 1 - k] * cw_ref[k:k + 1, cols]
        return out

    def token_tile(r0):
        x = x_ref[0, r0:r0 + ts, :]
        h2 = _rms_scale(x) * (g2_ref[...] * (1.0 + sc2)) + sh2
        h2b = h2.astype(BF16)

        def up_project(c):
            return [_dot(h2b, w_up_ref[:, pl.ds(c0, FF_CHUNK)]) for c0 in (c * FF_CHUNK, d_ff + c * FF_CHUNK)]

        ahead = [up_project(i) for i in range(min(UP_LOOKAHEAD, n_chunks))]
        acc = jnp.zeros((ts, d), F32)
        for c in range(n_chunks):
            cur = ahead.pop(0)
            if c + UP_LOOKAHEAD < n_chunks:
                ahead.append(up_project(c + UP_LOOKAHEAD))
            gate = conv_cols(cur[0], c * FF_CHUNK)
            val = conv_cols(cur[1], d_ff + c * FF_CHUNK)
            f = gate * jax.nn.sigmoid(gate) * val
            acc = acc + _dot_ref(f.astype(BF16), w_down_ref.at[c * FF_CHUNK:(c + 1) * FF_CHUNK, 0:d])
        x2 = x + gt2 * acc
        if final_norm:
            x2 = _rms_scale(x2) * gf_ref[...]
        o_ref[0, r0:r0 + ts, :] = _deinterleave_rows(x2)

    for r0 in range(0, x_ref.shape[1], ts):
        token_tile(r0)


def _channel_call(x, mod, g2, w_up, conv_w, conv_b, w_down, g_final, final_norm):
    bsz, s_len, d = x.shape
    ts = CHANNEL_TILE * CHANNEL_TILES_PER_STEP
    assert s_len % ts == 0 and CHANNEL_TILE >= CONV_WIDTH * SUBLANES and w_down.shape[0] % FF_CHUNK == 0
    row = lambda a: a.reshape(1, -1)
    operands = [
        (x, pl.BlockSpec((1, ts, d), lambda b, j: (b, j, 0))),
        (mod, pl.BlockSpec((1, N_ADA, d), lambda b, j: (b, 0, 0))),
    ]
    for a in (row(g2), w_up, conv_w, row(conv_b), w_down, row(g_final)):
        if a is conv_w:
            spec = pl.BlockSpec((None,) + a.shape[1:], lambda b, j: (0, 0, 0), pipeline_mode=pl.Buffered(1))
        else:
            spec = _resident(a.shape)
        operands.append((a, spec))
    return pl.pallas_call(
        functools.partial(_channel_kernel, final_norm=final_norm),
        grid=(bsz, s_len // ts),
        in_specs=[spec for _, spec in operands],
        out_specs=pl.BlockSpec((1, ts, d), lambda b, j: (b, j, 0)),
        out_shape=jax.ShapeDtypeStruct(x.shape, F32),
        scratch_shapes=[pltpu.VMEM((CONV_CARRY_ROWS, conv_w.shape[2]), F32)],
        compiler_params=pltpu.CompilerParams(
            dimension_semantics=("arbitrary", "arbitrary"), vmem_limit_bytes=VMEM_LIMIT_BYTES),
        name="channel",
    )(*[a for a, _ in operands])


def kernel(x, c, w_ada, b_ada, g_norm1, w_in, ln_v_g, ln_v_b, w_spatial, b_spatial, w_pool, b_pool,
           pool_scale, w_proj_a, w_proj_b, w_gate, b_gate, w_out, g_norm2, w_up, conv_w, conv_b,
           w_down, g_final):
    depth = w_ada.shape[0]
    bsz, s_len, d = x.shape
    gd = w_in.shape[2] // 3 // A_GROUPS
    for l in range(depth):
        bias_map = jnp.repeat(b_spatial[l].T, gd, axis=1)
        x, w_up_bf, w_down_bf, mod = _mixer_call(
            x, c, w_ada[l], b_ada[l], g_norm1[l], w_in[l], w_gate[l], b_gate[l], ln_v_g[l], ln_v_b[l], w_spatial[l],
            bias_map, w_pool[l], b_pool[l], pool_scale[l], w_proj_a[l], w_proj_b[l], w_out[l],
            w_up[l], w_down[l])
        mod = jnp.swapaxes(mod, 0, 1).reshape(bsz, N_ADA, d)
        x = _channel_call(
            x, mod, g_norm2[l], w_up_bf, conv_w[l:l + 1], conv_b[l], w_down_bf, g_final,
            final_norm=(l == depth - 1))
    return x
```

```python
import functools
import math

import jax
import jax.numpy as jnp
from jax import lax
from jax.experimental import pallas as pl
from jax.experimental.pallas import tpu as pltpu

EPS = 1e-6
CHUNK = 64
GMLP_BLOCK = 128
A_GROUPS = 8
POOL_WINDOWS = (2, 4, 8, 16)
CONV_WIDTH = 3
N_ADA = 6

SUBLANES = 8
LANES = 128
BF16_TILE_ROWS = 16
POOL_HALO = 16
CONV_CARRY_ROWS = (CONV_WIDTH - 1) * SUBLANES
MXU_WIDTH = 256
MIXER_TILE = 256
MIXER_TILES_PER_STEP = 4
MIXER_PREP_STEPS = 16
CHANNEL_TILE = 256
CHANNEL_TILES_PER_STEP = 2
FF_CHUNK = 256
UP_LOOKAHEAD = 5
ADA_TILE = 512
VMEM_LIMIT_BYTES = 56 * 1024 * 1024

BF16 = jnp.bfloat16
F32 = jnp.float32


def _dot(a, b):
    return jnp.dot(a, b, preferred_element_type=F32)


def _dot_ref(a, w_ref):
    n = w_ref.shape[-1]
    assert n % MXU_WIDTH == 0
    return jnp.concatenate(
        [_dot(a, w_ref[:, c0:c0 + MXU_WIDTH]) for c0 in range(0, n, MXU_WIDTH)], axis=1)


def _gelu(x):
    return 0.5 * x * (1.0 + lax.erf(x * math.sqrt(0.5)))


def _rms_scale(x):
    ms = jnp.mean(x * x, axis=-1, keepdims=True)
    return x * lax.rsqrt(ms + EPS)


def _mod_row(mod_ref, b, k, d):
    per = d // ADA_TILE
    return jnp.concatenate([mod_ref[k * per + i, pl.ds(b, 1), :] for i in range(per)], axis=1)


def _cast_block(src_ref, dst_ref):
    n = src_ref.shape[1]
    dst_ref[:, 0:n] = src_ref[...].astype(BF16)
    if dst_ref.shape[1] > n:
        dst_ref[:, n:] = jnp.zeros((dst_ref.shape[0], dst_ref.shape[1] - n), BF16)


def _mixer_kernel(x_ref, c_ref, w_ada_ref, b_ada_ref, g1_ref, b_gate_ref, lng_ref, lnb_ref, wsp_ref, bsp_ref,
                  wpool_f32, bpool_ref, pscale_ref, w_in_f32, w_gate_f32, wpa_f32, wpb_f32, wout_f32,
                  w_up_ref, w_down_ref, o_ref, w_up_bf_ref, w_down_bf_ref, mod_out_ref, w_ref, wpool_ref, mod_ref,
                  pool_carry_ref, *, n_prep, n_tiles):
    step = pl.program_id(0)

    @pl.when(step == 0)
    def _():
        wpool_ref[...] = wpool_f32[...].astype(BF16)

    @pl.when(step < mod_ref.shape[0])
    def _():
        c = c_ref[...]
        s = (c * jax.nn.sigmoid(c)).astype(BF16)
        mod_ref[step] = _dot(s, w_ada_ref[...].astype(BF16)) + b_ada_ref[...]

    @pl.when(step < n_prep)
    def _():
        rows = w_in_f32.shape[0]
        r0 = pl.multiple_of(step * rows, rows)
        off = 0
        for src in (w_in_f32, w_gate_f32, wpa_f32, wpb_f32, wout_f32):
            w_ref[pl.ds(r0, rows), off:off + src.shape[1]] = src[...].astype(BF16)
            off += src.shape[1]

    @pl.when(step >= n_prep)
    def _():
        @pl.when(step == n_prep)
        def _():
            mod_out_ref[...] = mod_ref[...]

        _mixer_tokens(x_ref, mod_ref, g1_ref, b_gate_ref, lng_ref, lnb_ref, wsp_ref, bsp_ref, wpool_ref,
                      bpool_ref, pscale_ref, w_up_ref, w_down_ref, o_ref, w_up_bf_ref, w_down_bf_ref,
                      w_ref, pool_carry_ref, in_width=w_in_f32.shape[1], j=(step - n_prep) % n_tiles,
                      b=(step - n_prep) // n_tiles)


def _mixer_tokens(x_ref, mod_ref, g1_ref, b_gate_ref, lng_ref, lnb_ref, wsp_ref, bsp_ref, wpool_ref,
                  bpool_ref, pscale_ref, w_up_ref, w_down_ref, o_ref, w_up_bf_ref, w_down_bf_ref,
                  w_ref, pool_carry_ref, *, in_width, j, b):
    ts, d = MIXER_TILE, x_ref.shape[2]
    _cast_block(w_up_ref, w_up_bf_ref)
    _cast_block(w_down_ref, w_down_bf_ref)
    w_in_ref = w_ref.at[:, 0:in_width]
    w_gate_ref = w_ref.at[:, in_width:in_width + 2 * d]
    wpa_ref = w_ref.at[:, in_width + 2 * d:in_width + 3 * d]
    wpb_ref = w_ref.at[:, in_width + 3 * d:in_width + 4 * d]
    wout_ref = w_ref.at[:, in_width + 4 * d:in_width + 5 * d]

    @pl.when(j == 0)
    def _():
        pool_carry_ref[...] = jnp.zeros_like(pool_carry_ref)

    sh1, sc1, gt1 = (_mod_row(mod_ref, b, k, d) for k in (0, 1, 2))
    a_w = d
    gd = a_w // A_GROUPS
    p = lax.broadcasted_iota(jnp.int32, (GMLP_BLOCK, GMLP_BLOCK), 0)
    q = lax.broadcasted_iota(jnp.int32, (GMLP_BLOCK, GMLP_BLOCK), 1)
    allowed = (q // CHUNK) <= (p // CHUNK)
    w_masked = [jnp.where(allowed, wsp_ref[g], 0.0).astype(BF16) for g in range(A_GROUPS)]

    def token_tile(row0):
        x = x_ref[0, row0:row0 + ts, :]
        h = _rms_scale(x) * (g1_ref[...] * (1.0 + sc1)) + sh1
        hb = h.astype(BF16)

        v = _gelu(_dot_ref(hb, w_in_ref.at[:, a_w:2 * a_w]))
        hbp = _dot_ref(hb, w_in_ref.at[:, 2 * a_w:w_in_ref.shape[1]])
        gates = jax.nn.sigmoid(_dot_ref(hb, w_gate_ref) + b_gate_ref[...])
        u = _gelu(_dot_ref(hb, w_in_ref.at[:, 0:a_w]))
        mu = jnp.mean(v, axis=-1, keepdims=True)
        vc = v - mu
        var = jnp.mean(vc * vc, axis=-1, keepdims=True)
        vn = (vc * lax.rsqrt(var + EPS)) * lng_ref[...] + lnb_ref[...]
        vnb = vn.astype(BF16)

        ext = jnp.concatenate([pool_carry_ref[...], hbp], axis=0)
        pool_carry_ref[...] = hbp[ts - POOL_HALO:, :]
        bgd = hbp.shape[1] // len(POOL_WINDOWS)
        t = j * x_ref.shape[1] + row0 + lax.broadcasted_iota(jnp.int32, (ts, bgd), 0)
        yb_cols = []
        for gi, w in enumerate(POOL_WINDOWS):
            e = ext[:, gi * bgd:(gi + 1) * bgd]
            acc, span = e, 1
            while span < w:
                acc = acc + pltpu.roll(acc, span, 0)
                span *= 2
            win = acc[POOL_HALO:, :]
            cur = e[POOL_HALO:, :]
            count = jnp.minimum(t + 1, w).astype(F32)
            pooled = win / count - cur
            mixed = _dot(pooled.astype(BF16), wpool_ref[gi]) + bpool_ref[gi:gi + 1, :]
            yb_cols.append(mixed)
        y_b = jnp.concatenate(yb_cols, axis=1) * pscale_ref[...]
        branch_b = gates[:, d:] * _dot_ref(y_b.astype(BF16), wpb_ref)

        bias_map = bsp_ref[...]
        s_rows = []
        for n in range(ts // GMLP_BLOCK):
            b0 = n * GMLP_BLOCK
            cols = [_dot(w_masked[g], vnb[b0:b0 + GMLP_BLOCK, g * gd:(g + 1) * gd]) for g in range(A_GROUPS)]
            s_rows.append(jnp.concatenate(cols, axis=1) + bias_map)
        s = jnp.concatenate(s_rows, axis=0)
        y_a = u * s

        merged = gates[:, :d] * _dot_ref(y_a.astype(BF16), wpa_ref) + branch_b
        x1 = x + gt1 * _dot_ref(merged.astype(BF16), wout_ref)
        o_ref[0, row0:row0 + ts, :] = jnp.concatenate(
            [_interleave_rows(x1[r:r + CHANNEL_TILE, :]) for r in range(0, ts, CHANNEL_TILE)], axis=0)

    for row0 in range(0, x_ref.shape[1], ts):
        token_tile(row0)


def _resident(shape):
    nd = len(shape)
    return pl.BlockSpec(shape, lambda b, j: (0,) * nd, pipeline_mode=pl.Buffered(1))


def _pitch_pad(n_cols):
    return LANES if (n_cols // LANES) % SUBLANES == 0 else 0


def _row_blocking(n_rows, n_steps):
    for steps_per_block in range(1, n_steps + 1):
        if n_steps % steps_per_block == 0 and (n_rows * steps_per_block) % n_steps == 0:
            rows = n_rows * steps_per_block // n_steps
            if rows % BF16_TILE_ROWS == 0:
                return rows, steps_per_block
    raise ValueError(f"no bf16-aligned row blocking of {n_rows} rows over {n_steps} steps")


def _mixer_call(x, c, w_ada, b_ada, g1, w_in, w_gate, b_gate, ln_g, ln_b, w_sp, bias_map, w_pool, b_pool,
                pool_scale, w_pa, w_pb, w_out, w_up, w_down):
    bsz, s_len, d = x.shape
    ts = MIXER_TILE * MIXER_TILES_PER_STEP
    n_tiles = s_len // ts
    n_prep = MIXER_PREP_STEPS
    assert s_len % ts == 0 and MIXER_TILE % GMLP_BLOCK == 0 and MIXER_TILE >= POOL_HALO
    assert MIXER_TILE % CHANNEL_TILE == 0
    assert w_gate.shape[1] == 2 * d and w_pa.shape == w_pb.shape == w_out.shape == (d, d)
    assert d % (n_prep * BF16_TILE_ROWS) == 0
    row = lambda a: a.reshape(1, -1)

    def token_step(t):
        return jnp.maximum(t - n_prep, 0)

    def token_block(t):
        return (token_step(t) // n_tiles, token_step(t) % n_tiles, 0)

    def resident(a):
        nd = a.ndim
        return pl.BlockSpec(a.shape, lambda t: (0,) * nd, pipeline_mode=pl.Buffered(1))

    n_ada = w_ada.shape[1] // ADA_TILE
    assert w_ada.shape[1] == N_ADA * d and d % ADA_TILE == 0 and n_ada <= n_prep
    ada_block = lambda t: (0, jnp.minimum(t, n_ada - 1))
    operands = [
        (x, pl.BlockSpec((1, ts, d), token_block)),
        (c, pl.BlockSpec(c.shape, lambda t: (0, 0))),
        (w_ada, pl.BlockSpec((d, ADA_TILE), ada_block)),
        (row(b_ada), pl.BlockSpec((1, ADA_TILE), ada_block)),
    ]
    for a in (row(g1), row(b_gate), row(ln_g), row(ln_b), w_sp, bias_map, w_pool, b_pool,
              row(pool_scale)):
        operands.append((a, resident(a)))
    mixer_weights = (w_in, w_gate, w_pa, w_pb, w_out)
    for w in mixer_weights:
        operands.append((w, pl.BlockSpec((d // n_prep, w.shape[1]), lambda t: (jnp.minimum(t, n_prep - 1), 0))))
    out_specs = [pl.BlockSpec((1, ts, d), token_block)]
    out_shapes = [jax.ShapeDtypeStruct(x.shape, F32)]
    for w in (w_up, w_down):
        rows, steps_per_block = _row_blocking(w.shape[0], bsz * n_tiles)
        index_map = lambda t, spb=steps_per_block: (token_step(t) // spb, 0)
        operands.append((w, pl.BlockSpec((rows, w.shape[1]), index_map)))
        padded = w.shape[1] + _pitch_pad(w.shape[1])
        out_specs.append(pl.BlockSpec((rows, padded), index_map))
        out_shapes.append(jax.ShapeDtypeStruct((w.shape[0], padded), BF16))
    out_specs.append(pl.BlockSpec((n_ada, bsz, ADA_TILE), lambda t: (0, 0, 0)))
    out_shapes.append(jax.ShapeDtypeStruct((n_ada, bsz, ADA_TILE), F32))
    packed_width = sum(w.shape[1] for w in mixer_weights)
    packed_width += _pitch_pad(packed_width)
    return pl.pallas_call(
        functools.partial(_mixer_kernel, n_prep=n_prep, n_tiles=n_tiles),
        grid=(n_prep + bsz * n_tiles,),
        in_specs=[spec for _, spec in operands],
        out_specs=out_specs,
        out_shape=out_shapes,
        scratch_shapes=[pltpu.VMEM((d, packed_width), BF16),
                        pltpu.VMEM(w_pool.shape, BF16),
                        pltpu.VMEM((n_ada, bsz, ADA_TILE), F32),
                        pltpu.VMEM((POOL_HALO, w_pool.shape[0] * w_pool.shape[1]), F32)],
        compiler_params=pltpu.CompilerParams(
            dimension_semantics=("arbitrary",), vmem_limit_bytes=VMEM_LIMIT_BYTES),
        name="mixer",
    )(*[a for a, _ in operands])


def _interleave_rows(x):
    n, d = x.shape
    return jnp.swapaxes(x.reshape(SUBLANES, n // SUBLANES, d), 0, 1).reshape(n, d)


def _deinterleave_rows(y):
    n, d = y.shape
    return jnp.swapaxes(y.reshape(n // SUBLANES, SUBLANES, d), 0, 1).reshape(n, d)


def _delay_rows(a, prev_row):
    n = a.shape[0]
    wrapped = pltpu.roll(a[n - SUBLANES:, :], 1, 0)
    first = jnp.where(lax.broadcasted_iota(jnp.int32, wrapped.shape, 0) == 0, prev_row, wrapped)
    return jnp.concatenate([first, a[:n - SUBLANES, :]], axis=0)


def _channel_kernel(x_ref, mod_ref, g2_ref, w_up_ref, cw_ref, cb_ref, w_down_ref, gf_ref,
                    o_ref, conv_carry_ref, *, final_norm):
    ts, d = CHANNEL_TILE, x_ref.shape[2]
    d_ff = w_down_ref.shape[0]
    n_chunks = d_ff // FF_CHUNK

    @pl.when(pl.program_id(1) == 0)
    def _():
        conv_carry_ref[...] = jnp.zeros_like(conv_carry_ref)

    sh2, sc2, gt2 = (_mod_row(mod_ref, pl.program_id(0), k, d) for k in (3, 4, 5))

    def conv_cols(pre, c0):
        cols = pl.ds(c0, FF_CHUNK)
        taps = [pre]
        for m in range(1, CONV_WIDTH):
            r = (CONV_WIDTH - 1 - m) * SUBLANES + SUBLANES - 1
            taps.append(_delay_rows(taps[-1], conv_carry_ref[r:r + 1, cols]))
        conv_carry_ref[:, cols] = pre[ts - CONV_CARRY_ROWS:, :]
        out = cb_ref[:, cols]
        for k in range(CONV_WIDTH):
            out = out + taps[CONV_WIDTH - 1 - k] * cw_ref[k:k + 1, cols]
        return out

    def token_tile(r0):
        x = x_ref[0, r0:r0 + ts, :]
        h2 = _rms_scale(x) * (g2_ref[...] * (1.0 + sc2)) + sh2
        h2b = h2.astype(BF16)

        def up_project(c):
            return [_dot(h2b, w_up_ref[:, pl.ds(c0, FF_CHUNK)]) for c0 in (c * FF_CHUNK, d_ff + c * FF_CHUNK)]

        ahead = [up_project(i) for i in range(min(UP_LOOKAHEAD, n_chunks))]
        acc = jnp.zeros((ts, d), F32)
        for c in range(n_chunks):
            cur = ahead.pop(0)
            if c + UP_LOOKAHEAD < n_chunks:
                ahead.append(up_project(c + UP_LOOKAHEAD))
            gate = conv_cols(cur[0], c * FF_CHUNK)
            val = conv_cols(cur[1], d_ff + c * FF_CHUNK)
            f = gate * jax.nn.sigmoid(gate) * val
            acc = acc + _dot_ref(f.astype(BF16), w_down_ref.at[c * FF_CHUNK:(c + 1) * FF_CHUNK, 0:d])
        x2 = x + gt2 * acc
        if final_norm:
            x2 = _rms_scale(x2) * gf_ref[...]
        o_ref[0, r0:r0 + ts, :] = _deinterleave_rows(x2)

    for r0 in range(0, x_ref.shape[1], ts):
        token_tile(r0)


def _channel_call(x, mod, g2, w_up, conv_w, conv_b, w_down, g_final, final_norm):
    bsz, s_len, d = x.shape
    ts = CHANNEL_TILE * CHANNEL_TILES_PER_STEP
    assert s_len % ts == 0 and CHANNEL_TILE >= CONV_WIDTH * SUBLANES and w_down.shape[0] % FF_CHUNK == 0
    row = lambda a: a.reshape(1, -1)
    operands = [
        (x, pl.BlockSpec((1, ts, d), lambda b, j: (b, j, 0))),
        (mod, _resident(mod.shape)),
    ]
    for a in (row(g2), w_up, conv_w, row(conv_b), w_down, row(g_final)):
        if a is conv_w:
            spec = pl.BlockSpec((None,) + a.shape[1:], lambda b, j: (0, 0, 0), pipeline_mode=pl.Buffered(1))
        else:
            spec = _resident(a.shape)
        operands.append((a, spec))
    return pl.pallas_call(
        functools.partial(_channel_kernel, final_norm=final_norm),
        grid=(bsz, s_len // ts),
        in_specs=[spec for _, spec in operands],
        out_specs=pl.BlockSpec((1, ts, d), lambda b, j: (b, j, 0)),
        out_shape=jax.ShapeDtypeStruct(x.shape, F32),
        scratch_shapes=[pltpu.VMEM((CONV_CARRY_ROWS, conv_w.shape[2]), F32)],
        compiler_params=pltpu.CompilerParams(
            dimension_semantics=("arbitrary", "arbitrary"), vmem_limit_bytes=VMEM_LIMIT_BYTES),
        name="channel",
    )(*[a for a, _ in operands])


def kernel(x, c, w_ada, b_ada, g_norm1, w_in, ln_v_g, ln_v_b, w_spatial, b_spatial, w_pool, b_pool,
           pool_scale, w_proj_a, w_proj_b, w_gate, b_gate, w_out, g_norm2, w_up, conv_w, conv_b,
           w_down, g_final):
    depth = w_ada.shape[0]
    bsz, s_len, d = x.shape
    gd = w_in.shape[2] // 3 // A_GROUPS
    for l in range(depth):
        bias_map = jnp.repeat(b_spatial[l].T, gd, axis=1)
        x, w_up_bf, w_down_bf, mod = _mixer_call(
            x, c, w_ada[l], b_ada[l], g_norm1[l], w_in[l], w_gate[l], b_gate[l], ln_v_g[l], ln_v_b[l], w_spatial[l],
            bias_map, w_pool[l], b_pool[l], pool_scale[l], w_proj_a[l], w_proj_b[l], w_out[l],
            w_up[l], w_down[l])
        x = _channel_call(
            x, mod, g_norm2[l], w_up_bf, conv_w[l:l + 1], conv_b[l], w_down_bf, g_final,
            final_norm=(l == depth - 1))
    return x
```

```python
import functools
import math

import jax
import jax.numpy as jnp
from jax import lax
from jax.experimental import pallas as pl
from jax.experimental.pallas import tpu as pltpu

EPS = 1e-6
CHUNK = 64
GMLP_BLOCK = 128
A_GROUPS = 8
POOL_WINDOWS = (2, 4, 8, 16)
CONV_WIDTH = 3
N_ADA = 6

SUBLANES = 8
LANES = 128
BF16_TILE_ROWS = 16
POOL_HALO = 16
CONV_CARRY_ROWS = (CONV_WIDTH - 1) * SUBLANES
MXU_WIDTH = 256
MIXER_TILE = 256
MIXER_TILES_PER_STEP = 4
MIXER_PREP_STEPS = 16
CHANNEL_TILE = 256
CHANNEL_TILES_PER_STEP = 2
FF_CHUNK = 256
UP_LOOKAHEAD = 5
ADA_TILE = 512
VMEM_LIMIT_BYTES = 56 * 1024 * 1024

BF16 = jnp.bfloat16
F32 = jnp.float32


def _dot(a, b):
    return jnp.dot(a, b, preferred_element_type=F32)


def _dot_ref(a, w_ref):
    n = w_ref.shape[-1]
    assert n % MXU_WIDTH == 0
    return jnp.concatenate(
        [_dot(a, w_ref[:, c0:c0 + MXU_WIDTH]) for c0 in range(0, n, MXU_WIDTH)], axis=1)


def _gelu(x):
    return 0.5 * x * (1.0 + lax.erf(x * math.sqrt(0.5)))


def _rms_scale(x):
    ms = jnp.mean(x * x, axis=-1, keepdims=True)
    return x * lax.rsqrt(ms + EPS)


def _mod_row(mod_ref, b, k, d):
    per = d // ADA_TILE
    return jnp.concatenate([mod_ref[k * per + i, pl.ds(b, 1), :] for i in range(per)], axis=1)


def _cast_block(src_ref, dst_ref):
    n = src_ref.shape[1]
    dst_ref[:, 0:n] = src_ref[...].astype(BF16)
    if dst_ref.shape[1] > n:
        dst_ref[:, n:] = jnp.zeros((dst_ref.shape[0], dst_ref.shape[1] - n), BF16)


def _mixer_kernel(x_ref, c_ref, w_ada_ref, b_ada_ref, g1_ref, b_gate_ref, lng_ref, lnb_ref, wsp_ref, b_sp_ref,
                  wpool_f32, bpool_ref, pscale_ref, w_in_f32, w_gate_f32, wpa_f32, wpb_f32, wout_f32,
                  w_up_ref, w_down_ref, o_ref, w_up_bf_ref, w_down_bf_ref, mod_out_ref, w_ref, wpool_ref, bsp_ref,
                  mod_ref, pool_carry_ref, *, n_prep, n_tiles):
    step = pl.program_id(0)

    @pl.when(step == 0)
    def _():
        wpool_ref[...] = wpool_f32[...].astype(BF16)
        gd = bsp_ref.shape[1] // A_GROUPS
        for g in range(A_GROUPS):
            bsp_ref[:, g * gd:(g + 1) * gd] = jnp.broadcast_to(b_sp_ref[g:g + 1, :], (gd, GMLP_BLOCK)).T

    @pl.when(step < mod_ref.shape[0])
    def _():
        c = c_ref[...]
        s = (c * jax.nn.sigmoid(c)).astype(BF16)
        mod_ref[step] = _dot(s, w_ada_ref[...].astype(BF16)) + b_ada_ref[...]

    @pl.when(step < n_prep)
    def _():
        rows = w_in_f32.shape[0]
        r0 = pl.multiple_of(step * rows, rows)
        off = 0
        for src in (w_in_f32, w_gate_f32, wpa_f32, wpb_f32, wout_f32):
            w_ref[pl.ds(r0, rows), off:off + src.shape[1]] = src[...].astype(BF16)
            off += src.shape[1]

    @pl.when(step >= n_prep)
    def _():
        @pl.when(step == n_prep)
        def _():
            mod_out_ref[...] = mod_ref[...]

        _mixer_tokens(x_ref, mod_ref, g1_ref, b_gate_ref, lng_ref, lnb_ref, wsp_ref, bsp_ref, wpool_ref,
                      bpool_ref, pscale_ref, w_up_ref, w_down_ref, o_ref, w_up_bf_ref, w_down_bf_ref,
                      w_ref, pool_carry_ref, in_width=w_in_f32.shape[1], j=(step - n_prep) % n_tiles,
                      b=(step - n_prep) // n_tiles)


def _mixer_tokens(x_ref, mod_ref, g1_ref, b_gate_ref, lng_ref, lnb_ref, wsp_ref, bsp_ref, wpool_ref,
                  bpool_ref, pscale_ref, w_up_ref, w_down_ref, o_ref, w_up_bf_ref, w_down_bf_ref,
                  w_ref, pool_carry_ref, *, in_width, j, b):
    ts, d = MIXER_TILE, x_ref.shape[2]
    _cast_block(w_up_ref, w_up_bf_ref)
    _cast_block(w_down_ref, w_down_bf_ref)
    w_in_ref = w_ref.at[:, 0:in_width]
    w_gate_ref = w_ref.at[:, in_width:in_width + 2 * d]
    wpa_ref = w_ref.at[:, in_width + 2 * d:in_width + 3 * d]
    wpb_ref = w_ref.at[:, in_width + 3 * d:in_width + 4 * d]
    wout_ref = w_ref.at[:, in_width + 4 * d:in_width + 5 * d]

    @pl.when(j == 0)
    def _():
        pool_carry_ref[...] = jnp.zeros_like(pool_carry_ref)

    sh1, sc1, gt1 = (_mod_row(mod_ref, b, k, d) for k in (0, 1, 2))
    a_w = d
    gd = a_w // A_GROUPS
    p = lax.broadcasted_iota(jnp.int32, (GMLP_BLOCK, GMLP_BLOCK), 0)
    q = lax.broadcasted_iota(jnp.int32, (GMLP_BLOCK, GMLP_BLOCK), 1)
    allowed = (q // CHUNK) <= (p // CHUNK)
    w_masked = [jnp.where(allowed, wsp_ref[g], 0.0).astype(BF16) for g in range(A_GROUPS)]

    def token_tile(row0):
        x = x_ref[0, row0:row0 + ts, :]
        h = _rms_scale(x) * (g1_ref[...] * (1.0 + sc1)) + sh1
        hb = h.astype(BF16)

        v = _gelu(_dot_ref(hb, w_in_ref.at[:, a_w:2 * a_w]))
        hbp = _dot_ref(hb, w_in_ref.at[:, 2 * a_w:w_in_ref.shape[1]])
        gates = jax.nn.sigmoid(_dot_ref(hb, w_gate_ref) + b_gate_ref[...])
        u = _gelu(_dot_ref(hb, w_in_ref.at[:, 0:a_w]))
        mu = jnp.mean(v, axis=-1, keepdims=True)
        vc = v - mu
        var = jnp.mean(vc * vc, axis=-1, keepdims=True)
        vn = (vc * lax.rsqrt(var + EPS)) * lng_ref[...] + lnb_ref[...]
        vnb = vn.astype(BF16)

        ext = jnp.concatenate([pool_carry_ref[...], hbp], axis=0)
        pool_carry_ref[...] = hbp[ts - POOL_HALO:, :]
        bgd = hbp.shape[1] // len(POOL_WINDOWS)
        t = j * x_ref.shape[1] + row0 + lax.broadcasted_iota(jnp.int32, (ts, bgd), 0)
        yb_cols = []
        for gi, w in enumerate(POOL_WINDOWS):
            e = ext[:, gi * bgd:(gi + 1) * bgd]
            acc, span = e, 1
            while span < w:
                acc = acc + pltpu.roll(acc, span, 0)
                span *= 2
            win = acc[POOL_HALO:, :]
            cur = e[POOL_HALO:, :]
            count = jnp.minimum(t + 1, w).astype(F32)
            pooled = win / count - cur
            mixed = _dot(pooled.astype(BF16), wpool_ref[gi]) + bpool_ref[gi:gi + 1, :]
            yb_cols.append(mixed)
        y_b = jnp.concatenate(yb_cols, axis=1) * pscale_ref[...]
        branch_b = gates[:, d:] * _dot_ref(y_b.astype(BF16), wpb_ref)

        bias_map = bsp_ref[...]
        s_rows = []
        for n in range(ts // GMLP_BLOCK):
            b0 = n * GMLP_BLOCK
            cols = [_dot(w_masked[g], vnb[b0:b0 + GMLP_BLOCK, g * gd:(g + 1) * gd]) for g in range(A_GROUPS)]
            s_rows.append(jnp.concatenate(cols, axis=1) + bias_map)
        s = jnp.concatenate(s_rows, axis=0)
        y_a = u * s

        merged = gates[:, :d] * _dot_ref(y_a.astype(BF16), wpa_ref) + branch_b
        x1 = x + gt1 * _dot_ref(merged.astype(BF16), wout_ref)
        o_ref[0, row0:row0 + ts, :] = jnp.concatenate(
            [_interleave_rows(x1[r:r + CHANNEL_TILE, :]) for r in range(0, ts, CHANNEL_TILE)], axis=0)

    for row0 in range(0, x_ref.shape[1], ts):
        token_tile(row0)


def _resident(shape):
    nd = len(shape)
    return pl.BlockSpec(shape, lambda b, j: (0,) * nd, pipeline_mode=pl.Buffered(1))


def _pitch_pad(n_cols):
    return LANES if (n_cols // LANES) % SUBLANES == 0 else 0


def _row_blocking(n_rows, n_steps):
    for steps_per_block in range(1, n_steps + 1):
        if n_steps % steps_per_block == 0 and (n_rows * steps_per_block) % n_steps == 0:
            rows = n_rows * steps_per_block // n_steps
            if rows % BF16_TILE_ROWS == 0:
                return rows, steps_per_block
    raise ValueError(f"no bf16-aligned row blocking of {n_rows} rows over {n_steps} steps")


def _mixer_call(x, c, w_ada, b_ada, g1, w_in, w_gate, b_gate, ln_g, ln_b, w_sp, b_sp, w_pool, b_pool,
                pool_scale, w_pa, w_pb, w_out, w_up, w_down):
    bsz, s_len, d = x.shape
    ts = MIXER_TILE * MIXER_TILES_PER_STEP
    n_tiles = s_len // ts
    n_prep = MIXER_PREP_STEPS
    assert s_len % ts == 0 and MIXER_TILE % GMLP_BLOCK == 0 and MIXER_TILE >= POOL_HALO
    assert MIXER_TILE % CHANNEL_TILE == 0
    assert w_gate.shape[1] == 2 * d and w_pa.shape == w_pb.shape == w_out.shape == (d, d)
    assert d % (n_prep * BF16_TILE_ROWS) == 0
    row = lambda a: a.reshape(1, -1)

    def token_step(t):
        return jnp.maximum(t - n_prep, 0)

    def token_block(t):
        return (token_step(t) // n_tiles, token_step(t) % n_tiles, 0)

    def resident(a):
        nd = a.ndim
        return pl.BlockSpec(a.shape, lambda t: (0,) * nd, pipeline_mode=pl.Buffered(1))

    n_ada = w_ada.shape[1] // ADA_TILE
    assert w_ada.shape[1] == N_ADA * d and d % ADA_TILE == 0 and n_ada <= n_prep
    ada_block = lambda t: (0, jnp.minimum(t, n_ada - 1))
    operands = [
        (x, pl.BlockSpec((1, ts, d), token_block)),
        (c, pl.BlockSpec(c.shape, lambda t: (0, 0))),
        (w_ada, pl.BlockSpec((d, ADA_TILE), ada_block)),
        (row(b_ada), pl.BlockSpec((1, ADA_TILE), ada_block)),
    ]
    for a in (row(g1), row(b_gate), row(ln_g), row(ln_b), w_sp, b_sp, w_pool, b_pool,
              row(pool_scale)):
        operands.append((a, resident(a)))
    mixer_weights = (w_in, w_gate, w_pa, w_pb, w_out)
    for w in mixer_weights:
        operands.append((w, pl.BlockSpec((d // n_prep, w.shape[1]), lambda t: (jnp.minimum(t, n_prep - 1), 0))))
    out_specs = [pl.BlockSpec((1, ts, d), token_block)]
    out_shapes = [jax.ShapeDtypeStruct(x.shape, F32)]
    for w in (w_up, w_down):
        rows, steps_per_block = _row_blocking(w.shape[0], bsz * n_tiles)
        index_map = lambda t, spb=steps_per_block: (token_step(t) // spb, 0)
        operands.append((w, pl.BlockSpec((rows, w.shape[1]), index_map)))
        padded = w.shape[1] + _pitch_pad(w.shape[1])
        out_specs.append(pl.BlockSpec((rows, padded), index_map))
        out_shapes.append(jax.ShapeDtypeStruct((w.shape[0], padded), BF16))
    out_specs.append(pl.BlockSpec((n_ada, bsz, ADA_TILE), lambda t: (0, 0, 0)))
    out_shapes.append(jax.ShapeDtypeStruct((n_ada, bsz, ADA_TILE), F32))
    packed_width = sum(w.shape[1] for w in mixer_weights)
    packed_width += _pitch_pad(packed_width)
    return pl.pallas_call(
        functools.partial(_mixer_kernel, n_prep=n_prep, n_tiles=n_tiles),
        grid=(n_prep + bsz * n_tiles,),
        in_specs=[spec for _, spec in operands],
        out_specs=out_specs,
        out_shape=out_shapes,
        scratch_shapes=[pltpu.VMEM((d, packed_width), BF16),
                        pltpu.VMEM(w_pool.shape, BF16),
                        pltpu.VMEM((GMLP_BLOCK, d), F32),
                        pltpu.VMEM((n_ada, bsz, ADA_TILE), F32),
                        pltpu.VMEM((POOL_HALO, w_pool.shape[0] * w_pool.shape[1]), F32)],
        compiler_params=pltpu.CompilerParams(
            dimension_semantics=("arbitrary",), vmem_limit_bytes=VMEM_LIMIT_BYTES),
        name="mixer",
    )(*[a for a, _ in operands])


def _interleave_rows(x):
    n, d = x.shape
    return jnp.swapaxes(x.reshape(SUBLANES, n // SUBLANES, d), 0, 1).reshape(n, d)


def _deinterleave_rows(y):
    n, d = y.shape
    return jnp.swapaxes(y.reshape(n // SUBLANES, SUBLANES, d), 0, 1).reshape(n, d)


def _delay_rows(a, prev_row):
    n = a.shape[0]
    wrapped = pltpu.roll(a[n - SUBLANES:, :], 1, 0)
    first = jnp.where(lax.broadcasted_iota(jnp.int32, wrapped.shape, 0) == 0, prev_row, wrapped)
    return jnp.concatenate([first, a[:n - SUBLANES, :]], axis=0)


def _channel_kernel(x_ref, mod_ref, g2_ref, w_up_ref, cw_ref, cb_ref, w_down_ref, gf_ref,
                    o_ref, conv_carry_ref, *, final_norm):
    ts, d = CHANNEL_TILE, x_ref.shape[2]
    d_ff = w_down_ref.shape[0]
    n_chunks = d_ff // FF_CHUNK

    @pl.when(pl.program_id(1) == 0)
    def _():
        conv_carry_ref[...] = jnp.zeros_like(conv_carry_ref)

    sh2, sc2, gt2 = (_mod_row(mod_ref, pl.program_id(0), k, d) for k in (3, 4, 5))

    def conv_cols(pre, c0):
        cols = pl.ds(c0, FF_CHUNK)
        taps = [pre]
        for m in range(1, CONV_WIDTH):
            r = (CONV_WIDTH - 1 - m) * SUBLANES + SUBLANES - 1
            taps.append(_delay_rows(taps[-1], conv_carry_ref[r:r + 1, cols]))
        conv_carry_ref[:, cols] = pre[ts - CONV_CARRY_ROWS:, :]
        out = cb_ref[:, cols]
        for k in range(CONV_WIDTH):
            out = out + taps[CONV_WIDTH - 1 - k] * cw_ref[k:k + 1, cols]
        return out

    def token_tile(r0):
        x = x_ref[0, r0:r0 + ts, :]
        h2 = _rms_scale(x) * (g2_ref[...] * (1.0 + sc2)) + sh2
        h2b = h2.astype(BF16)

        def up_project(c):
            return [_dot(h2b, w_up_ref[:, pl.ds(c0, FF_CHUNK)]) for c0 in (c * FF_CHUNK, d_ff + c * FF_CHUNK)]

        ahead = [up_project(i) for i in range(min(UP_LOOKAHEAD, n_chunks))]
        acc = jnp.zeros((ts, d), F32)
        for c in range(n_chunks):
            cur = ahead.pop(0)
            if c + UP_LOOKAHEAD < n_chunks:
                ahead.append(up_project(c + UP_LOOKAHEAD))
            gate = conv_cols(cur[0], c * FF_CHUNK)
            val = conv_cols(cur[1], d_ff + c * FF_CHUNK)
            f = gate * jax.nn.sigmoid(gate) * val
            acc = acc + _dot_ref(f.astype(BF16), w_down_ref.at[c * FF_CHUNK:(c + 1) * FF_CHUNK, 0:d])
        x2 = x + gt2 * acc
        if final_norm:
            x2 = _rms_scale(x2) * gf_ref[...]
        o_ref[0, r0:r0 + ts, :] = _deinterleave_rows(x2)

    for r0 in range(0, x_ref.shape[1], ts):
        token_tile(r0)


def _channel_call(x, mod, g2, w_up, conv_w, conv_b, w_down, g_final, final_norm):
    bsz, s_len, d = x.shape
    ts = CHANNEL_TILE * CHANNEL_TILES_PER_STEP
    assert s_len % ts == 0 and CHANNEL_TILE >= CONV_WIDTH * SUBLANES and w_down.shape[0] % FF_CHUNK == 0
    row = lambda a: a.reshape(1, -1)
    operands = [
        (x, pl.BlockSpec((1, ts, d), lambda b, j: (b, j, 0))),
        (mod, _resident(mod.shape)),
    ]
    for a in (row(g2), w_up, conv_w, row(conv_b), w_down, row(g_final)):
        if a is conv_w:
            spec = pl.BlockSpec((None,) + a.shape[1:], lambda b, j: (0, 0, 0), pipeline_mode=pl.Buffered(1))
        else:
            spec = _resident(a.shape)
        operands.append((a, spec))
    return pl.pallas_call(
        functools.partial(_channel_kernel, final_norm=final_norm),
        grid=(bsz, s_len // ts),
        in_specs=[spec for _, spec in operands],
        out_specs=pl.BlockSpec((1, ts, d), lambda b, j: (b, j, 0)),
        out_shape=jax.ShapeDtypeStruct(x.shape, F32),
        scratch_shapes=[pltpu.VMEM((CONV_CARRY_ROWS, conv_w.shape[2]), F32)],
        compiler_params=pltpu.CompilerParams(
            dimension_semantics=("arbitrary", "arbitrary"), vmem_limit_bytes=VMEM_LIMIT_BYTES),
        name="channel",
    )(*[a for a, _ in operands])


def kernel(x, c, w_ada, b_ada, g_norm1, w_in, ln_v_g, ln_v_b, w_spatial, b_spatial, w_pool, b_pool,
           pool_scale, w_proj_a, w_proj_b, w_gate, b_gate, w_out, g_norm2, w_up, conv_w, conv_b,
           w_down, g_final):
    depth = w_ada.shape[0]
    bsz, s_len, d = x.shape
    for l in range(depth):
        x, w_up_bf, w_down_bf, mod = _mixer_call(
            x, c, w_ada[l], b_ada[l], g_norm1[l], w_in[l], w_gate[l], b_gate[l], ln_v_g[l], ln_v_b[l], w_spatial[l],
            b_spatial[l], w_pool[l], b_pool[l], pool_scale[l], w_proj_a[l], w_proj_b[l], w_out[l],
            w_up[l], w_down[l])
        x = _channel_call(
            x, mod, g_norm2[l], w_up_bf, conv_w[l:l + 1], conv_b[l], w_down_bf, g_final,
            final_norm=(l == depth - 1))
    return x
```

```python
import functools
import math

import jax
import jax.numpy as jnp
from jax import lax
from jax.experimental import pallas as pl
from jax.experimental.pallas import tpu as pltpu

EPS = 1e-6
CHUNK = 64
GMLP_BLOCK = 128
A_GROUPS = 8
POOL_WINDOWS = (2, 4, 8, 16)
CONV_WIDTH = 3
N_ADA = 6

SUBLANES = 8
LANES = 128
BF16_TILE_ROWS = 16
POOL_HALO = 16
CONV_CARRY_ROWS = (CONV_WIDTH - 1) * SUBLANES
MXU_WIDTH = 256
MIXER_TILE = 256
MIXER_TILES_PER_STEP = 4
MIXER_PREP_STEPS = 16
CHANNEL_TILE = 256
CHANNEL_TILES_PER_STEP = 2
FF_CHUNK = 256
UP_LOOKAHEAD = 7
ADA_TILE = 512
VMEM_LIMIT_BYTES = 56 * 1024 * 1024

BF16 = jnp.bfloat16
F32 = jnp.float32


def _dot(a, b):
    return jnp.dot(a, b, preferred_element_type=F32)


def _dot_ref(a, w_ref):
    n = w_ref.shape[-1]
    assert n % MXU_WIDTH == 0
    return jnp.concatenate(
        [_dot(a, w_ref[:, c0:c0 + MXU_WIDTH]) for c0 in range(0, n, MXU_WIDTH)], axis=1)


def _gelu(x):
    return 0.5 * x * (1.0 + lax.erf(x * math.sqrt(0.5)))


def _rms_scale(x):
    ms = jnp.mean(x * x, axis=-1, keepdims=True)
    return x * lax.rsqrt(ms + EPS)


def _mod_row(mod_ref, b, k, d):
    per = d // ADA_TILE
    return jnp.concatenate([mod_ref[k * per + i, pl.ds(b, 1), :] for i in range(per)], axis=1)


def _cast_block(src_ref, dst_ref):
    n = src_ref.shape[1]
    dst_ref[:, 0:n] = src_ref[...].astype(BF16)
    if dst_ref.shape[1] > n:
        dst_ref[:, n:] = jnp.zeros((dst_ref.shape[0], dst_ref.shape[1] - n), BF16)


def _mixer_kernel(x_ref, c_ref, w_ada_ref, b_ada_ref, g1_ref, b_gate_ref, lng_ref, lnb_ref, wsp_ref, b_sp_ref,
                  wpool_f32, bpool_ref, pscale_ref, w_in_f32, w_gate_f32, wpa_f32, wpb_f32, wout_f32,
                  w_up_ref, w_down_ref, o_ref, w_up_bf_ref, w_down_bf_ref, mod_out_ref, w_ref, wpool_ref, bsp_ref,
                  mod_ref, pool_carry_ref, *, n_prep, n_tiles):
    step = pl.program_id(0)

    @pl.when(step == 0)
    def _():
        wpool_ref[...] = wpool_f32[...].astype(BF16)
        gd = bsp_ref.shape[1] // A_GROUPS
        for g in range(A_GROUPS):
            bsp_ref[:, g * gd:(g + 1) * gd] = jnp.broadcast_to(b_sp_ref[g:g + 1, :], (gd, GMLP_BLOCK)).T

    @pl.when(step < mod_ref.shape[0])
    def _():
        c = c_ref[...]
        s = (c * jax.nn.sigmoid(c)).astype(BF16)
        mod_ref[step] = _dot(s, w_ada_ref[...].astype(BF16)) + b_ada_ref[...]

    @pl.when(step < n_prep)
    def _():
        rows = w_in_f32.shape[0]
        r0 = pl.multiple_of(step * rows, rows)
        off = 0
        for src in (w_in_f32, w_gate_f32, wpa_f32, wpb_f32, wout_f32):
            w_ref[pl.ds(r0, rows), off:off + src.shape[1]] = src[...].astype(BF16)
            off += src.shape[1]

    @pl.when(step >= n_prep)
    def _():
        @pl.when(step == n_prep)
        def _():
            mod_out_ref[...] = mod_ref[...]

        _mixer_tokens(x_ref, mod_ref, g1_ref, b_gate_ref, lng_ref, lnb_ref, wsp_ref, bsp_ref, wpool_ref,
                      bpool_ref, pscale_ref, w_up_ref, w_down_ref, o_ref, w_up_bf_ref, w_down_bf_ref,
                      w_ref, pool_carry_ref, in_width=w_in_f32.shape[1], j=(step - n_prep) % n_tiles,
                      b=(step - n_prep) // n_tiles)


def _mixer_tokens(x_ref, mod_ref, g1_ref, b_gate_ref, lng_ref, lnb_ref, wsp_ref, bsp_ref, wpool_ref,
                  bpool_ref, pscale_ref, w_up_ref, w_down_ref, o_ref, w_up_bf_ref, w_down_bf_ref,
                  w_ref, pool_carry_ref, *, in_width, j, b):
    ts, d = MIXER_TILE, x_ref.shape[2]
    _cast_block(w_up_ref, w_up_bf_ref)
    _cast_block(w_down_ref, w_down_bf_ref)
    w_in_ref = w_ref.at[:, 0:in_width]
    w_gate_ref = w_ref.at[:, in_width:in_width + 2 * d]
    wpa_ref = w_ref.at[:, in_width + 2 * d:in_width + 3 * d]
    wpb_ref = w_ref.at[:, in_width + 3 * d:in_width + 4 * d]
    wout_ref = w_ref.at[:, in_width + 4 * d:in_width + 5 * d]

    @pl.when(j == 0)
    def _():
        pool_carry_ref[...] = jnp.zeros_like(pool_carry_ref)

    sh1, sc1, gt1 = (_mod_row(mod_ref, b, k, d) for k in (0, 1, 2))
    a_w = d
    gd = a_w // A_GROUPS
    p = lax.broadcasted_iota(jnp.int32, (GMLP_BLOCK, GMLP_BLOCK), 0)
    q = lax.broadcasted_iota(jnp.int32, (GMLP_BLOCK, GMLP_BLOCK), 1)
    allowed = (q // CHUNK) <= (p // CHUNK)
    w_masked = [jnp.where(allowed, wsp_ref[g], 0.0).astype(BF16) for g in range(A_GROUPS)]

    def token_tile(row0):
        x = x_ref[0, row0:row0 + ts, :]
        h = _rms_scale(x) * (g1_ref[...] * (1.0 + sc1)) + sh1
        hb = h.astype(BF16)

        v = _gelu(_dot_ref(hb, w_in_ref.at[:, a_w:2 * a_w]))
        hbp = _dot_ref(hb, w_in_ref.at[:, 2 * a_w:w_in_ref.shape[1]])
        gates = jax.nn.sigmoid(_dot_ref(hb, w_gate_ref) + b_gate_ref[...])
        u = _gelu(_dot_ref(hb, w_in_ref.at[:, 0:a_w]))
        mu = jnp.mean(v, axis=-1, keepdims=True)
        vc = v - mu
        var = jnp.mean(vc * vc, axis=-1, keepdims=True)
        vn = (vc * lax.rsqrt(var + EPS)) * lng_ref[...] + lnb_ref[...]
        vnb = vn.astype(BF16)

        ext = jnp.concatenate([pool_carry_ref[...], hbp], axis=0)
        pool_carry_ref[...] = hbp[ts - POOL_HALO:, :]
        bgd = hbp.shape[1] // len(POOL_WINDOWS)
        t = j * x_ref.shape[1] + row0 + lax.broadcasted_iota(jnp.int32, (ts, bgd), 0)
        yb_cols = []
        for gi, w in enumerate(POOL_WINDOWS):
            e = ext[:, gi * bgd:(gi + 1) * bgd]
            acc, span = e, 1
            while span < w:
                acc = acc + pltpu.roll(acc, span, 0)
                span *= 2
            win = acc[POOL_HALO:, :]
            cur = e[POOL_HALO:, :]
            count = jnp.minimum(t + 1, w).astype(F32)
            pooled = win / count - cur
            mixed = _dot(pooled.astype(BF16), wpool_ref[gi]) + bpool_ref[gi:gi + 1, :]
            yb_cols.append(mixed)
        y_b = jnp.concatenate(yb_cols, axis=1) * pscale_ref[...]
        branch_b = gates[:, d:] * _dot_ref(y_b.astype(BF16), wpb_ref)

        bias_map = bsp_ref[...]
        s_rows = []
        for n in range(ts // GMLP_BLOCK):
            b0 = n * GMLP_BLOCK
            cols = [_dot(w_masked[g], vnb[b0:b0 + GMLP_BLOCK, g * gd:(g + 1) * gd]) for g in range(A_GROUPS)]
            s_rows.append(jnp.concatenate(cols, axis=1) + bias_map)
        s = jnp.concatenate(s_rows, axis=0)
        y_a = u * s

        merged = gates[:, :d] * _dot_ref(y_a.astype(BF16), wpa_ref) + branch_b
        x1 = x + gt1 * _dot_ref(merged.astype(BF16), wout_ref)
        o_ref[0, row0:row0 + ts, :] = jnp.concatenate(
            [_interleave_rows(x1[r:r + CHANNEL_TILE, :]) for r in range(0, ts, CHANNEL_TILE)], axis=0)

    for row0 in range(0, x_ref.shape[1], ts):
        token_tile(row0)


def _resident(shape):
    nd = len(shape)
    return pl.BlockSpec(shape, lambda b, j: (0,) * nd, pipeline_mode=pl.Buffered(1))


def _pitch_pad(n_cols):
    return LANES if (n_cols // LANES) % SUBLANES == 0 else 0


def _row_blocking(n_rows, n_steps):
    for steps_per_block in range(1, n_steps + 1):
        if n_steps % steps_per_block == 0 and (n_rows * steps_per_block) % n_steps == 0:
            rows = n_rows * steps_per_block // n_steps
            if rows % BF16_TILE_ROWS == 0:
                return rows, steps_per_block
    raise ValueError(f"no bf16-aligned row blocking of {n_rows} rows over {n_steps} steps")


def _mixer_call(x, c, w_ada, b_ada, g1, w_in, w_gate, b_gate, ln_g, ln_b, w_sp, b_sp, w_pool, b_pool,
                pool_scale, w_pa, w_pb, w_out, w_up, w_down):
    bsz, s_len, d = x.shape
    ts = MIXER_TILE * MIXER_TILES_PER_STEP
    n_tiles = s_len // ts
    n_prep = MIXER_PREP_STEPS
    assert s_len % ts == 0 and MIXER_TILE % GMLP_BLOCK == 0 and MIXER_TILE >= POOL_HALO
    assert MIXER_TILE % CHANNEL_TILE == 0
    assert w_gate.shape[1] == 2 * d and w_pa.shape == w_pb.shape == w_out.shape == (d, d)
    assert d % (n_prep * BF16_TILE_ROWS) == 0
    row = lambda a: a.reshape(1, -1)

    def token_step(t):
        return jnp.maximum(t - n_prep, 0)

    def token_block(t):
        return (token_step(t) // n_tiles, token_step(t) % n_tiles, 0)

    def resident(a):
        nd = a.ndim
        return pl.BlockSpec(a.shape, lambda t: (0,) * nd, pipeline_mode=pl.Buffered(1))

    n_ada = w_ada.shape[1] // ADA_TILE
    assert w_ada.shape[1] == N_ADA * d and d % ADA_TILE == 0 and n_ada <= n_prep
    ada_block = lambda t: (0, jnp.minimum(t, n_ada - 1))
    operands = [
        (x, pl.BlockSpec((1, ts, d), token_block)),
        (c, pl.BlockSpec(c.shape, lambda t: (0, 0))),
        (w_ada, pl.BlockSpec((d, ADA_TILE), ada_block)),
        (row(b_ada), pl.BlockSpec((1, ADA_TILE), ada_block)),
    ]
    for a in (row(g1), row(b_gate), row(ln_g), row(ln_b), w_sp, b_sp, w_pool, b_pool,
              row(pool_scale)):
        operands.append((a, resident(a)))
    mixer_weights = (w_in, w_gate, w_pa, w_pb, w_out)
    for w in mixer_weights:
        operands.append((w, pl.BlockSpec((d // n_prep, w.shape[1]), lambda t: (jnp.minimum(t, n_prep - 1), 0))))
    out_specs = [pl.BlockSpec((1, ts, d), token_block)]
    out_shapes = [jax.ShapeDtypeStruct(x.shape, F32)]
    for w in (w_up, w_down):
        rows, steps_per_block = _row_blocking(w.shape[0], bsz * n_tiles)
        index_map = lambda t, spb=steps_per_block: (token_step(t) // spb, 0)
        operands.append((w, pl.BlockSpec((rows, w.shape[1]), index_map)))
        padded = w.shape[1] + _pitch_pad(w.shape[1])
        out_specs.append(pl.BlockSpec((rows, padded), index_map))
        out_shapes.append(jax.ShapeDtypeStruct((w.shape[0], padded), BF16))
    out_specs.append(pl.BlockSpec((n_ada, bsz, ADA_TILE), lambda t: (0, 0, 0)))
    out_shapes.append(jax.ShapeDtypeStruct((n_ada, bsz, ADA_TILE), F32))
    packed_width = sum(w.shape[1] for w in mixer_weights)
    packed_width += _pitch_pad(packed_width)
    return pl.pallas_call(
        functools.partial(_mixer_kernel, n_prep=n_prep, n_tiles=n_tiles),
        grid=(n_prep + bsz * n_tiles,),
        in_specs=[spec for _, spec in operands],
        out_specs=out_specs,
        out_shape=out_shapes,
        scratch_shapes=[pltpu.VMEM((d, packed_width), BF16),
                        pltpu.VMEM(w_pool.shape, BF16),
                        pltpu.VMEM((GMLP_BLOCK, d), F32),
                        pltpu.VMEM((n_ada, bsz, ADA_TILE), F32),
                        pltpu.VMEM((POOL_HALO, w_pool.shape[0] * w_pool.shape[1]), F32)],
        compiler_params=pltpu.CompilerParams(
            dimension_semantics=("arbitrary",), vmem_limit_bytes=VMEM_LIMIT_BYTES),
        name="mixer",
    )(*[a for a, _ in operands])


def _interleave_rows(x):
    n, d = x.shape
    return jnp.swapaxes(x.reshape(SUBLANES, n // SUBLANES, d), 0, 1).reshape(n, d)


def _deinterleave_rows(y):
    n, d = y.shape
    return jnp.swapaxes(y.reshape(n // SUBLANES, SUBLANES, d), 0, 1).reshape(n, d)


def _delay_rows(a, prev_row):
    n = a.shape[0]
    wrapped = pltpu.roll(a[n - SUBLANES:, :], 1, 0)
    first = jnp.where(lax.broadcasted_iota(jnp.int32, wrapped.shape, 0) == 0, prev_row, wrapped)
    return jnp.concatenate([first, a[:n - SUBLANES, :]], axis=0)


def _channel_kernel(x_ref, mod_ref, g2_ref, w_up_ref, cw_ref, cb_ref, w_down_ref, gf_ref,
                    o_ref, conv_carry_ref, *, final_norm):
    ts, d = CHANNEL_TILE, x_ref.shape[2]
    d_ff = w_down_ref.shape[0]
    n_chunks = d_ff // FF_CHUNK

    @pl.when(pl.program_id(1) == 0)
    def _():
        conv_carry_ref[...] = jnp.zeros_like(conv_carry_ref)

    sh2, sc2, gt2 = (_mod_row(mod_ref, pl.program_id(0), k, d) for k in (3, 4, 5))

    def conv_cols(pre, c0):
        cols = pl.ds(c0, FF_CHUNK)
        taps = [pre]
        for m in range(1, CONV_WIDTH):
            r = (CONV_WIDTH - 1 - m) * SUBLANES + SUBLANES - 1
            taps.append(_delay_rows(taps[-1], conv_carry_ref[r:r + 1, cols]))
        conv_carry_ref[:, cols] = pre[ts - CONV_CARRY_ROWS:, :]
        out = cb_ref[:, cols]
        for k in range(CONV_WIDTH):
            out = out + taps[CONV_WIDTH - 1 - k] * cw_ref[k:k + 1, cols]
        return out

    def token_tile(r0):
        x = x_ref[0, r0:r0 + ts, :]
        h2 = _rms_scale(x) * (g2_ref[...] * (1.0 + sc2)) + sh2
        h2b = h2.astype(BF16)

        def up_project(c):
            return [_dot(h2b, w_up_ref[:, pl.ds(c0, FF_CHUNK)]) for c0 in (c * FF_CHUNK, d_ff + c * FF_CHUNK)]

        ahead = [up_project(i) for i in range(min(UP_LOOKAHEAD, n_chunks))]
        acc = jnp.zeros((ts, d), F32)
        for c in range(n_chunks):
            cur = ahead.pop(0)
            if c + UP_LOOKAHEAD < n_chunks:
                ahead.append(up_project(c + UP_LOOKAHEAD))
            gate = conv_cols(cur[0], c * FF_CHUNK)
            val = conv_cols(cur[1], d_ff + c * FF_CHUNK)
            f = gate * jax.nn.sigmoid(gate) * val
            acc = acc + _dot_ref(f.astype(BF16), w_down_ref.at[c * FF_CHUNK:(c + 1) * FF_CHUNK, 0:d])
        x2 = x + gt2 * acc
        if final_norm:
            x2 = _rms_scale(x2) * gf_ref[...]
        o_ref[0, r0:r0 + ts, :] = _deinterleave_rows(x2)

    for r0 in range(0, x_ref.shape[1], ts):
        token_tile(r0)


def _channel_call(x, mod, g2, w_up, conv_w, conv_b, w_down, g_final, final_norm):
    bsz, s_len, d = x.shape
    ts = CHANNEL_TILE * CHANNEL_TILES_PER_STEP
    assert s_len % ts == 0 and CHANNEL_TILE >= CONV_WIDTH * SUBLANES and w_down.shape[0] % FF_CHUNK == 0
    row = lambda a: a.reshape(1, -1)
    operands = [
        (x, pl.BlockSpec((1, ts, d), lambda b, j: (b, j, 0))),
        (mod, _resident(mod.shape)),
    ]
    for a in (row(g2), w_up, conv_w, row(conv_b), w_down, row(g_final)):
        if a is conv_w:
            spec = pl.BlockSpec((None,) + a.shape[1:], lambda b, j: (0, 0, 0), pipeline_mode=pl.Buffered(1))
        else:
            spec = _resident(a.shape)
        operands.append((a, spec))
    return pl.pallas_call(
        functools.partial(_channel_kernel, final_norm=final_norm),
        grid=(bsz, s_len // ts),
        in_specs=[spec for _, spec in operands],
        out_specs=pl.BlockSpec((1, ts, d), lambda b, j: (b, j, 0)),
        out_shape=jax.ShapeDtypeStruct(x.shape, F32),
        scratch_shapes=[pltpu.VMEM((CONV_CARRY_ROWS, conv_w.shape[2]), F32)],
        compiler_params=pltpu.CompilerParams(
            dimension_semantics=("arbitrary", "arbitrary"), vmem_limit_bytes=VMEM_LIMIT_BYTES),
        name="channel",
    )(*[a for a, _ in operands])


def kernel(x, c, w_ada, b_ada, g_norm1, w_in, ln_v_g, ln_v_b, w_spatial, b_spatial, w_pool, b_pool,
           pool_scale, w_proj_a, w_proj_b, w_gate, b_gate, w_out, g_norm2, w_up, conv_w, conv_b,
           w_down, g_final):
    depth = w_ada.shape[0]
    bsz, s_len, d = x.shape
    for l in range(depth):
        x, w_up_bf, w_down_bf, mod = _mixer_call(
            x, c, w_ada[l], b_ada[l], g_norm1[l], w_in[l], w_gate[l], b_gate[l], ln_v_g[l], ln_v_b[l], w_spatial[l],
            b_spatial[l], w_pool[l], b_pool[l], pool_scale[l], w_proj_a[l], w_proj_b[l], w_out[l],
            w_up[l], w_down[l])
        x = _channel_call(
            x, mod, g_norm2[l], w_up_bf, conv_w[l:l + 1], conv_b[l], w_down_bf, g_final,
            final_norm=(l == depth - 1))
    return x
```

```python
import functools
import math

import jax
import jax.numpy as jnp
from jax import lax
from jax.experimental import pallas as pl
from jax.experimental.pallas import tpu as pltpu

EPS = 1e-6
CHUNK = 64
GMLP_BLOCK = 128
A_GROUPS = 8
POOL_WINDOWS = (2, 4, 8, 16)
CONV_WIDTH = 3
N_ADA = 6

SUBLANES = 8
LANES = 128
BF16_TILE_ROWS = 16
POOL_HALO = 16
CONV_CARRY_ROWS = (CONV_WIDTH - 1) * SUBLANES
MXU_WIDTH = 256
MIXER_TILE = 256
MIXER_TILES_PER_STEP = 4
MIXER_PREP_STEPS = 16
CHANNEL_TILE = 256
CHANNEL_TILES_PER_STEP = 2
FF_CHUNK = 256
UP_LOOKAHEAD = 5
ADA_TILE = 512
VMEM_LIMIT_BYTES = 56 * 1024 * 1024

BF16 = jnp.bfloat16
F32 = jnp.float32


def _dot(a, b):
    return jnp.dot(a, b, preferred_element_type=F32)


def _dot_ref(a, w_ref):
    n = w_ref.shape[-1]
    assert n % MXU_WIDTH == 0
    return jnp.concatenate(
        [_dot(a, w_ref[:, c0:c0 + MXU_WIDTH]) for c0 in range(0, n, MXU_WIDTH)], axis=1)


def _gelu(x):
    return 0.5 * x * (1.0 + lax.erf(x * math.sqrt(0.5)))


def _rms_scale(x):
    ms = jnp.mean(x * x, axis=-1, keepdims=True)
    return x * lax.rsqrt(ms + EPS)


def _mod_row(mod_ref, b, k, d):
    per = d // ADA_TILE
    return jnp.concatenate([mod_ref[k * per + i, pl.ds(b, 1), :] for i in range(per)], axis=1)


def _cast_block(src_ref, dst_ref):
    n = src_ref.shape[1]
    dst_ref[:, 0:n] = src_ref[...].astype(BF16)
    if dst_ref.shape[1] > n:
        dst_ref[:, n:] = jnp.zeros((dst_ref.shape[0], dst_ref.shape[1] - n), BF16)


def _mixer_kernel(x_ref, c_ref, w_ada_ref, b_ada_ref, g1_ref, b_gate_ref, lng_ref, lnb_ref, wsp_ref, b_sp_ref,
                  wpool_f32, bpool_ref, pscale_ref, w_in_f32, w_gate_f32, wpa_f32, wpb_f32, wout_f32,
                  w_up_ref, w_down_ref, o_ref, w_up_bf_ref, w_down_bf_ref, mod_out_ref, w_ref, wpool_ref, bsp_ref,
                  mod_ref, pool_carry_ref, *, n_prep, n_tiles):
    step = pl.program_id(0)

    @pl.when(step == 0)
    def _():
        wpool_ref[...] = wpool_f32[...].astype(BF16)
        gd = bsp_ref.shape[1] // A_GROUPS
        for g in range(A_GROUPS):
            bsp_ref[:, g * gd:(g + 1) * gd] = jnp.broadcast_to(b_sp_ref[g:g + 1, :], (gd, GMLP_BLOCK)).T

    @pl.when(step < mod_ref.shape[0])
    def _():
        c = c_ref[...]
        s = (c * jax.nn.sigmoid(c)).astype(BF16)
        mod_ref[step] = _dot(s, w_ada_ref[...].astype(BF16)) + b_ada_ref[...]

    @pl.when(step < n_prep)
    def _():
        rows = w_in_f32.shape[0]
        r0 = pl.multiple_of(step * rows, rows)
        off = 0
        for src in (w_in_f32, w_gate_f32, wpa_f32, wpb_f32, wout_f32):
            w_ref[pl.ds(r0, rows), off:off + src.shape[1]] = src[...].astype(BF16)
            off += src.shape[1]

    @pl.when(step >= n_prep)
    def _():
        @pl.when(step == n_prep)
        def _():
            mod_out_ref[...] = mod_ref[...]

        _mixer_tokens(x_ref, mod_ref, g1_ref, b_gate_ref, lng_ref, lnb_ref, wsp_ref, bsp_ref, wpool_ref,
                      bpool_ref, pscale_ref, w_up_ref, w_down_ref, o_ref, w_up_bf_ref, w_down_bf_ref,
                      w_ref, pool_carry_ref, in_width=w_in_f32.shape[1], j=(step - n_prep) % n_tiles,
                      b=(step - n_prep) // n_tiles)


def _mixer_tokens(x_ref, mod_ref, g1_ref, b_gate_ref, lng_ref, lnb_ref, wsp_ref, bsp_ref, wpool_ref,
                  bpool_ref, pscale_ref, w_up_ref, w_down_ref, o_ref, w_up_bf_ref, w_down_bf_ref,
                  w_ref, pool_carry_ref, *, in_width, j, b):
    ts, d = MIXER_TILE, x_ref.shape[2]
    _cast_block(w_up_ref, w_up_bf_ref)
    _cast_block(w_down_ref, w_down_bf_ref)
    w_in_ref = w_ref.at[:, 0:in_width]
    w_gate_ref = w_ref.at[:, in_width:in_width + 2 * d]
    wpa_ref = w_ref.at[:, in_width + 2 * d:in_width + 3 * d]
    wpb_ref = w_ref.at[:, in_width + 3 * d:in_width + 4 * d]
    wout_ref = w_ref.at[:, in_width + 4 * d:in_width + 5 * d]

    @pl.when(j == 0)
    def _():
        pool_carry_ref[...] = jnp.zeros_like(pool_carry_ref)

    sh1, sc1, gt1 = (_mod_row(mod_ref, b, k, d) for k in (0, 1, 2))
    a_w = d
    gd = a_w // A_GROUPS
    p = lax.broadcasted_iota(jnp.int32, (GMLP_BLOCK, GMLP_BLOCK), 0)
    q = lax.broadcasted_iota(jnp.int32, (GMLP_BLOCK, GMLP_BLOCK), 1)
    allowed = (q // CHUNK) <= (p // CHUNK)
    w_masked = [jnp.where(allowed, wsp_ref[g], 0.0).astype(BF16) for g in range(A_GROUPS)]

    def token_tile(row0):
        x = x_ref[0, row0:row0 + ts, :]
        h = _rms_scale(x) * (g1_ref[...] * (1.0 + sc1)) + sh1
        hb = h.astype(BF16)

        v = _gelu(_dot_ref(hb, w_in_ref.at[:, a_w:2 * a_w]))
        hbp = _dot_ref(hb, w_in_ref.at[:, 2 * a_w:w_in_ref.shape[1]])
        gates = jax.nn.sigmoid(_dot_ref(hb, w_gate_ref) + b_gate_ref[...])
        u = _gelu(_dot_ref(hb, w_in_ref.at[:, 0:a_w]))
        mu = jnp.mean(v, axis=-1, keepdims=True)
        vc = v - mu
        var = jnp.mean(vc * vc, axis=-1, keepdims=True)
        vn = (vc * lax.rsqrt(var + EPS)) * lng_ref[...] + lnb_ref[...]
        vnb = vn.astype(BF16)

        ext = jnp.concatenate([pool_carry_ref[...], hbp], axis=0)
        pool_carry_ref[...] = hbp[ts - POOL_HALO:, :]
        bgd = hbp.shape[1] // len(POOL_WINDOWS)
        t = j * x_ref.shape[1] + row0 + lax.broadcasted_iota(jnp.int32, (ts, bgd), 0)
        yb_cols = []
        for gi, w in enumerate(POOL_WINDOWS):
            e = ext[:, gi * bgd:(gi + 1) * bgd]
            acc, span = e, 1
            while span < w:
                acc = acc + pltpu.roll(acc, span, 0)
                span *= 2
            win = acc[POOL_HALO:, :]
            cur = e[POOL_HALO:, :]
            count = jnp.minimum(t + 1, w).astype(F32)
            pooled = win / count - cur
            mixed = _dot(pooled.astype(BF16), wpool_ref[gi]) + bpool_ref[gi:gi + 1, :]
            yb_cols.append(mixed)
        y_b = jnp.concatenate(yb_cols, axis=1) * pscale_ref[...]
        branch_b = gates[:, d:] * _dot_ref(y_b.astype(BF16), wpb_ref)

        bias_map = bsp_ref[...]
        s_rows = []
        for n in range(ts // GMLP_BLOCK):
            b0 = n * GMLP_BLOCK
            cols = [_dot(w_masked[g], vnb[b0:b0 + GMLP_BLOCK, g * gd:(g + 1) * gd]) for g in range(A_GROUPS)]
            s_rows.append(jnp.concatenate(cols, axis=1) + bias_map)
        s = jnp.concatenate(s_rows, axis=0)
        y_a = u * s

        merged = gates[:, :d] * _dot_ref(y_a.astype(BF16), wpa_ref) + branch_b
        x1 = x + gt1 * _dot_ref(merged.astype(BF16), wout_ref)
        o_ref[0, row0:row0 + ts, :] = jnp.concatenate(
            [_interleave_rows(x1[r:r + CHANNEL_TILE, :]) for r in range(0, ts, CHANNEL_TILE)], axis=0)

    for row0 in range(0, x_ref.shape[1], ts):
        token_tile(row0)


def _resident(shape):
    nd = len(shape)
    return pl.BlockSpec(shape, lambda b, j: (0,) * nd, pipeline_mode=pl.Buffered(1))


def _pitch_pad(n_cols):
    return LANES if (n_cols // LANES) % SUBLANES == 0 else 0


def _row_blocking(n_rows, n_steps):
    for steps_per_block in range(1, n_steps + 1):
        if n_steps % steps_per_block == 0 and (n_rows * steps_per_block) % n_steps == 0:
            rows = n_rows * steps_per_block // n_steps
            if rows % BF16_TILE_ROWS == 0:
                return rows, steps_per_block
    raise ValueError(f"no bf16-aligned row blocking of {n_rows} rows over {n_steps} steps")


def _mixer_call(x, c, w_ada, b_ada, g1, w_in, w_gate, b_gate, ln_g, ln_b, w_sp, b_sp, w_pool, b_pool,
                pool_scale, w_pa, w_pb, w_out, w_up, w_down):
    bsz, s_len, d = x.shape
    ts = MIXER_TILE * MIXER_TILES_PER_STEP
    n_tiles = s_len // ts
    n_prep = MIXER_PREP_STEPS
    assert s_len % ts == 0 and MIXER_TILE % GMLP_BLOCK == 0 and MIXER_TILE >= POOL_HALO
    assert MIXER_TILE % CHANNEL_TILE == 0
    assert w_gate.shape[1] == 2 * d and w_pa.shape == w_pb.shape == w_out.shape == (d, d)
    assert d % (n_prep * BF16_TILE_ROWS) == 0
    row = lambda a: a.reshape(1, -1)

    def token_step(t):
        return jnp.maximum(t - n_prep, 0)

    def token_block(t):
        return (token_step(t) // n_tiles, token_step(t) % n_tiles, 0)

    def resident(a):
        nd = a.ndim
        return pl.BlockSpec(a.shape, lambda t: (0,) * nd, pipeline_mode=pl.Buffered(1))

    n_ada = w_ada.shape[1] // ADA_TILE
    assert w_ada.shape[1] == N_ADA * d and d % ADA_TILE == 0 and n_ada <= n_prep
    ada_block = lambda t: (0, jnp.minimum(t, n_ada - 1))
    operands = [
        (x, pl.BlockSpec((1, ts, d), token_block)),
        (c, pl.BlockSpec(c.shape, lambda t: (0, 0))),
        (w_ada, pl.BlockSpec((d, ADA_TILE), ada_block)),
        (row(b_ada), pl.BlockSpec((1, ADA_TILE), ada_block)),
    ]
    for a in (row(g1), row(b_gate), row(ln_g), row(ln_b), w_sp, b_sp, w_pool, b_pool,
              row(pool_scale)):
        operands.append((a, resident(a)))
    mixer_weights = (w_in, w_gate, w_pa, w_pb, w_out)
    for w in mixer_weights:
        operands.append((w, pl.BlockSpec((d // n_prep, w.shape[1]), lambda t: (jnp.minimum(t, n_prep - 1), 0))))
    out_specs = [pl.BlockSpec((1, ts, d), token_block)]
    out_shapes = [jax.ShapeDtypeStruct(x.shape, F32)]
    for w in (w_up, w_down):
        rows, steps_per_block = _row_blocking(w.shape[0], bsz * n_tiles)
        index_map = lambda t, spb=steps_per_block: (token_step(t) // spb, 0)
        operands.append((w, pl.BlockSpec((rows, w.shape[1]), index_map)))
        padded = w.shape[1] + _pitch_pad(w.shape[1])
        out_specs.append(pl.BlockSpec((rows, padded), index_map))
        out_shapes.append(jax.ShapeDtypeStruct((w.shape[0], padded), BF16))
    out_specs.append(pl.BlockSpec((n_ada, bsz, ADA_TILE), lambda t: (0, 0, 0)))
    out_shapes.append(jax.ShapeDtypeStruct((n_ada, bsz, ADA_TILE), F32))
    packed_width = sum(w.shape[1] for w in mixer_weights)
    packed_width += _pitch_pad(packed_width)
    return pl.pallas_call(
        functools.partial(_mixer_kernel, n_prep=n_prep, n_tiles=n_tiles),
        grid=(n_prep + bsz * n_tiles,),
        in_specs=[spec for _, spec in operands],
        out_specs=out_specs,
        out_shape=out_shapes,
        scratch_shapes=[pltpu.VMEM((d, packed_width), BF16),
                        pltpu.VMEM(w_pool.shape, BF16),
                        pltpu.VMEM((GMLP_BLOCK, d), F32),
                        pltpu.VMEM((n_ada, bsz, ADA_TILE), F32),
                        pltpu.VMEM((POOL_HALO, w_pool.shape[0] * w_pool.shape[1]), F32)],
        compiler_params=pltpu.CompilerParams(
            dimension_semantics=("arbitrary",), vmem_limit_bytes=VMEM_LIMIT_BYTES),
        name="mixer",
    )(*[a for a, _ in operands])


def _interleave_rows(x):
    n, d = x.shape
    return jnp.swapaxes(x.reshape(SUBLANES, n // SUBLANES, d), 0, 1).reshape(n, d)


def _deinterleave_rows(y):
    n, d = y.shape
    return jnp.swapaxes(y.reshape(n // SUBLANES, SUBLANES, d), 0, 1).reshape(n, d)


def _delay_rows(a, prev_row):
    n = a.shape[0]
    wrapped = pltpu.roll(a[n - SUBLANES:, :], 1, 0)
    first = jnp.where(lax.broadcasted_iota(jnp.int32, wrapped.shape, 0) == 0, prev_row, wrapped)
    return jnp.concatenate([first, a[:n - SUBLANES, :]], axis=0)


def _channel_kernel(x_ref, mod_ref, w_up_ref, w_down_ref, cw_ref, cb_ref, g2_ref, gf_ref,
                    o_ref, conv_carry_ref, *, final_norm):
    ts, d = CHANNEL_TILE, x_ref.shape[2]
    d_ff = w_down_ref.shape[0]
    n_chunks = d_ff // FF_CHUNK

    @pl.when(pl.program_id(1) == 0)
    def _():
        conv_carry_ref[...] = jnp.zeros_like(conv_carry_ref)

    sh2, sc2, gt2 = (_mod_row(mod_ref, pl.program_id(0), k, d) for k in (3, 4, 5))

    def conv_cols(pre, c0):
        cols = pl.ds(c0, FF_CHUNK)
        taps = [pre]
        for m in range(1, CONV_WIDTH):
            r = (CONV_WIDTH - 1 - m) * SUBLANES + SUBLANES - 1
            taps.append(_delay_rows(taps[-1], conv_carry_ref[r:r + 1, cols]))
        conv_carry_ref[:, cols] = pre[ts - CONV_CARRY_ROWS:, :]
        out = cb_ref[:, cols]
        for k in range(CONV_WIDTH):
            out = out + taps[CONV_WIDTH - 1 - k] * cw_ref[k:k + 1, cols]
        return out

    def token_tile(r0):
        x = x_ref[0, r0:r0 + ts, :]
        h2 = _rms_scale(x) * (g2_ref[...] * (1.0 + sc2)) + sh2
        h2b = h2.astype(BF16)

        def up_project(c):
            return [_dot(h2b, w_up_ref[:, pl.ds(c0, FF_CHUNK)]) for c0 in (c * FF_CHUNK, d_ff + c * FF_CHUNK)]

        ahead = [up_project(i) for i in range(min(UP_LOOKAHEAD, n_chunks))]
        acc = jnp.zeros((ts, d), F32)
        for c in range(n_chunks):
            cur = ahead.pop(0)
            if c + UP_LOOKAHEAD < n_chunks:
                ahead.append(up_project(c + UP_LOOKAHEAD))
            gate = conv_cols(cur[0], c * FF_CHUNK)
            val = conv_cols(cur[1], d_ff + c * FF_CHUNK)
            f = gate * jax.nn.sigmoid(gate) * val
            acc = acc + _dot_ref(f.astype(BF16), w_down_ref.at[c * FF_CHUNK:(c + 1) * FF_CHUNK, 0:d])
        x2 = x + gt2 * acc
        if final_norm:
            x2 = _rms_scale(x2) * gf_ref[...]
        o_ref[0, r0:r0 + ts, :] = _deinterleave_rows(x2)

    for r0 in range(0, x_ref.shape[1], ts):
        token_tile(r0)


def _channel_call(x, mod, g2, w_up, conv_w, conv_b, w_down, g_final, final_norm):
    bsz, s_len, d = x.shape
    ts = CHANNEL_TILE * CHANNEL_TILES_PER_STEP
    assert s_len % ts == 0 and CHANNEL_TILE >= CONV_WIDTH * SUBLANES and w_down.shape[0] % FF_CHUNK == 0
    row = lambda a: a.reshape(1, -1)
    operands = [
        (x, pl.BlockSpec((1, ts, d), lambda b, j: (b, j, 0))),
        (mod, _resident(mod.shape)),
    ]
    for a in (w_up, w_down, conv_w, row(conv_b), row(g2), row(g_final)):
        if a is conv_w:
            spec = pl.BlockSpec((None,) + a.shape[1:], lambda b, j: (0, 0, 0), pipeline_mode=pl.Buffered(1))
        else:
            spec = _resident(a.shape)
        operands.append((a, spec))
    return pl.pallas_call(
        functools.partial(_channel_kernel, final_norm=final_norm),
        grid=(bsz, s_len // ts),
        in_specs=[spec for _, spec in operands],
        out_specs=pl.BlockSpec((1, ts, d), lambda b, j: (b, j, 0)),
        out_shape=jax.ShapeDtypeStruct(x.shape, F32),
        scratch_shapes=[pltpu.VMEM((CONV_CARRY_ROWS, conv_w.shape[2]), F32)],
        compiler_params=pltpu.CompilerParams(
            dimension_semantics=("arbitrary", "arbitrary"), vmem_limit_bytes=VMEM_LIMIT_BYTES),
        name="channel",
    )(*[a for a, _ in operands])


def kernel(x, c, w_ada, b_ada, g_norm1, w_in, ln_v_g, ln_v_b, w_spatial, b_spatial, w_pool, b_pool,
           pool_scale, w_proj_a, w_proj_b, w_gate, b_gate, w_out, g_norm2, w_up, conv_w, conv_b,
           w_down, g_final):
    depth = w_ada.shape[0]
    bsz, s_len, d = x.shape
    for l in range(depth):
        x, w_up_bf, w_down_bf, mod = _mixer_call(
            x, c, w_ada[l], b_ada[l], g_norm1[l], w_in[l], w_gate[l], b_gate[l], ln_v_g[l], ln_v_b[l], w_spatial[l],
            b_spatial[l], w_pool[l], b_pool[l], pool_scale[l], w_proj_a[l], w_proj_b[l], w_out[l],
            w_up[l], w_down[l])
        x = _channel_call(
            x, mod, g_norm2[l], w_up_bf, conv_w[l:l + 1], conv_b[l], w_down_bf, g_final,
            final_norm=(l == depth - 1))
    return x
```

```python
import functools
import math

import jax
import jax.numpy as jnp
from jax import lax
from jax.experimental import pallas as pl
from jax.experimental.pallas import tpu as pltpu

EPS = 1e-6
CHUNK = 64
GMLP_BLOCK = 128
A_GROUPS = 8
POOL_WINDOWS = (2, 4, 8, 16)
CONV_WIDTH = 3
N_ADA = 6

SUBLANES = 8
LANES = 128
BF16_TILE_ROWS = 16
POOL_HALO = 16
CONV_CARRY_ROWS = (CONV_WIDTH - 1) * SUBLANES
MXU_WIDTH = 256
MIXER_TILE = 256
MIXER_TILES_PER_STEP = 4
MIXER_PREP_STEPS = 16
CHANNEL_TILE = 256
CHANNEL_TILES_PER_STEP = 2
FF_CHUNK = 256
UP_LOOKAHEAD = 5
ADA_TILE = 512
VMEM_LIMIT_BYTES = 56 * 1024 * 1024

BF16 = jnp.bfloat16
F32 = jnp.float32


def _dot(a, b):
    return jnp.dot(a, b, preferred_element_type=F32)


def _dot_ref(a, w_ref):
    n = w_ref.shape[-1]
    assert n % MXU_WIDTH == 0
    return jnp.concatenate(
        [_dot(a, w_ref[:, c0:c0 + MXU_WIDTH]) for c0 in range(0, n, MXU_WIDTH)], axis=1)


def _gelu(x):
    return 0.5 * x * (1.0 + lax.erf(x * math.sqrt(0.5)))


def _rms_scale(x):
    ms = jnp.mean(x * x, axis=-1, keepdims=True)
    return x * lax.rsqrt(ms + EPS)


def _mod_row(mod_ref, b, k, d):
    per = d // ADA_TILE
    return jnp.concatenate([mod_ref[k * per + i, pl.ds(b, 1), :] for i in range(per)], axis=1)


def _cast_block(src_ref, dst_ref):
    n = src_ref.shape[1]
    dst_ref[:, 0:n] = src_ref[...].astype(BF16)
    if dst_ref.shape[1] > n:
        dst_ref[:, n:] = jnp.zeros((dst_ref.shape[0], dst_ref.shape[1] - n), BF16)


def _mixer_kernel(x_ref, c_ref, w_ada_ref, b_ada_ref, g1_ref, b_gate_ref, lng_ref, lnb_ref, wsp_ref, b_sp_ref,
                  wpool_f32, bpool_ref, pscale_ref, w_in_f32, w_gate_f32, wpa_f32, wpb_f32, wout_f32,
                  w_up_ref, w_down_ref, o_ref, w_up_bf_ref, w_down_bf_ref, mod_out_ref, w_ref, wpool_ref, bsp_ref,
                  mod_ref, pool_carry_ref, *, n_prep, n_tiles):
    step = pl.program_id(0)

    @pl.when(step == 0)
    def _():
        wpool_ref[...] = wpool_f32[...].astype(BF16)
        gd = bsp_ref.shape[1] // A_GROUPS
        for g in range(A_GROUPS):
            bsp_ref[:, g * gd:(g + 1) * gd] = jnp.broadcast_to(b_sp_ref[g:g + 1, :], (gd, GMLP_BLOCK)).T

    @pl.when(step < mod_ref.shape[0])
    def _():
        c = c_ref[...]
        s = (c * jax.nn.sigmoid(c)).astype(BF16)
        mod_ref[step] = _dot(s, w_ada_ref[...].astype(BF16)) + b_ada_ref[...]

    @pl.when(step < n_prep)
    def _():
        rows = w_in_f32.shape[0]
        r0 = pl.multiple_of(step * rows, rows)
        off = 0
        for src in (w_in_f32, w_gate_f32, wpa_f32, wpb_f32, wout_f32):
            w_ref[pl.ds(r0, rows), off:off + src.shape[1]] = src[...].astype(BF16)
            off += src.shape[1]

    @pl.when(step >= n_prep)
    def _():
        @pl.when(step == n_prep)
        def _():
            mod_out_ref[...] = mod_ref[...]

        _mixer_tokens(x_ref, mod_ref, g1_ref, b_gate_ref, lng_ref, lnb_ref, wsp_ref, bsp_ref, wpool_ref,
                      bpool_ref, pscale_ref, w_up_ref, w_down_ref, o_ref, w_up_bf_ref, w_down_bf_ref,
                      w_ref, pool_carry_ref, in_width=w_in_f32.shape[1], j=(step - n_prep) % n_tiles,
                      b=(step - n_prep) // n_tiles)


def _mixer_tokens(x_ref, mod_ref, g1_ref, b_gate_ref, lng_ref, lnb_ref, wsp_ref, bsp_ref, wpool_ref,
                  bpool_ref, pscale_ref, w_up_ref, w_down_ref, o_ref, w_up_bf_ref, w_down_bf_ref,
                  w_ref, pool_carry_ref, *, in_width, j, b):
    ts, d = MIXER_TILE, x_ref.shape[2]
    _cast_block(w_up_ref, w_up_bf_ref)
    _cast_block(w_down_ref, w_down_bf_ref)
    w_in_ref = w_ref.at[:, 0:in_width]
    w_gate_ref = w_ref.at[:, in_width:in_width + 2 * d]
    wpa_ref = w_ref.at[:, in_width + 2 * d:in_width + 3 * d]
    wpb_ref = w_ref.at[:, in_width + 3 * d:in_width + 4 * d]
    wout_ref = w_ref.at[:, in_width + 4 * d:in_width + 5 * d]

    @pl.when(j == 0)
    def _():
        pool_carry_ref[...] = jnp.zeros_like(pool_carry_ref)

    sh1, sc1, gt1 = (_mod_row(mod_ref, b, k, d) for k in (0, 1, 2))
    a_w = d
    gd = a_w // A_GROUPS
    p = lax.broadcasted_iota(jnp.int32, (GMLP_BLOCK, GMLP_BLOCK), 0)
    q = lax.broadcasted_iota(jnp.int32, (GMLP_BLOCK, GMLP_BLOCK), 1)
    allowed = (q // CHUNK) <= (p // CHUNK)
    w_masked = [jnp.where(allowed, wsp_ref[g], 0.0).astype(BF16) for g in range(A_GROUPS)]

    def token_tile(row0):
        x = x_ref[0, row0:row0 + ts, :]
        h = _rms_scale(x) * (g1_ref[...] * (1.0 + sc1)) + sh1
        hb = h.astype(BF16)

        v = _gelu(_dot_ref(hb, w_in_ref.at[:, a_w:2 * a_w]))
        hbp = _dot_ref(hb, w_in_ref.at[:, 2 * a_w:w_in_ref.shape[1]])
        gates = jax.nn.sigmoid(_dot_ref(hb, w_gate_ref) + b_gate_ref[...])
        u = _gelu(_dot_ref(hb, w_in_ref.at[:, 0:a_w]))
        mu = jnp.mean(v, axis=-1, keepdims=True)
        vc = v - mu
        var = jnp.mean(vc * vc, axis=-1, keepdims=True)
        vn = (vc * lax.rsqrt(var + EPS)) * lng_ref[...] + lnb_ref[...]
        vnb = vn.astype(BF16)

        ext = jnp.concatenate([pool_carry_ref[...], hbp], axis=0)
        pool_carry_ref[...] = hbp[ts - POOL_HALO:, :]
        bgd = hbp.shape[1] // len(POOL_WINDOWS)
        t = j * x_ref.shape[1] + row0 + lax.broadcasted_iota(jnp.int32, (ts, bgd), 0)
        yb_cols = []
        for gi, w in enumerate(POOL_WINDOWS):
            e = ext[:, gi * bgd:(gi + 1) * bgd]
            acc, span = e, 1
            while span < w:
                acc = acc + pltpu.roll(acc, span, 0)
                span *= 2
            win = acc[POOL_HALO:, :]
            cur = e[POOL_HALO:, :]
            count = jnp.minimum(t + 1, w).astype(F32)
            pooled = win / count - cur
            mixed = _dot(pooled.astype(BF16), wpool_ref[gi]) + bpool_ref[gi:gi + 1, :]
            yb_cols.append(mixed)
        y_b = jnp.concatenate(yb_cols, axis=1) * pscale_ref[...]
        branch_b = gates[:, d:] * _dot_ref(y_b.astype(BF16), wpb_ref)

        bias_map = bsp_ref[...]
        s_rows = []
        for n in range(ts // GMLP_BLOCK):
            b0 = n * GMLP_BLOCK
            cols = [_dot(w_masked[g], vnb[b0:b0 + GMLP_BLOCK, g * gd:(g + 1) * gd]) for g in range(A_GROUPS)]
            s_rows.append(jnp.concatenate(cols, axis=1) + bias_map)
        s = jnp.concatenate(s_rows, axis=0)
        y_a = u * s

        merged = gates[:, :d] * _dot_ref(y_a.astype(BF16), wpa_ref) + branch_b
        x1 = x + gt1 * _dot_ref(merged.astype(BF16), wout_ref)
        o_ref[0, row0:row0 + ts, :] = jnp.concatenate(
            [_interleave_rows(x1[r:r + CHANNEL_TILE, :]) for r in range(0, ts, CHANNEL_TILE)], axis=0)

    for row0 in range(0, x_ref.shape[1], ts):
        token_tile(row0)


def _resident(shape):
    nd = len(shape)
    return pl.BlockSpec(shape, lambda b, j: (0,) * nd, pipeline_mode=pl.Buffered(1))


def _pitch_pad(n_cols):
    return LANES if (n_cols // LANES) % SUBLANES == 0 else 0


def _row_blocking(n_rows, n_steps):
    for steps_per_block in range(1, n_steps + 1):
        if n_steps % steps_per_block == 0 and (n_rows * steps_per_block) % n_steps == 0:
            rows = n_rows * steps_per_block // n_steps
            if rows % BF16_TILE_ROWS == 0:
                return rows, steps_per_block
    raise ValueError(f"no bf16-aligned row blocking of {n_rows} rows over {n_steps} steps")


def _mixer_call(x, c, w_ada, b_ada, g1, w_in, w_gate, b_gate, ln_g, ln_b, w_sp, b_sp, w_pool, b_pool,
                pool_scale, w_pa, w_pb, w_out, w_up, w_down):
    bsz, s_len, d = x.shape
    ts = MIXER_TILE * MIXER_TILES_PER_STEP
    n_tiles = s_len // ts
    n_prep = MIXER_PREP_STEPS
    assert s_len % ts == 0 and MIXER_TILE % GMLP_BLOCK == 0 and MIXER_TILE >= POOL_HALO
    assert MIXER_TILE % CHANNEL_TILE == 0
    assert w_gate.shape[1] == 2 * d and w_pa.shape == w_pb.shape == w_out.shape == (d, d)
    assert d % (n_prep * BF16_TILE_ROWS) == 0
    row = lambda a: a.reshape(1, -1)

    def token_step(t):
        return jnp.maximum(t - n_prep, 0)

    def token_block(t):
        return (token_step(t) // n_tiles, token_step(t) % n_tiles, 0)

    def resident(a):
        nd = a.ndim
        return pl.BlockSpec(a.shape, lambda t: (0,) * nd, pipeline_mode=pl.Buffered(1))

    n_ada = w_ada.shape[1] // ADA_TILE
    assert w_ada.shape[1] == N_ADA * d and d % ADA_TILE == 0 and n_ada <= n_prep
    ada_block = lambda t: (0, jnp.minimum(t, n_ada - 1))
    operands = [
        (x, pl.BlockSpec((1, ts, d), token_block)),
        (c, pl.BlockSpec(c.shape, lambda t: (0, 0))),
        (w_ada, pl.BlockSpec((d, ADA_TILE), ada_block)),
        (row(b_ada), pl.BlockSpec((1, ADA_TILE), ada_block)),
    ]
    for a in (row(g1), row(b_gate), row(ln_g), row(ln_b), w_sp, b_sp, w_pool, b_pool,
              row(pool_scale)):
        operands.append((a, resident(a)))
    mixer_weights = (w_in, w_gate, w_pa, w_pb, w_out)
    for w in mixer_weights:
        operands.append((w, pl.BlockSpec((d // n_prep, w.shape[1]), lambda t: (jnp.minimum(t, n_prep - 1), 0))))
    out_specs = [pl.BlockSpec((1, ts, d), token_block)]
    out_shapes = [jax.ShapeDtypeStruct(x.shape, F32)]
    for w in (w_up, w_down):
        rows, steps_per_block = _row_blocking(w.shape[0], bsz * n_tiles)
        index_map = lambda t, spb=steps_per_block: (token_step(t) // spb, 0)
        operands.append((w, pl.BlockSpec((rows, w.shape[1]), index_map)))
        padded = w.shape[1] + _pitch_pad(w.shape[1])
        out_specs.append(pl.BlockSpec((rows, padded), index_map))
        out_shapes.append(jax.ShapeDtypeStruct((w.shape[0], padded), BF16))
    out_specs.append(pl.BlockSpec((n_ada, bsz, ADA_TILE), lambda t: (0, 0, 0)))
    out_shapes.append(jax.ShapeDtypeStruct((n_ada, bsz, ADA_TILE), F32))
    packed_width = sum(w.shape[1] for w in mixer_weights)
    packed_width += _pitch_pad(packed_width)
    return pl.pallas_call(
        functools.partial(_mixer_kernel, n_prep=n_prep, n_tiles=n_tiles),
        grid=(n_prep + bsz * n_tiles,),
        in_specs=[spec for _, spec in operands],
        out_specs=out_specs,
        out_shape=out_shapes,
        scratch_shapes=[pltpu.VMEM((d, packed_width), BF16),
                        pltpu.VMEM(w_pool.shape, BF16),
                        pltpu.VMEM((GMLP_BLOCK, d), F32),
                        pltpu.VMEM((n_ada, bsz, ADA_TILE), F32),
                        pltpu.VMEM((POOL_HALO, w_pool.shape[0] * w_pool.shape[1]), F32)],
        compiler_params=pltpu.CompilerParams(
            dimension_semantics=("arbitrary",), vmem_limit_bytes=VMEM_LIMIT_BYTES),
        name="mixer",
    )(*[a for a, _ in operands])


def _interleave_rows(x):
    n, d = x.shape
    return jnp.swapaxes(x.reshape(SUBLANES, n // SUBLANES, d), 0, 1).reshape(n, d)


def _deinterleave_rows(y):
    n, d = y.shape
    return jnp.swapaxes(y.reshape(n // SUBLANES, SUBLANES, d), 0, 1).reshape(n, d)


def _delay_rows(a, prev_row):
    n = a.shape[0]
    wrapped = pltpu.roll(a[n - SUBLANES:, :], 1, 0)
    first = jnp.where(lax.broadcasted_iota(jnp.int32, wrapped.shape, 0) == 0, prev_row, wrapped)
    return jnp.concatenate([first, a[:n - SUBLANES, :]], axis=0)


def _channel_kernel(x_ref, mod_ref, w_up_ref, w_down_ref, cw_ref, cb_ref, g2_ref, gf_ref,
                    o_ref, conv_carry_ref, *, final_norm):
    ts, d = CHANNEL_TILE, x_ref.shape[2]
    d_ff = w_down_ref.shape[0]
    n_chunks = d_ff // FF_CHUNK

    @pl.when(pl.program_id(1) == 0)
    def _():
        conv_carry_ref[...] = jnp.zeros_like(conv_carry_ref)

    sh2, sc2, gt2 = (_mod_row(mod_ref, pl.program_id(0), k, d) for k in (3, 4, 5))

    def conv_cols(pre, c0):
        cols = pl.ds(c0, FF_CHUNK)
        taps = [pre]
        for m in range(1, CONV_WIDTH):
            r = (CONV_WIDTH - 1 - m) * SUBLANES + SUBLANES - 1
            taps.append(_delay_rows(taps[-1], conv_carry_ref[r:r + 1, cols]))
        conv_carry_ref[:, cols] = pre[ts - CONV_CARRY_ROWS:, :]
        out = cb_ref[:, cols]
        for k in range(CONV_WIDTH):
            out = out + taps[CONV_WIDTH - 1 - k] * cw_ref[k:k + 1, cols]
        return out

    def token_tile(r0):
        x = x_ref[0, r0:r0 + ts, :]
        h2 = _rms_scale(x) * (g2_ref[...] * (1.0 + sc2)) + sh2
        h2b = h2.astype(BF16)

        def up_project(c):
            return [_dot(h2b, w_up_ref[:, pl.ds(c0, FF_CHUNK)]) for c0 in (c * FF_CHUNK, d_ff + c * FF_CHUNK)]

        ahead = [up_project(i) for i in range(min(UP_LOOKAHEAD, n_chunks))]
        acc = jnp.zeros((ts, d), F32)
        for c in range(n_chunks):
            cur = ahead.pop(0)
            if c + UP_LOOKAHEAD < n_chunks:
                ahead.append(up_project(c + UP_LOOKAHEAD))
            gate = conv_cols(cur[0], c * FF_CHUNK)
            val = conv_cols(cur[1], d_ff + c * FF_CHUNK)
            f = gate * jax.nn.sigmoid(gate) * val
            acc = acc + _dot_ref(f.astype(BF16), w_down_ref.at[c * FF_CHUNK:(c + 1) * FF_CHUNK, 0:d])
        x2 = x + gt2 * acc
        out = _deinterleave_rows(x2)
        if final_norm:
            scale = lax.rsqrt(jnp.mean(x2 * x2, axis=-1, keepdims=True) + EPS)
            scale = _deinterleave_rows(jnp.broadcast_to(scale, (ts, LANES)))[:, 0:1]
            out = out * scale * gf_ref[...]
        o_ref[0, r0:r0 + ts, :] = out

    for r0 in range(0, x_ref.shape[1], ts):
        token_tile(r0)


def _channel_call(x, mod, g2, w_up, conv_w, conv_b, w_down, g_final, final_norm):
    bsz, s_len, d = x.shape
    ts = CHANNEL_TILE * CHANNEL_TILES_PER_STEP
    assert s_len % ts == 0 and CHANNEL_TILE >= CONV_WIDTH * SUBLANES and w_down.shape[0] % FF_CHUNK == 0
    row = lambda a: a.reshape(1, -1)
    operands = [
        (x, pl.BlockSpec((1, ts, d), lambda b, j: (b, j, 0))),
        (mod, _resident(mod.shape)),
    ]
    for a in (w_up, w_down, conv_w, row(conv_b), row(g2), row(g_final)):
        if a is conv_w:
            spec = pl.BlockSpec((None,) + a.shape[1:], lambda b, j: (0, 0, 0), pipeline_mode=pl.Buffered(1))
        else:
            spec = _resident(a.shape)
        operands.append((a, spec))
    return pl.pallas_call(
        functools.partial(_channel_kernel, final_norm=final_norm),
        grid=(bsz, s_len // ts),
        in_specs=[spec for _, spec in operands],
        out_specs=pl.BlockSpec((1, ts, d), lambda b, j: (b, j, 0)),
        out_shape=jax.ShapeDtypeStruct(x.shape, F32),
        scratch_shapes=[pltpu.VMEM((CONV_CARRY_ROWS, conv_w.shape[2]), F32)],
        compiler_params=pltpu.CompilerParams(
            dimension_semantics=("arbitrary", "arbitrary"), vmem_limit_bytes=VMEM_LIMIT_BYTES),
        name="channel",
    )(*[a for a, _ in operands])


def kernel(x, c, w_ada, b_ada, g_norm1, w_in, ln_v_g, ln_v_b, w_spatial, b_spatial, w_pool, b_pool,
           pool_scale, w_proj_a, w_proj_b, w_gate, b_gate, w_out, g_norm2, w_up, conv_w, conv_b,
           w_down, g_final):
    depth = w_ada.shape[0]
    bsz, s_len, d = x.shape
    for l in range(depth):
        x, w_up_bf, w_down_bf, mod = _mixer_call(
            x, c, w_ada[l], b_ada[l], g_norm1[l], w_in[l], w_gate[l], b_gate[l], ln_v_g[l], ln_v_b[l], w_spatial[l],
            b_spatial[l], w_pool[l], b_pool[l], pool_scale[l], w_proj_a[l], w_proj_b[l], w_out[l],
            w_up[l], w_down[l])
        x = _channel_call(
            x, mod, g_norm2[l], w_up_bf, conv_w[l:l + 1], conv_b[l], w_down_bf, g_final,
            final_norm=(l == depth - 1))
    return x
```

```python
import functools
import math

import jax
import jax.numpy as jnp
from jax import lax
from jax.experimental import pallas as pl
from jax.experimental.pallas import tpu as pltpu

EPS = 1e-6
CHUNK = 64
GMLP_BLOCK = 128
A_GROUPS = 8
POOL_WINDOWS = (2, 4, 8, 16)
CONV_WIDTH = 3
N_ADA = 6

SUBLANES = 8
LANES = 128
BF16_TILE_ROWS = 16
POOL_HALO = 16
CONV_CARRY_ROWS = (CONV_WIDTH - 1) * SUBLANES
MXU_WIDTH = 256
MIXER_TILE = 256
MIXER_TILES_PER_STEP = 4
MIXER_PREP_STEPS = 16
CHANNEL_TILE = 256
CHANNEL_TILES_PER_STEP = 2
FF_CHUNK = 256
UP_LOOKAHEAD = 5
ADA_TILE = 512
VMEM_LIMIT_BYTES = 56 * 1024 * 1024

BF16 = jnp.bfloat16
F32 = jnp.float32


def _dot(a, b):
    return jnp.dot(a, b, preferred_element_type=F32)


def _dot_ref(a, w_ref):
    n = w_ref.shape[-1]
    assert n % MXU_WIDTH == 0
    return jnp.concatenate(
        [_dot(a, w_ref[:, c0:c0 + MXU_WIDTH]) for c0 in range(0, n, MXU_WIDTH)], axis=1)


def _gelu(x):
    return 0.5 * x * (1.0 + lax.erf(x * math.sqrt(0.5)))


def _rms_scale(x):
    ms = jnp.mean(x * x, axis=-1, keepdims=True)
    return x * lax.rsqrt(ms + EPS)


def _mod_row(mod_ref, b, k, d):
    per = d // ADA_TILE
    return jnp.concatenate([mod_ref[k * per + i, pl.ds(b, 1), :] for i in range(per)], axis=1)


def _cast_block(src_ref, dst_ref):
    n = src_ref.shape[1]
    dst_ref[:, 0:n] = src_ref[...].astype(BF16)
    if dst_ref.shape[1] > n:
        dst_ref[:, n:] = jnp.zeros((dst_ref.shape[0], dst_ref.shape[1] - n), BF16)


def _mixer_kernel(x_ref, c_ref, w_ada_ref, b_ada_ref, g1_ref, b_gate_ref, lng_ref, lnb_ref, wsp_ref, b_sp_ref,
                  wpool_f32, bpool_ref, pscale_ref, w_in_f32, w_gate_f32, wpa_f32, wpb_f32, wout_f32,
                  w_up_ref, w_down_ref, o_ref, w_up_bf_ref, w_down_bf_ref, mod_out_ref, w_ref, wpool_ref, bsp_ref,
                  mod_ref, pool_carry_ref, *, n_prep, n_tiles):
    step = pl.program_id(0)

    @pl.when(step == 0)
    def _():
        wpool_ref[...] = wpool_f32[...].astype(BF16)
        gd = bsp_ref.shape[1] // A_GROUPS
        for g in range(A_GROUPS):
            bsp_ref[:, g * gd:(g + 1) * gd] = jnp.broadcast_to(b_sp_ref[g:g + 1, :], (gd, GMLP_BLOCK)).T

    @pl.when(step < mod_ref.shape[0])
    def _():
        c = c_ref[...]
        s = (c * jax.nn.sigmoid(c)).astype(BF16)
        mod_ref[step] = _dot(s, w_ada_ref[...].astype(BF16)) + b_ada_ref[...]

    @pl.when(step < n_prep)
    def _():
        rows = w_in_f32.shape[0]
        r0 = pl.multiple_of(step * rows, rows)
        off = 0
        for src in (w_in_f32, w_gate_f32, wpa_f32, wpb_f32, wout_f32):
            w_ref[pl.ds(r0, rows), off:off + src.shape[1]] = src[...].astype(BF16)
            off += src.shape[1]

    @pl.when(step >= n_prep)
    def _():
        @pl.when(step == n_prep)
        def _():
            mod_out_ref[...] = mod_ref[...]

        _mixer_tokens(x_ref, mod_ref, g1_ref, b_gate_ref, lng_ref, lnb_ref, wsp_ref, bsp_ref, wpool_ref,
                      bpool_ref, pscale_ref, w_up_ref, w_down_ref, o_ref, w_up_bf_ref, w_down_bf_ref,
                      w_ref, pool_carry_ref, in_width=w_in_f32.shape[1], j=(step - n_prep) % n_tiles,
                      b=(step - n_prep) // n_tiles)


def _mixer_tokens(x_ref, mod_ref, g1_ref, b_gate_ref, lng_ref, lnb_ref, wsp_ref, bsp_ref, wpool_ref,
                  bpool_ref, pscale_ref, w_up_ref, w_down_ref, o_ref, w_up_bf_ref, w_down_bf_ref,
                  w_ref, pool_carry_ref, *, in_width, j, b):
    ts, d = MIXER_TILE, x_ref.shape[2]
    _cast_block(w_up_ref, w_up_bf_ref)
    _cast_block(w_down_ref, w_down_bf_ref)
    w_in_ref = w_ref.at[:, 0:in_width]
    w_gate_ref = w_ref.at[:, in_width:in_width + 2 * d]
    wpa_ref = w_ref.at[:, in_width + 2 * d:in_width + 3 * d]
    wpb_ref = w_ref.at[:, in_width + 3 * d:in_width + 4 * d]
    wout_ref = w_ref.at[:, in_width + 4 * d:in_width + 5 * d]

    @pl.when(j == 0)
    def _():
        pool_carry_ref[...] = jnp.zeros_like(pool_carry_ref)

    sh1, sc1, gt1 = (_mod_row(mod_ref, b, k, d) for k in (0, 1, 2))
    a_w = d
    gd = a_w // A_GROUPS
    p = lax.broadcasted_iota(jnp.int32, (GMLP_BLOCK, GMLP_BLOCK), 0)
    q = lax.broadcasted_iota(jnp.int32, (GMLP_BLOCK, GMLP_BLOCK), 1)
    allowed = (q // CHUNK) <= (p // CHUNK)
    w_masked = [jnp.where(allowed, wsp_ref[g], 0.0).astype(BF16) for g in range(A_GROUPS)]

    def token_tile(row0):
        x = x_ref[0, row0:row0 + ts, :]
        h = _rms_scale(x) * (g1_ref[...] * (1.0 + sc1)) + sh1
        hb = h.astype(BF16)

        v = _gelu(_dot_ref(hb, w_in_ref.at[:, a_w:2 * a_w]))
        hbp = _dot_ref(hb, w_in_ref.at[:, 2 * a_w:w_in_ref.shape[1]])
        gates = jax.nn.sigmoid(_dot_ref(hb, w_gate_ref) + b_gate_ref[...])
        u = _gelu(_dot_ref(hb, w_in_ref.at[:, 0:a_w]))
        mu = jnp.mean(v, axis=-1, keepdims=True)
        vc = v - mu
        var = jnp.mean(vc * vc, axis=-1, keepdims=True)
        vn = (vc * lax.rsqrt(var + EPS)) * lng_ref[...] + lnb_ref[...]
        vnb = vn.astype(BF16)

        ext = jnp.concatenate([pool_carry_ref[...], hbp], axis=0)
        pool_carry_ref[...] = hbp[ts - POOL_HALO:, :]
        bgd = hbp.shape[1] // len(POOL_WINDOWS)
        t = j * x_ref.shape[1] + row0 + lax.broadcasted_iota(jnp.int32, (ts, bgd), 0)
        yb_cols = []
        for gi, w in enumerate(POOL_WINDOWS):
            e = ext[:, gi * bgd:(gi + 1) * bgd]
            acc, span = e, 1
            while span < w:
                acc = acc + pltpu.roll(acc, span, 0)
                span *= 2
            win = acc[POOL_HALO:, :]
            cur = e[POOL_HALO:, :]
            count = jnp.minimum(t + 1, w).astype(F32)
            pooled = win / count - cur
            mixed = _dot(pooled.astype(BF16), wpool_ref[gi]) + bpool_ref[gi:gi + 1, :]
            yb_cols.append(mixed)
        y_b = jnp.concatenate(yb_cols, axis=1) * pscale_ref[...]
        branch_b = gates[:, d:] * _dot_ref(y_b.astype(BF16), wpb_ref)

        bias_map = bsp_ref[...]
        s_rows = []
        for n in range(ts // GMLP_BLOCK):
            b0 = n * GMLP_BLOCK
            cols = [_dot(w_masked[g], vnb[b0:b0 + GMLP_BLOCK, g * gd:(g + 1) * gd]) for g in range(A_GROUPS)]
            s_rows.append(jnp.concatenate(cols, axis=1) + bias_map)
        s = jnp.concatenate(s_rows, axis=0)
        y_a = u * s

        merged = gates[:, :d] * _dot_ref(y_a.astype(BF16), wpa_ref) + branch_b
        x1 = x + gt1 * _dot_ref(merged.astype(BF16), wout_ref)
        o_ref[0, row0:row0 + ts, :] = jnp.concatenate(
            [_interleave_rows(x1[r:r + CHANNEL_TILE, :]) for r in range(0, ts, CHANNEL_TILE)], axis=0)

    for row0 in range(0, x_ref.shape[1], ts):
        token_tile(row0)


def _resident(shape):
    nd = len(shape)
    return pl.BlockSpec(shape, lambda b, j: (0,) * nd, pipeline_mode=pl.Buffered(1))


def _pitch_pad(n_cols):
    return LANES if (n_cols // LANES) % SUBLANES == 0 else 0


def _row_blocking(n_rows, n_steps):
    for steps_per_block in range(1, n_steps + 1):
        if n_steps % steps_per_block == 0 and (n_rows * steps_per_block) % n_steps == 0:
            rows = n_rows * steps_per_block // n_steps
            if rows % BF16_TILE_ROWS == 0:
                return rows, steps_per_block
    raise ValueError(f"no bf16-aligned row blocking of {n_rows} rows over {n_steps} steps")


def _mixer_call(x, c, w_ada, b_ada, g1, w_in, w_gate, b_gate, ln_g, ln_b, w_sp, b_sp, w_pool, b_pool,
                pool_scale, w_pa, w_pb, w_out, w_up, w_down):
    bsz, s_len, d = x.shape
    ts = MIXER_TILE * MIXER_TILES_PER_STEP
    n_tiles = s_len // ts
    n_prep = MIXER_PREP_STEPS
    assert s_len % ts == 0 and MIXER_TILE % GMLP_BLOCK == 0 and MIXER_TILE >= POOL_HALO
    assert MIXER_TILE % CHANNEL_TILE == 0
    assert w_gate.shape[1] == 2 * d and w_pa.shape == w_pb.shape == w_out.shape == (d, d)
    assert d % (n_prep * BF16_TILE_ROWS) == 0
    row = lambda a: a.reshape(1, -1)

    def token_step(t):
        return jnp.maximum(t - n_prep, 0)

    def token_block(t):
        return (token_step(t) // n_tiles, token_step(t) % n_tiles, 0)

    def resident(a):
        nd = a.ndim
        return pl.BlockSpec(a.shape, lambda t: (0,) * nd, pipeline_mode=pl.Buffered(1))

    n_ada = w_ada.shape[1] // ADA_TILE
    assert w_ada.shape[1] == N_ADA * d and d % ADA_TILE == 0 and n_ada <= n_prep
    ada_block = lambda t: (0, jnp.minimum(t, n_ada - 1))
    operands = [
        (x, pl.BlockSpec((1, ts, d), token_block)),
        (c, pl.BlockSpec(c.shape, lambda t: (0, 0))),
        (w_ada, pl.BlockSpec((d, ADA_TILE), ada_block)),
        (row(b_ada), pl.BlockSpec((1, ADA_TILE), ada_block)),
    ]
    for a in (row(g1), row(b_gate), row(ln_g), row(ln_b), w_sp, b_sp, w_pool, b_pool,
              row(pool_scale)):
        operands.append((a, resident(a)))
    mixer_weights = (w_in, w_gate, w_pa, w_pb, w_out)
    for w in mixer_weights:
        operands.append((w, pl.BlockSpec((d // n_prep, w.shape[1]), lambda t: (jnp.minimum(t, n_prep - 1), 0))))
    out_specs = [pl.BlockSpec((1, ts, d), token_block)]
    out_shapes = [jax.ShapeDtypeStruct(x.shape, F32)]
    for w in (w_up, w_down):
        rows, steps_per_block = _row_blocking(w.shape[0], bsz * n_tiles)
        index_map = lambda t, spb=steps_per_block: (token_step(t) // spb, 0)
        operands.append((w, pl.BlockSpec((rows, w.shape[1]), index_map)))
        padded = w.shape[1] + _pitch_pad(w.shape[1])
        out_specs.append(pl.BlockSpec((rows, padded), index_map))
        out_shapes.append(jax.ShapeDtypeStruct((w.shape[0], padded), BF16))
    out_specs.append(pl.BlockSpec((n_ada, bsz, ADA_TILE), lambda t: (0, 0, 0)))
    out_shapes.append(jax.ShapeDtypeStruct((n_ada, bsz, ADA_TILE), F32))
    packed_width = sum(w.shape[1] for w in mixer_weights)
    packed_width += _pitch_pad(packed_width)
    return pl.pallas_call(
        functools.partial(_mixer_kernel, n_prep=n_prep, n_tiles=n_tiles),
        grid=(n_prep + bsz * n_tiles,),
        in_specs=[spec for _, spec in operands],
        out_specs=out_specs,
        out_shape=out_shapes,
        scratch_shapes=[pltpu.VMEM((d, packed_width), BF16),
                        pltpu.VMEM(w_pool.shape, BF16),
                        pltpu.VMEM((GMLP_BLOCK, d), F32),
                        pltpu.VMEM((n_ada, bsz, ADA_TILE), F32),
                        pltpu.VMEM((POOL_HALO, w_pool.shape[0] * w_pool.shape[1]), F32)],
        compiler_params=pltpu.CompilerParams(
            dimension_semantics=("arbitrary",), vmem_limit_bytes=VMEM_LIMIT_BYTES),
        name="mixer",
    )(*[a for a, _ in operands])


def _interleave_rows(x):
    n, d = x.shape
    return jnp.swapaxes(x.reshape(SUBLANES, n // SUBLANES, d), 0, 1).reshape(n, d)


def _deinterleave_rows(y):
    n, d = y.shape
    return jnp.swapaxes(y.reshape(n // SUBLANES, SUBLANES, d), 0, 1).reshape(n, d)


def _delay_rows(a, prev_row):
    n = a.shape[0]
    wrapped = pltpu.roll(a[n - SUBLANES:, :], 1, 0)
    first = jnp.where(lax.broadcasted_iota(jnp.int32, wrapped.shape, 0) == 0, prev_row, wrapped)
    return jnp.concatenate([first, a[:n - SUBLANES, :]], axis=0)


def _channel_kernel(x_ref, mod_ref, w_up_ref, w_down_ref, cw_ref, cb_ref, g2_ref, gf_ref,
                    o_ref, conv_carry_ref, *, final_norm):
    ts, d = CHANNEL_TILE, x_ref.shape[2]
    d_ff = w_down_ref.shape[0]
    n_chunks = d_ff // FF_CHUNK

    @pl.when(pl.program_id(1) == 0)
    def _():
        conv_carry_ref[...] = jnp.zeros_like(conv_carry_ref)

    sh2, sc2, gt2 = (_mod_row(mod_ref, pl.program_id(0), k, d) for k in (3, 4, 5))

    def conv_cols(pre, c0):
        cols = pl.ds(c0, FF_CHUNK)
        taps = [pre]
        for m in range(1, CONV_WIDTH):
            r = (CONV_WIDTH - 1 - m) * SUBLANES + SUBLANES - 1
            taps.append(_delay_rows(taps[-1], conv_carry_ref[r:r + 1, cols]))
        conv_carry_ref[:, cols] = pre[ts - CONV_CARRY_ROWS:, :]
        out = cb_ref[:, cols]
        for k in range(CONV_WIDTH):
            out = out + taps[CONV_WIDTH - 1 - k] * cw_ref[k:k + 1, cols]
        return out

    def token_tile(r0):
        x = x_ref[0, r0:r0 + ts, :]
        h2 = _rms_scale(x) * (g2_ref[...] * (1.0 + sc2)) + sh2
        h2b = h2.astype(BF16)

        def up_project(c):
            return [_dot(h2b, w_up_ref[:, pl.ds(c0, FF_CHUNK)]) for c0 in (c * FF_CHUNK, d_ff + c * FF_CHUNK)]

        ahead = [up_project(i) for i in range(min(UP_LOOKAHEAD, n_chunks))]
        def down_project(c, fb):
            return _dot_ref(fb, w_down_ref.at[c * FF_CHUNK:(c + 1) * FF_CHUNK, 0:d])

        acc = jnp.zeros((ts, d), F32)
        pending = None
        for c in range(n_chunks):
            cur = ahead.pop(0)
            if c + UP_LOOKAHEAD < n_chunks:
                ahead.append(up_project(c + UP_LOOKAHEAD))
            gate = conv_cols(cur[0], c * FF_CHUNK)
            val = conv_cols(cur[1], d_ff + c * FF_CHUNK)
            fb = (gate * jax.nn.sigmoid(gate) * val).astype(BF16)
            if pending is not None:
                acc = acc + down_project(*pending)
            pending = (c, fb)
        acc = acc + down_project(*pending)
        x2 = x + gt2 * acc
        if final_norm:
            x2 = _rms_scale(x2) * gf_ref[...]
        o_ref[0, r0:r0 + ts, :] = _deinterleave_rows(x2)

    for r0 in range(0, x_ref.shape[1], ts):
        token_tile(r0)


def _channel_call(x, mod, g2, w_up, conv_w, conv_b, w_down, g_final, final_norm):
    bsz, s_len, d = x.shape
    ts = CHANNEL_TILE * CHANNEL_TILES_PER_STEP
    assert s_len % ts == 0 and CHANNEL_TILE >= CONV_WIDTH * SUBLANES and w_down.shape[0] % FF_CHUNK == 0
    row = lambda a: a.reshape(1, -1)
    operands = [
        (x, pl.BlockSpec((1, ts, d), lambda b, j: (b, j, 0))),
        (mod, _resident(mod.shape)),
    ]
    for a in (w_up, w_down, conv_w, row(conv_b), row(g2), row(g_final)):
        if a is conv_w:
            spec = pl.BlockSpec((None,) + a.shape[1:], lambda b, j: (0, 0, 0), pipeline_mode=pl.Buffered(1))
        else:
            spec = _resident(a.shape)
        operands.append((a, spec))
    return pl.pallas_call(
        functools.partial(_channel_kernel, final_norm=final_norm),
        grid=(bsz, s_len // ts),
        in_specs=[spec for _, spec in operands],
        out_specs=pl.BlockSpec((1, ts, d), lambda b, j: (b, j, 0)),
        out_shape=jax.ShapeDtypeStruct(x.shape, F32),
        scratch_shapes=[pltpu.VMEM((CONV_CARRY_ROWS, conv_w.shape[2]), F32)],
        compiler_params=pltpu.CompilerParams(
            dimension_semantics=("arbitrary", "arbitrary"), vmem_limit_bytes=VMEM_LIMIT_BYTES),
        name="channel",
    )(*[a for a, _ in operands])


def kernel(x, c, w_ada, b_ada, g_norm1, w_in, ln_v_g, ln_v_b, w_spatial, b_spatial, w_pool, b_pool,
           pool_scale, w_proj_a, w_proj_b, w_gate, b_gate, w_out, g_norm2, w_up, conv_w, conv_b,
           w_down, g_final):
    depth = w_ada.shape[0]
    bsz, s_len, d = x.shape
    for l in range(depth):
        x, w_up_bf, w_down_bf, mod = _mixer_call(
            x, c, w_ada[l], b_ada[l], g_norm1[l], w_in[l], w_gate[l], b_gate[l], ln_v_g[l], ln_v_b[l], w_spatial[l],
            b_spatial[l], w_pool[l], b_pool[l], pool_scale[l], w_proj_a[l], w_proj_b[l], w_out[l],
            w_up[l], w_down[l])
        x = _channel_call(
            x, mod, g_norm2[l], w_up_bf, conv_w[l:l + 1], conv_b[l], w_down_bf, g_final,
            final_norm=(l == depth - 1))
    return x
```

```python
import functools
import math

import jax
import jax.numpy as jnp
from jax import lax
from jax.experimental import pallas as pl
from jax.experimental.pallas import tpu as pltpu

EPS = 1e-6
CHUNK = 64
GMLP_BLOCK = 128
A_GROUPS = 8
POOL_WINDOWS = (2, 4, 8, 16)
CONV_WIDTH = 3
N_ADA = 6

SUBLANES = 8
LANES = 128
BF16_TILE_ROWS = 16
POOL_HALO = 16
CONV_CARRY_ROWS = (CONV_WIDTH - 1) * SUBLANES
MXU_WIDTH = 256
MIXER_TILE = 256
MIXER_TILES_PER_STEP = 4
MIXER_PREP_STEPS = 16
CHANNEL_TILE = 256
CHANNEL_TILES_PER_STEP = 2
FF_CHUNK = 256
UP_LOOKAHEAD = 5
ADA_TILE = 512
VMEM_LIMIT_BYTES = 56 * 1024 * 1024

BF16 = jnp.bfloat16
F32 = jnp.float32


def _dot(a, b):
    return jnp.dot(a, b, preferred_element_type=F32)


def _dot_ref(a, w_ref):
    n = w_ref.shape[-1]
    assert n % MXU_WIDTH == 0
    return jnp.concatenate(
        [_dot(a, w_ref[:, c0:c0 + MXU_WIDTH]) for c0 in range(0, n, MXU_WIDTH)], axis=1)


def _gelu(x):
    return 0.5 * x * (1.0 + lax.erf(x * math.sqrt(0.5)))


def _rms_scale(x):
    ms = jnp.mean(x * x, axis=-1, keepdims=True)
    return x * lax.rsqrt(ms + EPS)


def _mod_row(mod_ref, b, k, d):
    per = d // ADA_TILE
    return jnp.concatenate([mod_ref[k * per + i, pl.ds(b, 1), :] for i in range(per)], axis=1)


def _cast_block(src_ref, dst_ref):
    n = src_ref.shape[1]
    dst_ref[:, 0:n] = src_ref[...].astype(BF16)
    if dst_ref.shape[1] > n:
        dst_ref[:, n:] = jnp.zeros((dst_ref.shape[0], dst_ref.shape[1] - n), BF16)


def _mixer_kernel(x_ref, c_ref, w_ada_ref, b_ada_ref, g1_ref, b_gate_ref, lng_ref, lnb_ref, wsp_ref, b_sp_ref,
                  wpool_f32, bpool_ref, pscale_ref, w_in_f32, w_gate_f32, wpa_f32, wpb_f32, wout_f32,
                  w_up_ref, w_down_ref, o_ref, w_up_bf_ref, w_down_bf_ref, mod_out_ref, w_ref, wpool_ref, bsp_ref,
                  mod_ref, pool_carry_ref, *, n_prep, n_tiles):
    step = pl.program_id(0)

    @pl.when(step == 0)
    def _():
        wpool_ref[...] = wpool_f32[...].astype(BF16)
        gd = bsp_ref.shape[1] // A_GROUPS
        for g in range(A_GROUPS):
            bsp_ref[:, g * gd:(g + 1) * gd] = jnp.broadcast_to(b_sp_ref[g:g + 1, :], (gd, GMLP_BLOCK)).T

    @pl.when(step < mod_ref.shape[0])
    def _():
        c = c_ref[...]
        s = (c * jax.nn.sigmoid(c)).astype(BF16)
        mod_ref[step] = _dot(s, w_ada_ref[...].astype(BF16)) + b_ada_ref[...]

    @pl.when(step < n_prep)
    def _():
        rows = w_in_f32.shape[0]
        r0 = pl.multiple_of(step * rows, rows)
        off = 0
        for src in (w_in_f32, w_gate_f32, wpa_f32, wpb_f32, wout_f32):
            w_ref[pl.ds(r0, rows), off:off + src.shape[1]] = src[...].astype(BF16)
            off += src.shape[1]

    @pl.when(step >= n_prep)
    def _():
        @pl.when(step == n_prep)
        def _():
            mod_out_ref[...] = mod_ref[...]

        _mixer_tokens(x_ref, mod_ref, g1_ref, b_gate_ref, lng_ref, lnb_ref, wsp_ref, bsp_ref, wpool_ref,
                      bpool_ref, pscale_ref, w_up_ref, w_down_ref, o_ref, w_up_bf_ref, w_down_bf_ref,
                      w_ref, pool_carry_ref, in_width=w_in_f32.shape[1], j=(step - n_prep) % n_tiles,
                      b=(step - n_prep) // n_tiles)


def _mixer_tokens(x_ref, mod_ref, g1_ref, b_gate_ref, lng_ref, lnb_ref, wsp_ref, bsp_ref, wpool_ref,
                  bpool_ref, pscale_ref, w_up_ref, w_down_ref, o_ref, w_up_bf_ref, w_down_bf_ref,
                  w_ref, pool_carry_ref, *, in_width, j, b):
    ts, d = MIXER_TILE, x_ref.shape[2]
    d_ff = w_up_ref.shape[1] // 2
    for c0 in range(0, d_ff, FF_CHUNK):
        w_up_bf_ref[:, 2 * c0:2 * c0 + FF_CHUNK] = w_up_ref[:, c0:c0 + FF_CHUNK].astype(BF16)
        w_up_bf_ref[:, 2 * c0 + FF_CHUNK:2 * (c0 + FF_CHUNK)] = w_up_ref[:, d_ff + c0:d_ff + c0 + FF_CHUNK].astype(BF16)
    if w_up_bf_ref.shape[1] > 2 * d_ff:
        w_up_bf_ref[:, 2 * d_ff:] = jnp.zeros((w_up_bf_ref.shape[0], w_up_bf_ref.shape[1] - 2 * d_ff), BF16)
    _cast_block(w_down_ref, w_down_bf_ref)
    w_in_ref = w_ref.at[:, 0:in_width]
    w_gate_ref = w_ref.at[:, in_width:in_width + 2 * d]
    wpa_ref = w_ref.at[:, in_width + 2 * d:in_width + 3 * d]
    wpb_ref = w_ref.at[:, in_width + 3 * d:in_width + 4 * d]
    wout_ref = w_ref.at[:, in_width + 4 * d:in_width + 5 * d]

    @pl.when(j == 0)
    def _():
        pool_carry_ref[...] = jnp.zeros_like(pool_carry_ref)

    sh1, sc1, gt1 = (_mod_row(mod_ref, b, k, d) for k in (0, 1, 2))
    a_w = d
    gd = a_w // A_GROUPS
    p = lax.broadcasted_iota(jnp.int32, (GMLP_BLOCK, GMLP_BLOCK), 0)
    q = lax.broadcasted_iota(jnp.int32, (GMLP_BLOCK, GMLP_BLOCK), 1)
    allowed = (q // CHUNK) <= (p // CHUNK)
    w_masked = [jnp.where(allowed, wsp_ref[g], 0.0).astype(BF16) for g in range(A_GROUPS)]

    def token_tile(row0):
        x = x_ref[0, row0:row0 + ts, :]
        h = _rms_scale(x) * (g1_ref[...] * (1.0 + sc1)) + sh1
        hb = h.astype(BF16)

        v = _gelu(_dot_ref(hb, w_in_ref.at[:, a_w:2 * a_w]))
        hbp = _dot_ref(hb, w_in_ref.at[:, 2 * a_w:w_in_ref.shape[1]])
        gates = jax.nn.sigmoid(_dot_ref(hb, w_gate_ref) + b_gate_ref[...])
        u = _gelu(_dot_ref(hb, w_in_ref.at[:, 0:a_w]))
        mu = jnp.mean(v, axis=-1, keepdims=True)
        vc = v - mu
        var = jnp.mean(vc * vc, axis=-1, keepdims=True)
        vn = (vc * lax.rsqrt(var + EPS)) * lng_ref[...] + lnb_ref[...]
        vnb = vn.astype(BF16)

        ext = jnp.concatenate([pool_carry_ref[...], hbp], axis=0)
        pool_carry_ref[...] = hbp[ts - POOL_HALO:, :]
        bgd = hbp.shape[1] // len(POOL_WINDOWS)
        t = j * x_ref.shape[1] + row0 + lax.broadcasted_iota(jnp.int32, (ts, bgd), 0)
        yb_cols = []
        for gi, w in enumerate(POOL_WINDOWS):
            e = ext[:, gi * bgd:(gi + 1) * bgd]
            acc, span = e, 1
            while span < w:
                acc = acc + pltpu.roll(acc, span, 0)
                span *= 2
            win = acc[POOL_HALO:, :]
            cur = e[POOL_HALO:, :]
            count = jnp.minimum(t + 1, w).astype(F32)
            pooled = win / count - cur
            mixed = _dot(pooled.astype(BF16), wpool_ref[gi]) + bpool_ref[gi:gi + 1, :]
            yb_cols.append(mixed)
        y_b = jnp.concatenate(yb_cols, axis=1) * pscale_ref[...]
        branch_b = gates[:, d:] * _dot_ref(y_b.astype(BF16), wpb_ref)

        bias_map = bsp_ref[...]
        s_rows = []
        for n in range(ts // GMLP_BLOCK):
            b0 = n * GMLP_BLOCK
            cols = [_dot(w_masked[g], vnb[b0:b0 + GMLP_BLOCK, g * gd:(g + 1) * gd]) for g in range(A_GROUPS)]
            s_rows.append(jnp.concatenate(cols, axis=1) + bias_map)
        s = jnp.concatenate(s_rows, axis=0)
        y_a = u * s

        merged = gates[:, :d] * _dot_ref(y_a.astype(BF16), wpa_ref) + branch_b
        x1 = x + gt1 * _dot_ref(merged.astype(BF16), wout_ref)
        o_ref[0, row0:row0 + ts, :] = jnp.concatenate(
            [_interleave_rows(x1[r:r + CHANNEL_TILE, :]) for r in range(0, ts, CHANNEL_TILE)], axis=0)

    for row0 in range(0, x_ref.shape[1], ts):
        token_tile(row0)


def _resident(shape):
    nd = len(shape)
    return pl.BlockSpec(shape, lambda b, j: (0,) * nd, pipeline_mode=pl.Buffered(1))


def _pitch_pad(n_cols):
    return LANES if (n_cols // LANES) % SUBLANES == 0 else 0


def _row_blocking(n_rows, n_steps):
    for steps_per_block in range(1, n_steps + 1):
        if n_steps % steps_per_block == 0 and (n_rows * steps_per_block) % n_steps == 0:
            rows = n_rows * steps_per_block // n_steps
            if rows % BF16_TILE_ROWS == 0:
                return rows, steps_per_block
    raise ValueError(f"no bf16-aligned row blocking of {n_rows} rows over {n_steps} steps")


def _mixer_call(x, c, w_ada, b_ada, g1, w_in, w_gate, b_gate, ln_g, ln_b, w_sp, b_sp, w_pool, b_pool,
                pool_scale, w_pa, w_pb, w_out, w_up, w_down):
    bsz, s_len, d = x.shape
    ts = MIXER_TILE * MIXER_TILES_PER_STEP
    n_tiles = s_len // ts
    n_prep = MIXER_PREP_STEPS
    assert s_len % ts == 0 and MIXER_TILE % GMLP_BLOCK == 0 and MIXER_TILE >= POOL_HALO
    assert MIXER_TILE % CHANNEL_TILE == 0
    assert w_gate.shape[1] == 2 * d and w_pa.shape == w_pb.shape == w_out.shape == (d, d)
    assert d % (n_prep * BF16_TILE_ROWS) == 0
    row = lambda a: a.reshape(1, -1)

    def token_step(t):
        return jnp.maximum(t - n_prep, 0)

    def token_block(t):
        return (token_step(t) // n_tiles, token_step(t) % n_tiles, 0)

    def resident(a):
        nd = a.ndim
        return pl.BlockSpec(a.shape, lambda t: (0,) * nd, pipeline_mode=pl.Buffered(1))

    n_ada = w_ada.shape[1] // ADA_TILE
    assert w_ada.shape[1] == N_ADA * d and d % ADA_TILE == 0 and n_ada <= n_prep
    ada_block = lambda t: (0, jnp.minimum(t, n_ada - 1))
    operands = [
        (x, pl.BlockSpec((1, ts, d), token_block)),
        (c, pl.BlockSpec(c.shape, lambda t: (0, 0))),
        (w_ada, pl.BlockSpec((d, ADA_TILE), ada_block)),
        (row(b_ada), pl.BlockSpec((1, ADA_TILE), ada_block)),
    ]
    for a in (row(g1), row(b_gate), row(ln_g), row(ln_b), w_sp, b_sp, w_pool, b_pool,
              row(pool_scale)):
        operands.append((a, resident(a)))
    mixer_weights = (w_in, w_gate, w_pa, w_pb, w_out)
    for w in mixer_weights:
        operands.append((w, pl.BlockSpec((d // n_prep, w.shape[1]), lambda t: (jnp.minimum(t, n_prep - 1), 0))))
    out_specs = [pl.BlockSpec((1, ts, d), token_block)]
    out_shapes = [jax.ShapeDtypeStruct(x.shape, F32)]
    for w in (w_up, w_down):
        rows, steps_per_block = _row_blocking(w.shape[0], bsz * n_tiles)
        index_map = lambda t, spb=steps_per_block: (token_step(t) // spb, 0)
        operands.append((w, pl.BlockSpec((rows, w.shape[1]), index_map)))
        padded = w.shape[1] + _pitch_pad(w.shape[1])
        out_specs.append(pl.BlockSpec((rows, padded), index_map))
        out_shapes.append(jax.ShapeDtypeStruct((w.shape[0], padded), BF16))
    out_specs.append(pl.BlockSpec((n_ada, bsz, ADA_TILE), lambda t: (0, 0, 0)))
    out_shapes.append(jax.ShapeDtypeStruct((n_ada, bsz, ADA_TILE), F32))
    packed_width = sum(w.shape[1] for w in mixer_weights)
    packed_width += _pitch_pad(packed_width)
    return pl.pallas_call(
        functools.partial(_mixer_kernel, n_prep=n_prep, n_tiles=n_tiles),
        grid=(n_prep + bsz * n_tiles,),
        in_specs=[spec for _, spec in operands],
        out_specs=out_specs,
        out_shape=out_shapes,
        scratch_shapes=[pltpu.VMEM((d, packed_width), BF16),
                        pltpu.VMEM(w_pool.shape, BF16),
                        pltpu.VMEM((GMLP_BLOCK, d), F32),
                        pltpu.VMEM((n_ada, bsz, ADA_TILE), F32),
                        pltpu.VMEM((POOL_HALO, w_pool.shape[0] * w_pool.shape[1]), F32)],
        compiler_params=pltpu.CompilerParams(
            dimension_semantics=("arbitrary",), vmem_limit_bytes=VMEM_LIMIT_BYTES),
        name="mixer",
    )(*[a for a, _ in operands])


def _interleave_rows(x):
    n, d = x.shape
    return jnp.swapaxes(x.reshape(SUBLANES, n // SUBLANES, d), 0, 1).reshape(n, d)


def _deinterleave_rows(y):
    n, d = y.shape
    return jnp.swapaxes(y.reshape(n // SUBLANES, SUBLANES, d), 0, 1).reshape(n, d)


def _delay_rows(a, prev_row):
    n = a.shape[0]
    wrapped = pltpu.roll(a[n - SUBLANES:, :], 1, 0)
    first = jnp.where(lax.broadcasted_iota(jnp.int32, wrapped.shape, 0) == 0, prev_row, wrapped)
    return jnp.concatenate([first, a[:n - SUBLANES, :]], axis=0)


def _channel_kernel(x_ref, mod_ref, w_up_ref, w_down_ref, cw_ref, cb_ref, g2_ref, gf_ref,
                    o_ref, conv_carry_ref, *, final_norm):
    ts, d = CHANNEL_TILE, x_ref.shape[2]
    d_ff = w_down_ref.shape[0]
    n_chunks = d_ff // FF_CHUNK

    @pl.when(pl.program_id(1) == 0)
    def _():
        conv_carry_ref[...] = jnp.zeros_like(conv_carry_ref)

    sh2, sc2, gt2 = (_mod_row(mod_ref, pl.program_id(0), k, d) for k in (3, 4, 5))

    def conv_cols(pre, c0):
        cols = pl.ds(c0, FF_CHUNK)
        taps = [pre]
        for m in range(1, CONV_WIDTH):
            r = (CONV_WIDTH - 1 - m) * SUBLANES + SUBLANES - 1
            taps.append(_delay_rows(taps[-1], conv_carry_ref[r:r + 1, cols]))
        conv_carry_ref[:, cols] = pre[ts - CONV_CARRY_ROWS:, :]
        out = cb_ref[:, cols]
        for k in range(CONV_WIDTH):
            out = out + taps[CONV_WIDTH - 1 - k] * cw_ref[k:k + 1, cols]
        return out

    def token_tile(r0):
        x = x_ref[0, r0:r0 + ts, :]
        h2 = _rms_scale(x) * (g2_ref[...] * (1.0 + sc2)) + sh2
        h2b = h2.astype(BF16)

        def up_project(c):
            both = _dot(h2b, w_up_ref[:, pl.ds(2 * c * FF_CHUNK, 2 * FF_CHUNK)])
            return [both[:, :FF_CHUNK], both[:, FF_CHUNK:]]

        ahead = [up_project(i) for i in range(min(UP_LOOKAHEAD, n_chunks))]
        acc = jnp.zeros((ts, d), F32)
        for c in range(n_chunks):
            cur = ahead.pop(0)
            if c + UP_LOOKAHEAD < n_chunks:
                ahead.append(up_project(c + UP_LOOKAHEAD))
            gate = conv_cols(cur[0], c * FF_CHUNK)
            val = conv_cols(cur[1], d_ff + c * FF_CHUNK)
            f = gate * jax.nn.sigmoid(gate) * val
            acc = acc + _dot_ref(f.astype(BF16), w_down_ref.at[c * FF_CHUNK:(c + 1) * FF_CHUNK, 0:d])
        x2 = x + gt2 * acc
        if final_norm:
            x2 = _rms_scale(x2) * gf_ref[...]
        o_ref[0, r0:r0 + ts, :] = _deinterleave_rows(x2)

    for r0 in range(0, x_ref.shape[1], ts):
        token_tile(r0)


def _channel_call(x, mod, g2, w_up, conv_w, conv_b, w_down, g_final, final_norm):
    bsz, s_len, d = x.shape
    ts = CHANNEL_TILE * CHANNEL_TILES_PER_STEP
    assert s_len % ts == 0 and CHANNEL_TILE >= CONV_WIDTH * SUBLANES and w_down.shape[0] % FF_CHUNK == 0
    row = lambda a: a.reshape(1, -1)
    operands = [
        (x, pl.BlockSpec((1, ts, d), lambda b, j: (b, j, 0))),
        (mod, _resident(mod.shape)),
    ]
    for a in (w_up, w_down, conv_w, row(conv_b), row(g2), row(g_final)):
        if a is conv_w:
            spec = pl.BlockSpec((None,) + a.shape[1:], lambda b, j: (0, 0, 0), pipeline_mode=pl.Buffered(1))
        else:
            spec = _resident(a.shape)
        operands.append((a, spec))
    return pl.pallas_call(
        functools.partial(_channel_kernel, final_norm=final_norm),
        grid=(bsz, s_len // ts),
        in_specs=[spec for _, spec in operands],
        out_specs=pl.BlockSpec((1, ts, d), lambda b, j: (b, j, 0)),
        out_shape=jax.ShapeDtypeStruct(x.shape, F32),
        scratch_shapes=[pltpu.VMEM((CONV_CARRY_ROWS, conv_w.shape[2]), F32)],
        compiler_params=pltpu.CompilerParams(
            dimension_semantics=("arbitrary", "arbitrary"), vmem_limit_bytes=VMEM_LIMIT_BYTES),
        name="channel",
    )(*[a for a, _ in operands])


def kernel(x, c, w_ada, b_ada, g_norm1, w_in, ln_v_g, ln_v_b, w_spatial, b_spatial, w_pool, b_pool,
           pool_scale, w_proj_a, w_proj_b, w_gate, b_gate, w_out, g_norm2, w_up, conv_w, conv_b,
           w_down, g_final):
    depth = w_ada.shape[0]
    bsz, s_len, d = x.shape
    for l in range(depth):
        x, w_up_bf, w_down_bf, mod = _mixer_call(
            x, c, w_ada[l], b_ada[l], g_norm1[l], w_in[l], w_gate[l], b_gate[l], ln_v_g[l], ln_v_b[l], w_spatial[l],
            b_spatial[l], w_pool[l], b_pool[l], pool_scale[l], w_proj_a[l], w_proj_b[l], w_out[l],
            w_up[l], w_down[l])
        x = _channel_call(
            x, mod, g_norm2[l], w_up_bf, conv_w[l:l + 1], conv_b[l], w_down_bf, g_final,
            final_norm=(l == depth - 1))
    return x
```

```python
import functools
import math

import jax
import jax.numpy as jnp
from jax import lax
from jax.experimental import pallas as pl
from jax.experimental.pallas import tpu as pltpu

EPS = 1e-6
CHUNK = 64
GMLP_BLOCK = 128
A_GROUPS = 8
POOL_WINDOWS = (2, 4, 8, 16)
CONV_WIDTH = 3
N_ADA = 6

SUBLANES = 8
LANES = 128
BF16_TILE_ROWS = 16
POOL_HALO = 16
CONV_CARRY_ROWS = (CONV_WIDTH - 1) * SUBLANES
MXU_WIDTH = 256
MIXER_TILE = 256
MIXER_TILES_PER_STEP = 4
MIXER_PREP_STEPS = 16
CHANNEL_TILE = 256
CHANNEL_TILES_PER_STEP = 2
FF_CHUNK = 256
UP_LOOKAHEAD = 5
ADA_TILE = 512
VMEM_LIMIT_BYTES = 56 * 1024 * 1024

BF16 = jnp.bfloat16
F32 = jnp.float32


def _dot(a, b):
    return jnp.dot(a, b, preferred_element_type=F32)


def _dot_ref(a, w_ref):
    n = w_ref.shape[-1]
    assert n % MXU_WIDTH == 0
    return jnp.concatenate(
        [_dot(a, w_ref[:, c0:c0 + MXU_WIDTH]) for c0 in range(0, n, MXU_WIDTH)], axis=1)


def _gelu(x):
    return 0.5 * x * (1.0 + lax.erf(x * math.sqrt(0.5)))


def _rms_scale(x):
    ms = jnp.mean(x * x, axis=-1, keepdims=True)
    return x * lax.rsqrt(ms + EPS)


def _mod_row(mod_ref, b, k, d):
    per = d // ADA_TILE
    return jnp.concatenate([mod_ref[k * per + i, pl.ds(b, 1), :] for i in range(per)], axis=1)


def _cast_block(src_ref, dst_ref):
    n = src_ref.shape[1]
    dst_ref[:, 0:n] = src_ref[...].astype(BF16)
    if dst_ref.shape[1] > n:
        dst_ref[:, n:] = jnp.zeros((dst_ref.shape[0], dst_ref.shape[1] - n), BF16)


def _mixer_kernel(x_ref, c_ref, w_ada_ref, b_ada_ref, g1_ref, b_gate_ref, lng_ref, lnb_ref, wsp_ref, b_sp_ref,
                  wpool_f32, bpool_ref, pscale_ref, w_in_f32, w_gate_f32, wpa_f32, wpb_f32, wout_f32,
                  w_up_ref, w_down_ref, o_ref, w_up_bf_ref, w_down_bf_ref, mod_out_ref, w_ref, wpool_ref, bsp_ref,
                  mod_ref, pool_carry_ref, *, n_prep, n_tiles):
    step = pl.program_id(0)

    @pl.when(step == 0)
    def _():
        wpool_ref[...] = wpool_f32[...].astype(BF16)
        gd = bsp_ref.shape[1] // A_GROUPS
        for g in range(A_GROUPS):
            bsp_ref[:, g * gd:(g + 1) * gd] = jnp.broadcast_to(b_sp_ref[g:g + 1, :], (gd, GMLP_BLOCK)).T

    @pl.when(step < mod_ref.shape[0])
    def _():
        c = c_ref[...]
        s = (c * jax.nn.sigmoid(c)).astype(BF16)
        mod_ref[step] = _dot(s, w_ada_ref[...].astype(BF16)) + b_ada_ref[...]

    @pl.when(step < n_prep)
    def _():
        rows = w_in_f32.shape[0]
        r0 = pl.multiple_of(step * rows, rows)
        off = 0
        for src in (w_in_f32, w_gate_f32, wpa_f32, wpb_f32, wout_f32):
            w_ref[pl.ds(r0, rows), off:off + src.shape[1]] = src[...].astype(BF16)
            off += src.shape[1]

    @pl.when(step >= n_prep)
    def _():
        @pl.when(step == n_prep)
        def _():
            mod_out_ref[...] = mod_ref[...]

        _mixer_tokens(x_ref, mod_ref, g1_ref, b_gate_ref, lng_ref, lnb_ref, wsp_ref, bsp_ref, wpool_ref,
                      bpool_ref, pscale_ref, w_up_ref, w_down_ref, o_ref, w_up_bf_ref, w_down_bf_ref,
                      w_ref, pool_carry_ref, in_width=w_in_f32.shape[1], j=(step - n_prep) % n_tiles,
                      b=(step - n_prep) // n_tiles)


def _mixer_tokens(x_ref, mod_ref, g1_ref, b_gate_ref, lng_ref, lnb_ref, wsp_ref, bsp_ref, wpool_ref,
                  bpool_ref, pscale_ref, w_up_ref, w_down_ref, o_ref, w_up_bf_ref, w_down_bf_ref,
                  w_ref, pool_carry_ref, *, in_width, j, b):
    ts, d = MIXER_TILE, x_ref.shape[2]
    _cast_block(w_up_ref, w_up_bf_ref)
    _cast_block(w_down_ref, w_down_bf_ref)
    w_in_ref = w_ref.at[:, 0:in_width]
    w_gate_ref = w_ref.at[:, in_width:in_width + 2 * d]
    wpa_ref = w_ref.at[:, in_width + 2 * d:in_width + 3 * d]
    wpb_ref = w_ref.at[:, in_width + 3 * d:in_width + 4 * d]
    wout_ref = w_ref.at[:, in_width + 4 * d:in_width + 5 * d]

    @pl.when(j == 0)
    def _():
        pool_carry_ref[...] = jnp.zeros_like(pool_carry_ref)

    sh1, sc1, gt1 = (_mod_row(mod_ref, b, k, d) for k in (0, 1, 2))
    a_w = d
    gd = a_w // A_GROUPS
    p = lax.broadcasted_iota(jnp.int32, (GMLP_BLOCK, GMLP_BLOCK), 0)
    q = lax.broadcasted_iota(jnp.int32, (GMLP_BLOCK, GMLP_BLOCK), 1)
    allowed = (q // CHUNK) <= (p // CHUNK)
    w_masked = [jnp.where(allowed, wsp_ref[g], 0.0).astype(BF16) for g in range(A_GROUPS)]

    def token_tile(row0):
        x = x_ref[0, row0:row0 + ts, :]
        h = _rms_scale(x) * (g1_ref[...] * (1.0 + sc1)) + sh1
        hb = h.astype(BF16)

        v = _gelu(_dot_ref(hb, w_in_ref.at[:, a_w:2 * a_w]))
        hbp = _dot_ref(hb, w_in_ref.at[:, 2 * a_w:w_in_ref.shape[1]])
        gates = jax.nn.sigmoid(_dot_ref(hb, w_gate_ref) + b_gate_ref[...])
        u = _gelu(_dot_ref(hb, w_in_ref.at[:, 0:a_w]))
        mu = jnp.mean(v, axis=-1, keepdims=True)
        vc = v - mu
        var = jnp.mean(vc * vc, axis=-1, keepdims=True)
        vn = (vc * lax.rsqrt(var + EPS)) * lng_ref[...] + lnb_ref[...]
        vnb = vn.astype(BF16)

        ext = jnp.concatenate([pool_carry_ref[...], hbp], axis=0)
        pool_carry_ref[...] = hbp[ts - POOL_HALO:, :]
        bgd = hbp.shape[1] // len(POOL_WINDOWS)
        t = j * x_ref.shape[1] + row0 + lax.broadcasted_iota(jnp.int32, (ts, bgd), 0)
        yb_cols = []
        for gi, w in enumerate(POOL_WINDOWS):
            e = ext[:, gi * bgd:(gi + 1) * bgd]
            acc, span = e, 1
            while span < w:
                acc = acc + pltpu.roll(acc, span, 0)
                span *= 2
            win = acc[POOL_HALO:, :]
            cur = e[POOL_HALO:, :]
            count = jnp.minimum(t + 1, w).astype(F32)
            pooled = win / count - cur
            mixed = _dot(pooled.astype(BF16), wpool_ref[gi]) + bpool_ref[gi:gi + 1, :]
            yb_cols.append(mixed)
        y_b = jnp.concatenate(yb_cols, axis=1) * pscale_ref[...]
        branch_b = gates[:, d:] * _dot_ref(y_b.astype(BF16), wpb_ref)

        bias_map = bsp_ref[...]
        s_rows = []
        for n in range(ts // GMLP_BLOCK):
            b0 = n * GMLP_BLOCK
            cols = [_dot(w_masked[g], vnb[b0:b0 + GMLP_BLOCK, g * gd:(g + 1) * gd]) for g in range(A_GROUPS)]
            s_rows.append(jnp.concatenate(cols, axis=1) + bias_map)
        s = jnp.concatenate(s_rows, axis=0)
        y_a = u * s

        merged = gates[:, :d] * _dot_ref(y_a.astype(BF16), wpa_ref) + branch_b
        x1 = x + gt1 * _dot_ref(merged.astype(BF16), wout_ref)
        o_ref[0, row0:row0 + ts, :] = jnp.concatenate(
            [_interleave_rows(x1[r:r + CHANNEL_TILE, :]) for r in range(0, ts, CHANNEL_TILE)], axis=0)

    for row0 in range(0, x_ref.shape[1], ts):
        token_tile(row0)


def _resident(shape):
    nd = len(shape)
    return pl.BlockSpec(shape, lambda b, j: (0,) * nd, pipeline_mode=pl.Buffered(1))


def _pitch_pad(n_cols):
    return LANES if (n_cols // LANES) % SUBLANES == 0 else 0


def _row_blocking(n_rows, n_steps):
    for steps_per_block in range(1, n_steps + 1):
        if n_steps % steps_per_block == 0 and (n_rows * steps_per_block) % n_steps == 0:
            rows = n_rows * steps_per_block // n_steps
            if rows % BF16_TILE_ROWS == 0:
                return rows, steps_per_block
    raise ValueError(f"no bf16-aligned row blocking of {n_rows} rows over {n_steps} steps")


def _mixer_call(x, c, w_ada, b_ada, g1, w_in, w_gate, b_gate, ln_g, ln_b, w_sp, b_sp, w_pool, b_pool,
                pool_scale, w_pa, w_pb, w_out, w_up, w_down):
    bsz, s_len, d = x.shape
    ts = MIXER_TILE * MIXER_TILES_PER_STEP
    n_tiles = s_len // ts
    n_prep = MIXER_PREP_STEPS
    assert s_len % ts == 0 and MIXER_TILE % GMLP_BLOCK == 0 and MIXER_TILE >= POOL_HALO
    assert MIXER_TILE % CHANNEL_TILE == 0
    assert w_gate.shape[1] == 2 * d and w_pa.shape == w_pb.shape == w_out.shape == (d, d)
    assert d % (n_prep * BF16_TILE_ROWS) == 0
    row = lambda a: a.reshape(1, -1)

    def token_step(t):
        return jnp.maximum(t - n_prep, 0)

    def token_block(t):
        return (token_step(t) // n_tiles, token_step(t) % n_tiles, 0)

    def resident(a):
        nd = a.ndim
        return pl.BlockSpec(a.shape, lambda t: (0,) * nd, pipeline_mode=pl.Buffered(1))

    n_ada = w_ada.shape[1] // ADA_TILE
    assert w_ada.shape[1] == N_ADA * d and d % ADA_TILE == 0 and n_ada <= n_prep
    ada_block = lambda t: (0, jnp.minimum(t, n_ada - 1))
    operands = [
        (x, pl.BlockSpec((1, ts, d), token_block)),
        (c, pl.BlockSpec(c.shape, lambda t: (0, 0))),
        (w_ada, pl.BlockSpec((d, ADA_TILE), ada_block)),
        (row(b_ada), pl.BlockSpec((1, ADA_TILE), ada_block)),
    ]
    for a in (row(g1), row(b_gate), row(ln_g), row(ln_b), w_sp, b_sp, w_pool, b_pool,
              row(pool_scale)):
        operands.append((a, resident(a)))
    mixer_weights = (w_in, w_gate, w_pa, w_pb, w_out)
    for w in mixer_weights:
        operands.append((w, pl.BlockSpec((d // n_prep, w.shape[1]), lambda t: (jnp.minimum(t, n_prep - 1), 0))))
    out_specs = [pl.BlockSpec((1, ts, d), token_block)]
    out_shapes = [jax.ShapeDtypeStruct(x.shape, F32)]
    for w in (w_up, w_down):
        rows, steps_per_block = _row_blocking(w.shape[0], bsz * n_tiles)
        index_map = lambda t, spb=steps_per_block: (token_step(t) // spb, 0)
        operands.append((w, pl.BlockSpec((rows, w.shape[1]), index_map)))
        padded = w.shape[1] + _pitch_pad(w.shape[1])
        out_specs.append(pl.BlockSpec((rows, padded), index_map))
        out_shapes.append(jax.ShapeDtypeStruct((w.shape[0], padded), BF16))
    out_specs.append(pl.BlockSpec((n_ada, bsz, ADA_TILE), lambda t: (0, 0, 0)))
    out_shapes.append(jax.ShapeDtypeStruct((n_ada, bsz, ADA_TILE), F32))
    packed_width = sum(w.shape[1] for w in mixer_weights)
    packed_width += _pitch_pad(packed_width)
    return pl.pallas_call(
        functools.partial(_mixer_kernel, n_prep=n_prep, n_tiles=n_tiles),
        grid=(n_prep + bsz * n_tiles,),
        in_specs=[spec for _, spec in operands],
        out_specs=out_specs,
        out_shape=out_shapes,
        scratch_shapes=[pltpu.VMEM((d, packed_width), BF16),
                        pltpu.VMEM(w_pool.shape, BF16),
                        pltpu.VMEM((GMLP_BLOCK, d), F32),
                        pltpu.VMEM((n_ada, bsz, ADA_TILE), F32),
                        pltpu.VMEM((POOL_HALO, w_pool.shape[0] * w_pool.shape[1]), F32)],
        compiler_params=pltpu.CompilerParams(
            dimension_semantics=("arbitrary",), vmem_limit_bytes=VMEM_LIMIT_BYTES),
        name="mixer",
    )(*[a for a, _ in operands])


def _interleave_rows(x):
    n, d = x.shape
    return jnp.swapaxes(x.reshape(SUBLANES, n // SUBLANES, d), 0, 1).reshape(n, d)


def _deinterleave_rows(y):
    n, d = y.shape
    return jnp.swapaxes(y.reshape(n // SUBLANES, SUBLANES, d), 0, 1).reshape(n, d)


def _delay_rows(a, prev_row):
    n = a.shape[0]
    wrapped = pltpu.roll(a[n - SUBLANES:, :], 1, 0)
    first = jnp.where(lax.broadcasted_iota(jnp.int32, wrapped.shape, 0) == 0, prev_row, wrapped)
    return jnp.concatenate([first, a[:n - SUBLANES, :]], axis=0)


def _channel_kernel(x_ref, mod_ref, w_up_ref, w_down_ref, cw_ref, cb_ref, g2_ref, gf_ref,
                    o_ref, conv_carry_ref, acc_ref, *, final_norm):
    ts, d = CHANNEL_TILE, x_ref.shape[2]
    d_ff = w_down_ref.shape[0]
    n_chunks = d_ff // FF_CHUNK

    @pl.when(pl.program_id(1) == 0)
    def _():
        conv_carry_ref[...] = jnp.zeros_like(conv_carry_ref)

    sh2, sc2, gt2 = (_mod_row(mod_ref, pl.program_id(0), k, d) for k in (3, 4, 5))

    def conv_cols(pre, c0):
        cols = pl.ds(c0, FF_CHUNK)
        taps = [pre]
        for m in range(1, CONV_WIDTH):
            r = (CONV_WIDTH - 1 - m) * SUBLANES + SUBLANES - 1
            taps.append(_delay_rows(taps[-1], conv_carry_ref[r:r + 1, cols]))
        conv_carry_ref[:, cols] = pre[ts - CONV_CARRY_ROWS:, :]
        out = cb_ref[:, cols]
        for k in range(CONV_WIDTH):
            out = out + taps[CONV_WIDTH - 1 - k] * cw_ref[k:k + 1, cols]
        return out

    def token_tile(r0):
        x = x_ref[0, r0:r0 + ts, :]
        h2 = _rms_scale(x) * (g2_ref[...] * (1.0 + sc2)) + sh2
        h2b = h2.astype(BF16)

        def up_project(c):
            return [_dot(h2b, w_up_ref[:, pl.ds(c0, FF_CHUNK)]) for c0 in (c * FF_CHUNK, d_ff + c * FF_CHUNK)]

        ahead = [up_project(i) for i in range(min(UP_LOOKAHEAD, n_chunks))]
        for c in range(n_chunks):
            cur = ahead.pop(0)
            if c + UP_LOOKAHEAD < n_chunks:
                ahead.append(up_project(c + UP_LOOKAHEAD))
            gate = conv_cols(cur[0], c * FF_CHUNK)
            val = conv_cols(cur[1], d_ff + c * FF_CHUNK)
            f = gate * jax.nn.sigmoid(gate) * val
            part = _dot_ref(f.astype(BF16), w_down_ref.at[c * FF_CHUNK:(c + 1) * FF_CHUNK, 0:d])
            if c == 0:
                acc_ref[...] = part
            else:
                acc_ref[...] += part
        x2 = x + gt2 * acc_ref[...]
        if final_norm:
            x2 = _rms_scale(x2) * gf_ref[...]
        o_ref[0, r0:r0 + ts, :] = _deinterleave_rows(x2)

    for r0 in range(0, x_ref.shape[1], ts):
        token_tile(r0)


def _channel_call(x, mod, g2, w_up, conv_w, conv_b, w_down, g_final, final_norm):
    bsz, s_len, d = x.shape
    ts = CHANNEL_TILE * CHANNEL_TILES_PER_STEP
    assert s_len % ts == 0 and CHANNEL_TILE >= CONV_WIDTH * SUBLANES and w_down.shape[0] % FF_CHUNK == 0
    row = lambda a: a.reshape(1, -1)
    operands = [
        (x, pl.BlockSpec((1, ts, d), lambda b, j: (b, j, 0))),
        (mod, _resident(mod.shape)),
    ]
    for a in (w_up, w_down, conv_w, row(conv_b), row(g2), row(g_final)):
        if a is conv_w:
            spec = pl.BlockSpec((None,) + a.shape[1:], lambda b, j: (0, 0, 0), pipeline_mode=pl.Buffered(1))
        else:
            spec = _resident(a.shape)
        operands.append((a, spec))
    return pl.pallas_call(
        functools.partial(_channel_kernel, final_norm=final_norm),
        grid=(bsz, s_len // ts),
        in_specs=[spec for _, spec in operands],
        out_specs=pl.BlockSpec((1, ts, d), lambda b, j: (b, j, 0)),
        out_shape=jax.ShapeDtypeStruct(x.shape, F32),
        scratch_shapes=[pltpu.VMEM((CONV_CARRY_ROWS, conv_w.shape[2]), F32),
                        pltpu.VMEM((CHANNEL_TILE, d), F32)],
        compiler_params=pltpu.CompilerParams(
            dimension_semantics=("arbitrary", "arbitrary"), vmem_limit_bytes=VMEM_LIMIT_BYTES),
        name="channel",
    )(*[a for a, _ in operands])


def kernel(x, c, w_ada, b_ada, g_norm1, w_in, ln_v_g, ln_v_b, w_spatial, b_spatial, w_pool, b_pool,
           pool_scale, w_proj_a, w_proj_b, w_gate, b_gate, w_out, g_norm2, w_up, conv_w, conv_b,
           w_down, g_final):
    depth = w_ada.shape[0]
    bsz, s_len, d = x.shape
    for l in range(depth):
        x, w_up_bf, w_down_bf, mod = _mixer_call(
            x, c, w_ada[l], b_ada[l], g_norm1[l], w_in[l], w_gate[l], b_gate[l], ln_v_g[l], ln_v_b[l], w_spatial[l],
            b_spatial[l], w_pool[l], b_pool[l], pool_scale[l], w_proj_a[l], w_proj_b[l], w_out[l],
            w_up[l], w_down[l])
        x = _channel_call(
            x, mod, g_norm2[l], w_up_bf, conv_w[l:l + 1], conv_b[l], w_down_bf, g_final,
            final_norm=(l == depth - 1))
    return x
```
